```python
import math
import jax, jax.numpy as jnp
from jax import lax
import numpy as np

D_MODEL = 1024
BATCH = 16
SEQ = 4096
DEPTH = 1

N_META = 16
ROPE_THETA = 10000.0
NORM_EPS = 1e-5
DIFF_HEADS = 4
DIFF_HEAD_DIM = 64
DIFF_WIDTH = DIFF_HEADS * 2 * DIFF_HEAD_DIM
Q_BLOCK = 128
RET_HEADS = 4
RET_KEY_DIM = 64
RET_VALUE_DIM = 2 * RET_KEY_DIM
RET_WIDTH = RET_HEADS * RET_VALUE_DIM
RET_CHUNK = 128
MIX_WIDTH = DIFF_WIDTH + RET_WIDTH
IN_PROJ_SIZES = (2 * DIFF_HEADS * DIFF_HEAD_DIM, 2 * DIFF_HEADS * DIFF_HEAD_DIM, DIFF_WIDTH,
                 RET_HEADS * RET_KEY_DIM, RET_HEADS * RET_KEY_DIM, RET_WIDTH, RET_WIDTH)
IN_PROJ_WIDTH = sum(IN_PROJ_SIZES)
N_EXPERTS = 32
TOP_K = 4
D_FF = D_MODEL
SWIGLU_ALPHA = 1.702
SWIGLU_LIMIT = 7.0
MOE_BLOCK = 512

kernel_name = "hymba_diffattn_retnet_gptoss_moe"


def rms_norm(x, g=None, eps=NORM_EPS):
    x32 = x.astype(jnp.float32)
    y = (x32 * lax.rsqrt(jnp.mean(x32 * x32, axis=-1, keepdims=True) + eps)).astype(x.dtype)
    return y if g is None else y * g


def rope(t, pos, inv_freq):
    ang = pos[:, None] * inv_freq[None, :]
    cos = jnp.cos(ang)[None, :, None, :].astype(t.dtype)
    sin = jnp.sin(ang)[None, :, None, :].astype(t.dtype)
    t1, t2 = jnp.split(t, 2, axis=-1)
    return jnp.concatenate([t1 * cos - t2 * sin, t2 * cos + t1 * sin], axis=-1)


def diff_attention(q, k, v, lam, subln_g, lambda_init):
    B_, L_ = q.shape[0], q.shape[1]
    q = jnp.swapaxes(q, 1, 2) * (DIFF_HEAD_DIM ** -0.5)
    k = jnp.swapaxes(k, 1, 2)
    v = jnp.swapaxes(v, 1, 2)
    bounds = [0, N_META] + list(range(N_META + Q_BLOCK, L_, Q_BLOCK)) + [L_]
    outs = []
    for s, e in zip(bounds[:-1], bounds[1:]):
        scores = jnp.einsum('bhqd,bhkd->bhqk', q[:, :, s:e], k[:, :, :e]).astype(jnp.float32)
        causal = jnp.arange(e)[None, :] <= jnp.arange(s, e)[:, None]
        probs = jax.nn.softmax(jnp.where(causal, scores, -jnp.inf), axis=-1)
        probs = probs.reshape(B_, DIFF_HEADS, 2, e - s, e)
        diff = (probs[:, :, 0] - lam * probs[:, :, 1]).astype(v.dtype)
        outs.append(jnp.einsum('bhqk,bhkd->bhqd', diff, v[:, :, :e]))
    o = jnp.concatenate(outs, axis=2)
    o = rms_norm(o, subln_g) * (1.0 - lambda_init)
    return jnp.swapaxes(o, 1, 2).reshape(B_, L_, DIFF_WIDTH)


def retention(q, k, v, gate):
    B_, L_ = q.shape[0], q.shape[1]
    pad = (-L_) % RET_CHUNK
    n_chunks = (L_ + pad) // RET_CHUNK
    k = k * (RET_KEY_DIM ** -0.5)

    def to_chunks(t):
        t = jnp.pad(t, ((0, 0), (pad, 0), (0, 0), (0, 0)))
        return t.reshape(B_, n_chunks, RET_CHUNK, RET_HEADS, t.shape[-1]).transpose(0, 3, 1, 2, 4)

    qc, kc, vc = to_chunks(q), to_chunks(k), to_chunks(v)
    log_gamma = jnp.log1p(-jnp.exp2(-5.0 - jnp.arange(RET_HEADS, dtype=jnp.float32)))
    pos = jnp.arange(RET_CHUNK, dtype=jnp.float32)
    rel = pos[:, None] - pos[None, :]
    decay_intra = jnp.where(rel >= 0, jnp.exp(log_gamma[:, None, None] * jnp.maximum(rel, 0.0)), 0.0)
    key_decay = jnp.exp(log_gamma[:, None] * (RET_CHUNK - 1 - pos)[None, :])
    query_decay = jnp.exp(log_gamma[:, None] * (pos + 1.0)[None, :])
    chunk_decay = jnp.exp(log_gamma * RET_CHUNK)
    decay_intra, key_decay, query_decay, chunk_decay = (
        t.astype(q.dtype) for t in (decay_intra, key_decay, query_decay, chunk_decay))

    scores = jnp.einsum('bhnid,bhnjd->bhnij', qc, kc) * decay_intra[:, None]
    y_intra = jnp.einsum('bhnij,bhnje->bhnie', scores, vc)
    kv = jnp.einsum('bhnjd,hj,bhnje->nbhde', kc, key_decay, vc)

    def step(state, kv_n):
        return chunk_decay[None, :, None, None] * state + kv_n, state

    _, s_prev = lax.scan(step, jnp.zeros(kv.shape[1:], kv.dtype), kv)
    y_cross = jnp.einsum('bhnid,hi,nbhde->bhnie', qc, query_decay, s_prev)
    y = (y_intra + y_cross).transpose(0, 2, 3, 1, 4).reshape(B_, L_ + pad, RET_HEADS, RET_VALUE_DIM)
    y = rms_norm(y[:, pad:]).reshape(B_, L_, RET_WIDTH)
    return y * jax.nn.silu(gate)


def clamped_swiglu(h):
    glu, lin = h[..., ::2], h[..., 1::2]
    glu = jnp.minimum(glu, SWIGLU_LIMIT)
    lin = jnp.clip(lin, -SWIGLU_LIMIT, SWIGLU_LIMIT)
    return glu * jax.nn.sigmoid(SWIGLU_ALPHA * glu) * (lin + 1.0)


def moe_ffn(h, w_router, b_router, w_up, b_up, w_down, b_down):
    B_, L_, D_ = h.shape
    T = B_ * L_
    xt = h.reshape(T, D_)
    logits = (xt @ w_router + b_router).astype(jnp.float32)
    top_val, top_idx = lax.top_k(logits, TOP_K)
    gates = jax.nn.softmax(top_val, axis=-1).astype(h.dtype)
    n_assign = T * TOP_K
    flat_e = top_idx.reshape(-1)
    flat_tok = jnp.arange(n_assign, dtype=jnp.int32) // TOP_K
    order = jnp.argsort(flat_e)
    sorted_e = flat_e[order]
    counts = jnp.bincount(flat_e, length=N_EXPERTS)
    padded = (counts + MOE_BLOCK - 1) // MOE_BLOCK * MOE_BLOCK
    pad_end = jnp.cumsum(padded)
    pad_start = pad_end - padded
    grp_start = jnp.cumsum(counts) - counts
    dest = pad_start[sorted_e] + (jnp.arange(n_assign) - grp_start[sorted_e])
    n_blocks = -(-(n_assign + N_EXPERTS * (MOE_BLOCK - 1)) // MOE_BLOCK)
    n_rows = n_blocks * MOE_BLOCK
    row_tok = jnp.zeros((n_rows,), jnp.int32).at[dest].set(flat_tok[order])
    row_gate = jnp.zeros((n_rows,), h.dtype).at[dest].set(gates.reshape(-1)[order])
    block_e = jnp.minimum(jnp.searchsorted(pad_end, jnp.arange(n_blocks) * MOE_BLOCK, side='right'),
                          N_EXPERTS - 1)

    def block_ffn(args):
        tok, g, e = args
        hb = clamped_swiglu(xt[tok] @ w_up[e] + b_up[e])
        return (hb @ w_down[e] + b_down[e]) * g[:, None]

    ys = lax.map(block_ffn, (row_tok.reshape(n_blocks, MOE_BLOCK),
                             row_gate.reshape(n_blocks, MOE_BLOCK), block_e))
    out = jax.ops.segment_sum(ys.reshape(n_rows, D_), row_tok, num_segments=T)
    return out.reshape(B_, L_, D_)


def setup_inputs(seed: int = 0) -> dict:
    key = jax.random.key(seed)
    ks = jax.random.split(key, 15)
    nrm = jax.random.normal
    return {
        "x": nrm(ks[0], (BATCH, SEQ, D_MODEL), jnp.float32),
        "meta_tokens": nrm(ks[1], (N_META, D_MODEL), jnp.float32),
        "attn_norm_g": 1.0 + 0.02 * nrm(ks[2], (DEPTH, D_MODEL), jnp.float32),
        "w_in": nrm(ks[3], (DEPTH, D_MODEL, IN_PROJ_WIDTH), jnp.float32) * D_MODEL ** -0.5,
        "diff_lambda": 0.1 * nrm(ks[4], (DEPTH, 4, DIFF_HEAD_DIM), jnp.float32),
        "diff_subln_g": 1.0 + 0.02 * nrm(ks[5], (DEPTH, 2 * DIFF_HEAD_DIM), jnp.float32),
        "w_out": nrm(ks[6], (DEPTH, MIX_WIDTH, D_MODEL), jnp.float32) * MIX_WIDTH ** -0.5,
        "ffn_norm_g": 1.0 + 0.02 * nrm(ks[7], (DEPTH, D_MODEL), jnp.float32),
        "w_router": nrm(ks[8], (DEPTH, D_MODEL, N_EXPERTS), jnp.float32) * D_MODEL ** -0.5,
        "b_router": 0.01 * nrm(ks[9], (DEPTH, N_EXPERTS), jnp.float32),
        "w_up": nrm(ks[10], (DEPTH, N_EXPERTS, D_MODEL, 2 * D_FF), jnp.float32) * D_MODEL ** -0.5,
        "b_up": 0.01 * nrm(ks[11], (DEPTH, N_EXPERTS, 2 * D_FF), jnp.float32),
        "w_down": nrm(ks[12], (DEPTH, N_EXPERTS, D_FF, D_MODEL), jnp.float32) * D_FF ** -0.5,
        "b_down": 0.01 * nrm(ks[13], (DEPTH, N_EXPERTS, D_MODEL), jnp.float32),
        "final_norm_g": 1.0 + 0.02 * nrm(ks[14], (D_MODEL,), jnp.float32),
    }


def reference(x, meta_tokens, attn_norm_g, w_in, diff_lambda, diff_subln_g, w_out, ffn_norm_g,
              w_router, b_router, w_up, b_up, w_down, b_down, final_norm_g):
    B_ = x.shape[0]
    h = jnp.concatenate([jnp.broadcast_to(meta_tokens[None].astype(x.dtype), (B_, N_META, D_MODEL)), x], axis=1)
    L_ = h.shape[1]
    pos = jnp.arange(L_, dtype=jnp.float32)
    diff_inv_freq = ROPE_THETA ** (-jnp.arange(0, DIFF_HEAD_DIM, 2, dtype=jnp.float32) / DIFF_HEAD_DIM)
    ret_inv_freq = ROPE_THETA ** (-jnp.linspace(0.0, 1.0, RET_KEY_DIM // 2, dtype=jnp.float32))
    split_at = np.cumsum(IN_PROJ_SIZES)[:-1].tolist()
    for l in range(DEPTH):
        hn = rms_norm(h, attn_norm_g[l])
        proj = hn @ w_in[l]
        dq, dk, dv, rq, rk, rv, rg = jnp.split(proj, split_at, axis=-1)
        dq = rope(dq.reshape(B_, L_, 2 * DIFF_HEADS, DIFF_HEAD_DIM), pos, diff_inv_freq)
        dk = rope(dk.reshape(B_, L_, 2 * DIFF_HEADS, DIFF_HEAD_DIM), pos, diff_inv_freq)
        dv = dv.reshape(B_, L_, DIFF_HEADS, 2 * DIFF_HEAD_DIM)
        lambda_init = 0.8 - 0.6 * math.exp(-0.3 * l)
        lp = diff_lambda[l].astype(jnp.float32)
        lam = jnp.exp(jnp.sum(lp[0] * lp[1])) - jnp.exp(jnp.sum(lp[2] * lp[3])) + lambda_init
        diff_out = diff_attention(dq, dk, dv, lam, diff_subln_g[l], lambda_init)
        rq = rope(rq.reshape(B_, L_, RET_HEADS, RET_KEY_DIM), pos, ret_inv_freq)
        rk = rope(rk.reshape(B_, L_, RET_HEADS, RET_KEY_DIM), pos, ret_inv_freq)
        rv = rv.reshape(B_, L_, RET_HEADS, RET_VALUE_DIM)
        ret_out = retention(rq, rk, rv, rg)
        h = h + jnp.concatenate([diff_out, ret_out], axis=-1) @ w_out[l]
        h = h + moe_ffn(rms_norm(h, ffn_norm_g[l]), w_router[l], b_router[l],
                        w_up[l], b_up[l], w_down[l], b_down[l])
    return rms_norm(h, final_norm_g)[:, N_META:]
```

```python
import functools
import math

import jax
import jax.numpy as jnp
import numpy as np
from jax import lax
from jax.experimental import pallas as pl
from jax.experimental.pallas import tpu as pltpu

F32 = jnp.float32
BF16 = jnp.bfloat16

D_MODEL = 1024
N_META = 16
ROPE_THETA = 10000.0
NORM_EPS = 1e-5
DIFF_HEADS = 4
HEAD_DIM = 64
VALUE_DIM = 128
DIFF_WIDTH = DIFF_HEADS * VALUE_DIM
RET_HEADS = 4
RET_WIDTH = RET_HEADS * VALUE_DIM
QK_WIDTH = RET_HEADS * HEAD_DIM
IN_PROJ_WIDTH = 3 * DIFF_WIDTH + 2 * QK_WIDTH + 2 * RET_WIDTH
LAMBDA_INIT = 0.8 - 0.6 * math.exp(-0.3 * 0)
N_EXPERTS = 32
TOP_K = 4
D_FF = D_MODEL
SWIGLU_ALPHA = 1.702
SWIGLU_LIMIT = 7.0

LANES = 128
META_ROWS = 128
VMEM_LIMIT = 56 * 1024 * 1024

PROJ_TILE = 512
ATTN_TILE = 256
RET_CHUNK = 256
ROUTE_TILE = 512
ROW_TILE = 256
MOE_TILE = 512

_LOG_GAMMA = [float(v) for v in np.log1p(-np.exp2(-5.0 - np.arange(RET_HEADS, dtype=np.float32)))]


def _params(sem):
    return pltpu.CompilerParams(dimension_semantics=sem, vmem_limit_bytes=VMEM_LIMIT)


def _rope(t, cos, sin_signed):
    lane = lax.broadcasted_iota(jnp.int32, t.shape, 1)
    first_half = (lane % HEAD_DIM) < (HEAD_DIM // 2)
    partner = jnp.where(first_half, pltpu.roll(t, LANES - HEAD_DIM // 2, 1),
                        pltpu.roll(t, HEAD_DIM // 2, 1))
    return t * cos + partner * sin_signed


def _inproj_kernel(x_ref, g_ref, w_ref, cd_ref, sd_ref, cr_ref, sr_ref,
                   dq_ref, dk_ref, dv_ref, rq_ref, rk_ref, rv_ref, rg_ref):
    x = x_ref[...]
    ms = jnp.mean(x * x, axis=-1, keepdims=True)
    hn = (x * lax.rsqrt(ms + NORM_EPS) * g_ref[...]).astype(BF16)
    cd, sd, cr, sr = cd_ref[...], sd_ref[...], cr_ref[...], sr_ref[...]
    scale = HEAD_DIM ** -0.5

    def proj(col, width):
        return jnp.dot(hn, w_ref[:, col:col + width], preferred_element_type=F32)

    col = 0
    for out_ref, width, cos, sin, mul in (
            (dq_ref, DIFF_WIDTH, cd, sd, scale), (dk_ref, DIFF_WIDTH, cd, sd, None),
            (dv_ref, DIFF_WIDTH, None, None, None),
            (rq_ref, QK_WIDTH, cr, sr, None), (rk_ref, QK_WIDTH, cr, sr, scale),
            (rv_ref, RET_WIDTH, None, None, None), (rg_ref, RET_WIDTH, None, None, None)):
        for c in range(0, width, LANES):
            p = proj(col + c, LANES)
            if cos is not None:
                p = _rope(p, cos, sin)
            if mul is not None:
                p = p * mul
            out_ref[:, c:c + LANES] = p.astype(out_ref.dtype)
        col += width


def _in_projection(rows, g, w_bf16, tables, tile, seq_tiles):
    n = rows.shape[0]
    row_spec = lambda w: pl.BlockSpec((tile, w), lambda i: (i, 0))
    table_spec = pl.BlockSpec((tile, LANES), lambda i: (i % seq_tiles, 0))
    widths = (DIFF_WIDTH, DIFF_WIDTH, DIFF_WIDTH, QK_WIDTH, QK_WIDTH, RET_WIDTH, RET_WIDTH)
    return pl.pallas_call(
        _inproj_kernel,
        grid=(n // tile,),
        in_specs=[row_spec(D_MODEL),
                  pl.BlockSpec((1, D_MODEL), lambda i: (0, 0)),
                  pl.BlockSpec((D_MODEL, IN_PROJ_WIDTH), lambda i: (0, 0)),
                  table_spec, table_spec, table_spec, table_spec],
        out_specs=[row_spec(w) for w in widths],
        out_shape=[jax.ShapeDtypeStruct((n, w), BF16) for w in widths],
        compiler_params=_params(("parallel",)),
        name="in_projection",
    )(rows, g, w_bf16, *tables)


def _rope_tables(pos, inv_freq):
    ang = pos[:, None] * inv_freq[None, :]
    cos = jnp.tile(jnp.cos(ang), (1, LANES // (HEAD_DIM // 2)))
    sin = jnp.sin(ang)
    sin_signed = jnp.tile(jnp.concatenate([-sin, sin], axis=1), (1, LANES // HEAD_DIM))
    return cos, sin_signed


def _attn_kernel(q_ref, k_ref, v_ref, km_ref, vm_ref, lp_ref, sg_ref, o_ref,
                 m_sc, l_sc, acc_sc, *, tile):
    i = pl.program_id(2)
    q = q_ref[...]
    lane = lax.broadcasted_iota(jnp.int32, q.shape, 1)
    qs = (jnp.where(lane < HEAD_DIM, q, jnp.zeros_like(q)),
          jnp.where(lane >= HEAD_DIM, q, jnp.zeros_like(q)))

    def scores(qm, kc):
        return lax.dot_general(qm, kc, (((1,), (1,)), ((), ())), preferred_element_type=F32)

    km, vm = km_ref[...], vm_ref[...]
    meta_col = lax.broadcasted_iota(jnp.int32, (tile, META_ROWS), 1) >= META_ROWS - N_META
    for a in range(2):
        s = jnp.where(meta_col, scores(qs[a], km), -jnp.inf)
        m = jnp.max(s, axis=-1, keepdims=True)
        p = jnp.exp(s - m)
        m_sc[a] = m
        l_sc[a] = jnp.sum(p, axis=-1, keepdims=True)
        acc_sc[a] = jnp.dot(p.astype(BF16), vm, preferred_element_type=F32)

    def update(kc, vc, mask):
        for a in range(2):
            s = scores(qs[a], kc)
            if mask is not None:
                s = jnp.where(mask, s, -jnp.inf)
            m_old = m_sc[a]
            m_new = jnp.maximum(m_old, jnp.max(s, axis=-1, keepdims=True))
            alpha = jnp.exp(m_old - m_new)
            p = jnp.exp(s - m_new)
            l_sc[a] = alpha * l_sc[a] + jnp.sum(p, axis=-1, keepdims=True)
            acc_sc[a] = alpha * acc_sc[a] + jnp.dot(p.astype(BF16), vc, preferred_element_type=F32)
            m_sc[a] = m_new

    def body(j, carry):
        start = pl.multiple_of(j * tile, tile)
        update(k_ref[pl.ds(start, tile), :], v_ref[pl.ds(start, tile), :], None)
        return carry

    lax.fori_loop(0, i, body, 0)
    start = pl.multiple_of(i * tile, tile)
    row = lax.broadcasted_iota(jnp.int32, (tile, tile), 0)
    colk = lax.broadcasted_iota(jnp.int32, (tile, tile), 1)
    update(k_ref[pl.ds(start, tile), :], v_ref[pl.ds(start, tile), :], colk <= row)

    lp = lp_ref[...]
    lam = (jnp.exp(jnp.sum(lp[0:1] * lp[1:2], axis=-1, keepdims=True))
           - jnp.exp(jnp.sum(lp[2:3] * lp[3:4], axis=-1, keepdims=True)) + LAMBDA_INIT)
    o = acc_sc[0] / l_sc[0] - lam * (acc_sc[1] / l_sc[1])
    o = o * lax.rsqrt(jnp.mean(o * o, axis=-1, keepdims=True) + NORM_EPS)
    o_ref[...] = (o * sg_ref[...] * (1.0 - LAMBDA_INIT)).astype(o_ref.dtype)


def _diff_attention(dq, dk, dv, km, vm, lam_params, subln_g):
    b, seq, _ = dq.shape
    tile = ATTN_TILE
    head_rows = pl.BlockSpec((None, tile, LANES), lambda bi, h, i: (bi, i, h))
    head_seq = pl.BlockSpec((None, seq, LANES), lambda bi, h, i: (bi, 0, h))
    head_meta = pl.BlockSpec((META_ROWS, LANES), lambda bi, h, i: (0, h))
    return pl.pallas_call(
        functools.partial(_attn_kernel, tile=tile),
        grid=(b, DIFF_HEADS, seq // tile),
        in_specs=[head_rows, head_seq, head_seq, head_meta, head_meta,
                  pl.BlockSpec((4, HEAD_DIM), lambda bi, h, i: (0, 0)),
                  pl.BlockSpec((1, VALUE_DIM), lambda bi, h, i: (0, 0))],
        out_specs=head_rows,
        out_shape=jax.ShapeDtypeStruct((b, seq, DIFF_WIDTH), BF16),
        scratch_shapes=[pltpu.VMEM((2, tile, 1), F32), pltpu.VMEM((2, tile, 1), F32),
                        pltpu.VMEM((2, tile, VALUE_DIM), F32)],
        compiler_params=_params(("parallel", "parallel", "arbitrary")),
        name="diff_attention",
    )(dq, dk, dv, km, vm, lam_params, subln_g)


def _decay_col(log_gamma, n, offset_fn):
    pos = lax.broadcasted_iota(jnp.int32, (n, 1), 0).astype(F32)
    return jnp.exp(log_gamma * offset_fn(pos))


def _ret_kernel(q_ref, k_ref, v_ref, g_ref, km_ref, vm_ref, o_ref, state_sc, *, chunk):
    c = pl.program_id(1)

    @pl.when(c == 0)
    def _():
        km = km_ref[...].astype(F32)
        for h in range(RET_HEADS):
            pair = h // 2
            kt = km[:, pair * LANES:(pair + 1) * LANES].T.astype(BF16)
            kd = _decay_col(_LOG_GAMMA[h], META_ROWS, lambda p: (META_ROWS - 1) - p)
            vd = (vm_ref[:, h * VALUE_DIM:(h + 1) * VALUE_DIM].astype(F32) * kd).astype(BF16)
            state_sc[h] = jnp.dot(kt, vd, preferred_element_type=F32)

    row = lax.broadcasted_iota(jnp.int32, (chunk, chunk), 0)
    col = lax.broadcasted_iota(jnp.int32, (chunk, chunk), 1)
    rel = (row - col).astype(F32)
    lane = lax.broadcasted_iota(jnp.int32, (chunk, LANES), 1)
    for pair in range(RET_HEADS // 2):
        q_pair = q_ref[:, pair * LANES:(pair + 1) * LANES]
        k_pair = k_ref[:, pair * LANES:(pair + 1) * LANES]
        kt = k_pair.astype(F32).T.astype(BF16)
        for h in (2 * pair, 2 * pair + 1):
            lg = _LOG_GAMMA[h]
            own = (lane < HEAD_DIM) if h % 2 == 0 else (lane >= HEAD_DIM)
            qm = jnp.where(own, q_pair, jnp.zeros_like(q_pair))
            v = v_ref[:, h * VALUE_DIM:(h + 1) * VALUE_DIM]
            s = lax.dot_general(qm, k_pair, (((1,), (1,)), ((), ())), preferred_element_type=F32)
            decay = jnp.where(rel >= 0, jnp.exp(lg * jnp.maximum(rel, 0.0)), 0.0)
            y = jnp.dot((s * decay).astype(BF16), v, preferred_element_type=F32)
            qd = _decay_col(lg, chunk, lambda p: p + 1.0)
            state = state_sc[h]
            y = y + jnp.dot((qm.astype(F32) * qd).astype(BF16), state.astype(BF16),
                            preferred_element_type=F32)
            kd = _decay_col(lg, chunk, lambda p: (chunk - 1) - p)
            vd = (v.astype(F32) * kd).astype(BF16)
            state_sc[h] = math.exp(lg * chunk) * state + jnp.dot(kt, vd, preferred_element_type=F32)
            y = y * lax.rsqrt(jnp.mean(y * y, axis=-1, keepdims=True) + NORM_EPS)
            gate = g_ref[:, h * VALUE_DIM:(h + 1) * VALUE_DIM].astype(F32)
            o_ref[:, h * VALUE_DIM:(h + 1) * VALUE_DIM] = (y * gate * jax.nn.sigmoid(gate)).astype(o_ref.dtype)


def _retention(rq, rk, rv, rg, km, vm):
    b, seq, _ = rq.shape
    chunk = RET_CHUNK
    rows = lambda w: pl.BlockSpec((None, chunk, w), lambda bi, c: (bi, c, 0))
    return pl.pallas_call(
        functools.partial(_ret_kernel, chunk=chunk),
        grid=(b, seq // chunk),
        in_specs=[rows(QK_WIDTH), rows(QK_WIDTH), rows(RET_WIDTH), rows(RET_WIDTH),
                  pl.BlockSpec((META_ROWS, QK_WIDTH), lambda bi, c: (0, 0)),
                  pl.BlockSpec((META_ROWS, RET_WIDTH), lambda bi, c: (0, 0))],
        out_specs=rows(RET_WIDTH),
        out_shape=jax.ShapeDtypeStruct((b, seq, RET_WIDTH), BF16),
        scratch_shapes=[pltpu.VMEM((RET_HEADS, LANES, VALUE_DIM), F32)],
        compiler_params=_params(("parallel", "arbitrary")),
        name="retention",
    )(rq, rk, rv, rg, km, vm)


def _route_kernel(x_ref, d_ref, r_ref, wo_ref, g_ref, wr_ref, br_ref,
                  h_ref, xn_ref, idx_ref, gate_ref, rank_ref, cnt_ref, carry_sc, *, tile):
    step = pl.program_id(0)

    @pl.when(step == 0)
    def _():
        carry_sc[...] = jnp.zeros_like(carry_sc)

    h = (x_ref[...]
         + jnp.dot(d_ref[...], wo_ref[0:DIFF_WIDTH, :], preferred_element_type=F32)
         + jnp.dot(r_ref[...], wo_ref[DIFF_WIDTH:, :], preferred_element_type=F32))
    h_ref[...] = h
    xn = h * lax.rsqrt(jnp.mean(h * h, axis=-1, keepdims=True) + NORM_EPS) * g_ref[...]
    xn_ref[...] = xn
    logits = jnp.dot(xn, wr_ref[...], preferred_element_type=F32,
                     precision=lax.Precision.HIGHEST) + br_ref[...]

    lane = lax.broadcasted_iota(jnp.int32, logits.shape, 1)
    work = logits
    vals, hots = [], []
    for k in range(TOP_K):
        v = jnp.max(work, axis=-1, keepdims=True)
        idx = jnp.min(jnp.where(work == v, lane, N_EXPERTS), axis=-1, keepdims=True)
        hot = lane == idx
        work = jnp.where(hot, -jnp.inf, work)
        idx_ref[:, k:k + 1] = idx
        vals.append(v)
        hots.append(hot.astype(F32))
    exps = [jnp.exp(v - vals[0]) for v in vals]
    denom = exps[0] + exps[1] + exps[2] + exps[3]
    for k in range(TOP_K):
        gate_ref[:, k:k + 1] = exps[k] / denom

    chosen = hots[0] + hots[1] + hots[2] + hots[3]
    r = lax.broadcasted_iota(jnp.int32, (tile, tile), 0)
    c = lax.broadcasted_iota(jnp.int32, (tile, tile), 1)
    earlier = (c < r).astype(BF16)
    before = jnp.dot(earlier, chosen.astype(BF16), preferred_element_type=F32) + carry_sc[...]
    for k in range(TOP_K):
        rank_ref[:, k:k + 1] = jnp.sum(hots[k] * before, axis=-1, keepdims=True).astype(jnp.int32)
    carry_sc[...] = carry_sc[...] + jnp.sum(chosen, axis=0, keepdims=True)
    cnt_ref[...] = carry_sc[...].astype(jnp.int32)


def _out_proj_and_route(x, diff, ret, wo, g, wr, br):
    n = x.shape[0]
    tile = ROUTE_TILE
    rows = lambda w: pl.BlockSpec((tile, w), lambda i: (i, 0))
    full = lambda a, b: pl.BlockSpec((a, b), lambda i: (0, 0))
    return pl.pallas_call(
        functools.partial(_route_kernel, tile=tile),
        grid=(n // tile,),
        in_specs=[rows(D_MODEL), rows(DIFF_WIDTH), rows(RET_WIDTH), full(D_MODEL, D_MODEL),
                  full(1, D_MODEL), full(D_MODEL, N_EXPERTS), full(1, N_EXPERTS)],
        out_specs=[rows(D_MODEL), rows(D_MODEL), rows(TOP_K), rows(TOP_K), rows(TOP_K),
                   full(1, N_EXPERTS)],
        out_shape=[jax.ShapeDtypeStruct((n, D_MODEL), F32), jax.ShapeDtypeStruct((n, D_MODEL), F32),
                   jax.ShapeDtypeStruct((n, TOP_K), jnp.int32), jax.ShapeDtypeStruct((n, TOP_K), F32),
                   jax.ShapeDtypeStruct((n, TOP_K), jnp.int32),
                   jax.ShapeDtypeStruct((1, N_EXPERTS), jnp.int32)],
        scratch_shapes=[pltpu.VMEM((1, N_EXPERTS), F32)],
        compiler_params=_params(("arbitrary",)),
        name="out_proj_route",
    )(x, diff, ret, wo, g, wr, br)


def _dispatch_kernel(dest_hbm, xn_ref, xs_in, xs_out, idx_smem, idx_sem, row_sem, *, tile):
    del xs_in
    i = pl.program_id(0)
    n_idx = tile * TOP_K
    idx_copy = pltpu.make_async_copy(dest_hbm.at[pl.ds(i * n_idx, n_idx)], idx_smem, idx_sem)
    idx_copy.start()
    idx_copy.wait()

    def row_copy(t, dst_row):
        return pltpu.make_async_copy(xn_ref.at[pl.ds(t, 1), :], xs_out.at[pl.ds(dst_row, 1), :], row_sem)

    def body(t, carry):
        for k in range(TOP_K):
            row_copy(t, idx_smem[t * TOP_K + k]).start()
        return carry

    lax.fori_loop(0, tile, body, 0, unroll=8)
    for k in range(TOP_K):
        pltpu.make_async_copy(xn_ref, xs_out.at[pl.ds(0, tile), :], row_sem).wait()


def _dispatch(dest_flat, xn, n_rows):
    n = xn.shape[0]
    tile = ROW_TILE
    xs0 = jnp.zeros((n_rows, D_MODEL), F32)
    return pl.pallas_call(
        functools.partial(_dispatch_kernel, tile=tile),
        grid=(n // tile,),
        in_specs=[pl.BlockSpec(memory_space=pl.ANY),
                  pl.BlockSpec((tile, D_MODEL), lambda i: (i, 0)),
                  pl.BlockSpec(memory_space=pl.ANY)],
        out_specs=pl.BlockSpec(memory_space=pl.ANY),
        out_shape=jax.ShapeDtypeStruct((n_rows, D_MODEL), F32),
        scratch_shapes=[pltpu.SMEM((tile * TOP_K,), jnp.int32),
                        pltpu.SemaphoreType.DMA, pltpu.SemaphoreType.DMA],
        input_output_aliases={2: 0},
        compiler_params=_params(("arbitrary",)),
        name="moe_dispatch",
    )(dest_flat, xn, xs0)


def _expert_kernel(be_ref, nu_ref, xs_ref, wu_ref, bu_ref, wd_ref, bd_ref, ys_ref):
    del be_ref

    @pl.when(pl.program_id(0) < nu_ref[0])
    def _():
        x = xs_ref[...].astype(BF16)
        glu = jnp.dot(x, wu_ref[:, 0:D_FF], preferred_element_type=F32) + bu_ref[:, 0:D_FF]
        lin = jnp.dot(x, wu_ref[:, D_FF:], preferred_element_type=F32) + bu_ref[:, D_FF:]
        glu = jnp.minimum(glu, SWIGLU_LIMIT)
        lin = jnp.clip(lin, -SWIGLU_LIMIT, SWIGLU_LIMIT)
        act = glu * jax.nn.sigmoid(SWIGLU_ALPHA * glu) * (lin + 1.0)
        ys_ref[...] = jnp.dot(act.astype(BF16), wd_ref[...], preferred_element_type=F32) + bd_ref[...]

    @pl.when(pl.program_id(0) >= nu_ref[0])
    def _():
        ys_ref[...] = jnp.zeros_like(ys_ref)


def _expert_ffn(block_expert, n_used, xs, wu, bu, wd, bd):
    n_rows = xs.shape[0]
    tile = MOE_TILE
    rows = pl.BlockSpec((tile, D_MODEL), lambda j, be, nu: (jnp.minimum(j, nu[0] - 1), 0))
    per_expert = lambda a, b: pl.BlockSpec((None, a, b), lambda j, be, nu: (be[j], 0, 0))
    return pl.pallas_call(
        _expert_kernel,
        grid_spec=pltpu.PrefetchScalarGridSpec(
            num_scalar_prefetch=2,
            grid=(n_rows // tile,),
            in_specs=[rows, per_expert(D_MODEL, 2 * D_FF), per_expert(1, 2 * D_FF),
                      per_expert(D_FF, D_MODEL), per_expert(1, D_MODEL)],
            out_specs=pl.BlockSpec((tile, D_MODEL), lambda j, be, nu: (j, 0))),
        out_shape=jax.ShapeDtypeStruct((n_rows, D_MODEL), F32),
        compiler_params=_params(("arbitrary",)),
        name="expert_ffn",
    )(block_expert, n_used, xs, wu, bu, wd, bd)


def _combine_kernel(dest_hbm, gate_ref, h_ref, ys_hbm, g_ref, o_ref,
                    idx_smem, buf, idx_sem, row_sem, *, tile):
    i = pl.program_id(0)
    n_idx = tile * TOP_K
    idx_copy = pltpu.make_async_copy(dest_hbm.at[pl.ds(i * n_idx, n_idx)], idx_smem, idx_sem)
    idx_copy.start()
    idx_copy.wait()

    def body(t, carry):
        for k in range(TOP_K):
            pltpu.make_async_copy(ys_hbm.at[pl.ds(idx_smem[t * TOP_K + k], 1), :],
                                  buf.at[k, pl.ds(t, 1), :], row_sem).start()
        return carry

    lax.fori_loop(0, tile, body, 0, unroll=8)
    for k in range(TOP_K):
        pltpu.make_async_copy(ys_hbm.at[pl.ds(0, tile), :], buf.at[k], row_sem).wait()

    gates = gate_ref[...]
    h = h_ref[...]
    for k in range(TOP_K):
        h = h + gates[:, k:k + 1] * buf[k]
    o_ref[...] = h * lax.rsqrt(jnp.mean(h * h, axis=-1, keepdims=True) + NORM_EPS) * g_ref[...]


def _combine(dest_flat, gates, h, ys, g):
    n = h.shape[0]
    tile = ROW_TILE
    return pl.pallas_call(
        functools.partial(_combine_kernel, tile=tile),
        grid=(n // tile,),
        in_specs=[pl.BlockSpec(memory_space=pl.ANY),
                  pl.BlockSpec((tile, TOP_K), lambda i: (i, 0)),
                  pl.BlockSpec((tile, D_MODEL), lambda i: (i, 0)),
                  pl.BlockSpec(memory_space=pl.ANY),
                  pl.BlockSpec((1, D_MODEL), lambda i: (0, 0))],
        out_specs=pl.BlockSpec((tile, D_MODEL), lambda i: (i, 0)),
        out_shape=jax.ShapeDtypeStruct((n, D_MODEL), F32),
        scratch_shapes=[pltpu.SMEM((tile * TOP_K,), jnp.int32),
                        pltpu.VMEM((TOP_K, tile, D_MODEL), F32),
                        pltpu.SemaphoreType.DMA, pltpu.SemaphoreType.DMA],
        compiler_params=_params(("arbitrary",)),
        name="moe_combine",
    )(dest_flat, gates, h, ys, g)


def kernel(x, meta_tokens, attn_norm_g, w_in, diff_lambda, diff_subln_g, w_out, ffn_norm_g,
           w_router, b_router, w_up, b_up, w_down, b_down, final_norm_g):
    b, seq, d = x.shape
    assert d == D_MODEL and seq % PROJ_TILE == 0 and seq % RET_CHUNK == 0 and seq % ATTN_TILE == 0
    assert w_in.shape[0] == 1, "one layer"
    n_tok = b * seq
    assert n_tok % ROUTE_TILE == 0 and n_tok % ROW_TILE == 0

    w_in_b = w_in[0].astype(BF16)
    w_out_b = w_out[0].astype(BF16)
    w_up_b = jnp.concatenate([w_up[0][..., 0::2], w_up[0][..., 1::2]], axis=-1).astype(BF16)
    b_up_s = jnp.concatenate([b_up[0][..., 0::2], b_up[0][..., 1::2]], axis=-1)[:, None, :]
    w_down_b = w_down[0].astype(BF16)
    b_down_s = b_down[0][:, None, :]

    diff_inv_freq = ROPE_THETA ** (-jnp.arange(0, HEAD_DIM, 2, dtype=F32) / HEAD_DIM)
    ret_inv_freq = ROPE_THETA ** (-jnp.linspace(0.0, 1.0, HEAD_DIM // 2, dtype=F32))
    pos_x = jnp.arange(seq, dtype=F32) + N_META
    pos_m = jnp.arange(META_ROWS, dtype=F32) - (META_ROWS - N_META)
    tables_x = _rope_tables(pos_x, diff_inv_freq) + _rope_tables(pos_x, ret_inv_freq)
    tables_m = _rope_tables(pos_m, diff_inv_freq) + _rope_tables(pos_m, ret_inv_freq)

    g_attn = attn_norm_g[0][None, :]
    x2 = x.reshape(n_tok, D_MODEL)
    meta_rows = jnp.concatenate(
        [jnp.zeros((META_ROWS - N_META, D_MODEL), x.dtype), meta_tokens.astype(x.dtype)], axis=0)
    dq, dk, dv, rq, rk, rv, rg = _in_projection(x2, g_attn, w_in_b, tables_x, PROJ_TILE, seq // PROJ_TILE)
    _, dk_m, dv_m, _, rk_m, rv_m, _ = _in_projection(meta_rows, g_attn, w_in_b, tables_m, META_ROWS, 1)

    per_batch = lambda t: t.reshape(b, seq, t.shape[-1])
    diff_out = _diff_attention(per_batch(dq), per_batch(dk), per_batch(dv), dk_m, dv_m,
                               diff_lambda[0], diff_subln_g[0][None, :])
    ret_out = _retention(per_batch(rq), per_batch(rk), per_batch(rv), per_batch(rg), rk_m, rv_m)

    h, xn, top_idx, gates, rank, counts = _out_proj_and_route(
        x2, diff_out.reshape(n_tok, DIFF_WIDTH), ret_out.reshape(n_tok, RET_WIDTH),
        w_out_b, ffn_norm_g[0][None, :], w_router[0], b_router[0][None, :])

    n_assign = n_tok * TOP_K
    n_blocks = -(-(n_assign + N_EXPERTS * (MOE_TILE - 1)) // MOE_TILE)
    counts = counts[0]
    blocks_per = (counts + MOE_TILE - 1) // MOE_TILE
    block_end = jnp.cumsum(blocks_per)
    group_start = (block_end - blocks_per) * MOE_TILE
    dest = (group_start[top_idx] + rank).reshape(-1).astype(jnp.int32)
    n_used = block_end[-1:].astype(jnp.int32)
    block_ids = jnp.minimum(jnp.arange(n_blocks, dtype=jnp.int32), n_used[0] - 1)
    block_expert = jnp.minimum(jnp.searchsorted(block_end, block_ids, side='right'),
                               N_EXPERTS - 1).astype(jnp.int32)

    xs = _dispatch(dest, xn, n_blocks * MOE_TILE)
    ys = _expert_ffn(block_expert, n_used, xs, w_up_b, b_up_s, w_down_b, b_down_s)
    out = _combine(dest, gates, h, ys, final_norm_g[None, :])
    return out.reshape(b, seq, D_MODEL)
```

```python
import functools
import math

import jax
import jax.numpy as jnp
import numpy as np
from jax import lax
from jax.experimental import pallas as pl
from jax.experimental.pallas import tpu as pltpu

F32 = jnp.float32
BF16 = jnp.bfloat16

D_MODEL = 1024
N_META = 16
ROPE_THETA = 10000.0
NORM_EPS = 1e-5
DIFF_HEADS = 4
HEAD_DIM = 64
VALUE_DIM = 128
DIFF_WIDTH = DIFF_HEADS * VALUE_DIM
RET_HEADS = 4
RET_WIDTH = RET_HEADS * VALUE_DIM
QK_WIDTH = RET_HEADS * HEAD_DIM
IN_PROJ_WIDTH = 3 * DIFF_WIDTH + 2 * QK_WIDTH + 2 * RET_WIDTH
LAMBDA_INIT = 0.8 - 0.6 * math.exp(-0.3 * 0)
N_EXPERTS = 32
TOP_K = 4
D_FF = D_MODEL
SWIGLU_ALPHA = 1.702
SWIGLU_LIMIT = 7.0

LANES = 128
META_ROWS = 128
VMEM_LIMIT = 56 * 1024 * 1024

ONES_ROWS = 16
LOG2_E = math.log2(math.e)

PROJ_TILE = 512
ATTN_TILE = 512
RET_CHUNK = 256
ROUTE_TILE = 512
ROW_TILE = 256
MOE_TILE = 512

_LOG_GAMMA = [float(v) for v in np.log1p(-np.exp2(-5.0 - np.arange(RET_HEADS, dtype=np.float32)))]


def _params(sem):
    return pltpu.CompilerParams(dimension_semantics=sem, vmem_limit_bytes=VMEM_LIMIT)


def _rope(t, cos, sin_signed):
    lane = lax.broadcasted_iota(jnp.int32, t.shape, 1)
    first_half = (lane % HEAD_DIM) < (HEAD_DIM // 2)
    partner = jnp.where(first_half, pltpu.roll(t, LANES - HEAD_DIM // 2, 1),
                        pltpu.roll(t, HEAD_DIM // 2, 1))
    return t * cos + partner * sin_signed


def _inproj_kernel(x_ref, g_ref, w_ref, cd_ref, sd_ref, cr_ref, sr_ref,
                   dq_ref, dk_ref, dv_ref, rq_ref, rk_ref, rv_ref, rg_ref):
    x = x_ref[...]
    ms = jnp.mean(x * x, axis=-1, keepdims=True)
    hn = (x * lax.rsqrt(ms + NORM_EPS) * g_ref[...]).astype(BF16)
    cd, sd, cr, sr = cd_ref[...], sd_ref[...], cr_ref[...], sr_ref[...]
    scale = HEAD_DIM ** -0.5

    def proj(col, width):
        return jnp.dot(hn, w_ref[:, col:col + width], preferred_element_type=F32)

    col = 0
    for out_ref, width, cos, sin, mul in (
            (dq_ref, DIFF_WIDTH, cd, sd, scale * LOG2_E), (dk_ref, DIFF_WIDTH, cd, sd, None),
            (dv_ref, DIFF_WIDTH, None, None, None),
            (rq_ref, QK_WIDTH, cr, sr, None), (rk_ref, QK_WIDTH, cr, sr, scale),
            (rv_ref, RET_WIDTH, None, None, None), (rg_ref, RET_WIDTH, None, None, None)):
        for c in range(0, width, LANES):
            p = proj(col + c, LANES)
            if cos is not None:
                p = _rope(p, cos, sin)
            if mul is not None:
                p = p * mul
            out_ref[:, c:c + LANES] = p.astype(out_ref.dtype)
        col += width


def _in_projection(rows, g, w_bf16, tables, tile, seq_tiles):
    n = rows.shape[0]
    row_spec = lambda w: pl.BlockSpec((tile, w), lambda i: (i, 0))
    table_spec = pl.BlockSpec((tile, LANES), lambda i: (i % seq_tiles, 0))
    widths = (DIFF_WIDTH, DIFF_WIDTH, DIFF_WIDTH, QK_WIDTH, QK_WIDTH, RET_WIDTH, RET_WIDTH)
    return pl.pallas_call(
        _inproj_kernel,
        grid=(n // tile,),
        in_specs=[row_spec(D_MODEL),
                  pl.BlockSpec((1, D_MODEL), lambda i: (0, 0)),
                  pl.BlockSpec((D_MODEL, IN_PROJ_WIDTH), lambda i: (0, 0)),
                  table_spec, table_spec, table_spec, table_spec],
        out_specs=[row_spec(w) for w in widths],
        out_shape=[jax.ShapeDtypeStruct((n, w), BF16) for w in widths],
        compiler_params=_params(("parallel",)),
        name="in_projection",
    )(rows, g, w_bf16, *tables)


def _rope_tables(pos, inv_freq):
    ang = pos[:, None] * inv_freq[None, :]
    cos = jnp.tile(jnp.cos(ang), (1, LANES // (HEAD_DIM // 2)))
    sin = jnp.sin(ang)
    sin_signed = jnp.tile(jnp.concatenate([-sin, sin], axis=1), (1, LANES // HEAD_DIM))
    return cos, sin_signed


def _transpose_bf16(t):
    return t.astype(F32).T.astype(BF16)


def _with_ones_rows(vt):
    return jnp.concatenate([vt, jnp.ones((ONES_ROWS, vt.shape[1]), vt.dtype)], axis=0)


def _attn_kernel(q_ref, k_ref, v_ref, km_ref, vm_ref, lp_ref, sg_ref, o_ref,
                 vt_sc, m_sc, acc_sc, *, tile, n_chunks):
    i = pl.program_id(2)

    @pl.when(i == 0)
    def _():
        def transpose_chunk(j, carry):
            start = pl.multiple_of(j * tile, tile)
            vt_sc[j] = _with_ones_rows(_transpose_bf16(v_ref[pl.ds(start, tile), :]))
            return carry
        lax.fori_loop(0, n_chunks, transpose_chunk, 0)

    qt = q_ref[...].astype(F32).T
    sub = lax.broadcasted_iota(jnp.int32, qt.shape, 0)
    qcat = jnp.concatenate([jnp.where(sub < HEAD_DIM, qt, 0.0).astype(BF16),
                            jnp.where(sub >= HEAD_DIM, qt, 0.0).astype(BF16)], axis=1)

    meta_row = lax.broadcasted_iota(jnp.int32, (META_ROWS, 2 * tile), 0) >= META_ROWS - N_META
    s = jnp.where(meta_row, jnp.dot(km_ref[...], qcat, preferred_element_type=F32), -jnp.inf)
    m = jnp.max(s, axis=0, keepdims=True)
    m_sc[...] = m
    acc_sc[...] = jnp.dot(_with_ones_rows(_transpose_bf16(vm_ref[...])), jnp.exp2(s - m).astype(BF16),
                          preferred_element_type=F32)

    def update(kc, vtc, mask):
        s = jnp.dot(kc, qcat, preferred_element_type=F32)
        if mask is not None:
            s = jnp.where(mask, s, -jnp.inf)
        m_old = m_sc[...]
        m_new = jnp.maximum(m_old, jnp.max(s, axis=0, keepdims=True))
        p = jnp.exp2(s - m_new).astype(BF16)
        acc_sc[...] = jnp.exp2(m_old - m_new) * acc_sc[...] + jnp.dot(vtc, p, preferred_element_type=F32)
        m_sc[...] = m_new

    def body(j, carry):
        start = pl.multiple_of(j * tile, tile)
        update(k_ref[pl.ds(start, tile), :], vt_sc[j], None)
        return carry

    lax.fori_loop(0, i, body, 0)
    start = pl.multiple_of(i * tile, tile)
    key = lax.broadcasted_iota(jnp.int32, (tile, 2 * tile), 0)
    lane = lax.broadcasted_iota(jnp.int32, (tile, 2 * tile), 1)
    query = jnp.where(lane >= tile, lane - tile, lane)
    update(k_ref[pl.ds(start, tile), :], vt_sc[i], key <= query)

    lp = lp_ref[...]
    lam = (jnp.exp(jnp.sum(lp[0:1] * lp[1:2], axis=-1, keepdims=True))
           - jnp.exp(jnp.sum(lp[2:3] * lp[3:4], axis=-1, keepdims=True)) + LAMBDA_INIT)
    num, den = acc_sc[0:VALUE_DIM, :], acc_sc[VALUE_DIM:VALUE_DIM + 1, :]
    o = num[:, :tile] / den[:, :tile] - lam * (num[:, tile:] / den[:, tile:])
    o = o * lax.rsqrt(jnp.mean(o * o, axis=0, keepdims=True) + NORM_EPS)
    o_ref[...] = (o * sg_ref[...] * (1.0 - LAMBDA_INIT)).T.astype(o_ref.dtype)


def _diff_attention(dq, dk, dv, km, vm, lam_params, subln_g):
    b, seq, _ = dq.shape
    tile = ATTN_TILE
    n_chunks = seq // tile
    head_rows = pl.BlockSpec((None, tile, LANES), lambda bi, h, i: (bi, i, h))
    head_seq = pl.BlockSpec((None, seq, LANES), lambda bi, h, i: (bi, 0, h))
    head_meta = pl.BlockSpec((META_ROWS, LANES), lambda bi, h, i: (0, h))
    return pl.pallas_call(
        functools.partial(_attn_kernel, tile=tile, n_chunks=n_chunks),
        grid=(b, DIFF_HEADS, n_chunks),
        in_specs=[head_rows, head_seq, head_seq, head_meta, head_meta,
                  pl.BlockSpec((4, HEAD_DIM), lambda bi, h, i: (0, 0)),
                  pl.BlockSpec((VALUE_DIM, 1), lambda bi, h, i: (0, 0))],
        out_specs=head_rows,
        out_shape=jax.ShapeDtypeStruct((b, seq, DIFF_WIDTH), BF16),
        scratch_shapes=[pltpu.VMEM((n_chunks, VALUE_DIM + ONES_ROWS, tile), BF16),
                        pltpu.VMEM((1, 2 * tile), F32),
                        pltpu.VMEM((VALUE_DIM + ONES_ROWS, 2 * tile), F32)],
        compiler_params=_params(("parallel", "parallel", "arbitrary")),
        name="diff_attention",
    )(dq, dk, dv, km, vm, lam_params, subln_g)


def _decay_col(log_gamma, n, offset_fn):
    pos = lax.broadcasted_iota(jnp.int32, (n, 1), 0).astype(F32)
    return jnp.exp(log_gamma * offset_fn(pos))


def _ret_kernel(q_ref, k_ref, v_ref, g_ref, km_ref, vm_ref, o_ref, state_sc, *, chunk):
    c = pl.program_id(1)

    @pl.when(c == 0)
    def _():
        km = km_ref[...].astype(F32)
        for h in range(RET_HEADS):
            pair = h // 2
            kt = km[:, pair * LANES:(pair + 1) * LANES].T.astype(BF16)
            kd = _decay_col(_LOG_GAMMA[h], META_ROWS, lambda p: (META_ROWS - 1) - p)
            vd = (vm_ref[:, h * VALUE_DIM:(h + 1) * VALUE_DIM].astype(F32) * kd).astype(BF16)
            state_sc[h] = jnp.dot(kt, vd, preferred_element_type=F32)

    row = lax.broadcasted_iota(jnp.int32, (chunk, chunk), 0)
    col = lax.broadcasted_iota(jnp.int32, (chunk, chunk), 1)
    rel = (row - col).astype(F32)
    lane = lax.broadcasted_iota(jnp.int32, (chunk, LANES), 1)
    for pair in range(RET_HEADS // 2):
        q_pair = q_ref[:, pair * LANES:(pair + 1) * LANES]
        k_pair = k_ref[:, pair * LANES:(pair + 1) * LANES]
        kt = k_pair.astype(F32).T.astype(BF16)
        for h in (2 * pair, 2 * pair + 1):
            lg = _LOG_GAMMA[h]
            own = (lane < HEAD_DIM) if h % 2 == 0 else (lane >= HEAD_DIM)
            qm = jnp.where(own, q_pair, jnp.zeros_like(q_pair))
            v = v_ref[:, h * VALUE_DIM:(h + 1) * VALUE_DIM]
            s = lax.dot_general(qm, k_pair, (((1,), (1,)), ((), ())), preferred_element_type=F32)
            decay = jnp.where(rel >= 0, jnp.exp(lg * jnp.maximum(rel, 0.0)), 0.0)
            y = jnp.dot((s * decay).astype(BF16), v, preferred_element_type=F32)
            qd = _decay_col(lg, chunk, lambda p: p + 1.0)
            state = state_sc[h]
            y = y + jnp.dot((qm.astype(F32) * qd).astype(BF16), state.astype(BF16),
                            preferred_element_type=F32)
            kd = _decay_col(lg, chunk, lambda p: (chunk - 1) - p)
            vd = (v.astype(F32) * kd).astype(BF16)
            state_sc[h] = math.exp(lg * chunk) * state + jnp.dot(kt, vd, preferred_element_type=F32)
            y = y * lax.rsqrt(jnp.mean(y * y, axis=-1, keepdims=True) + NORM_EPS)
            gate = g_ref[:, h * VALUE_DIM:(h + 1) * VALUE_DIM].astype(F32)
            o_ref[:, h * VALUE_DIM:(h + 1) * VALUE_DIM] = (y * gate * jax.nn.sigmoid(gate)).astype(o_ref.dtype)


def _retention(rq, rk, rv, rg, km, vm):
    b, seq, _ = rq.shape
    chunk = RET_CHUNK
    rows = lambda w: pl.BlockSpec((None, chunk, w), lambda bi, c: (bi, c, 0))
    return pl.pallas_call(
        functools.partial(_ret_kernel, chunk=chunk),
        grid=(b, seq // chunk),
        in_specs=[rows(QK_WIDTH), rows(QK_WIDTH), rows(RET_WIDTH), rows(RET_WIDTH),
                  pl.BlockSpec((META_ROWS, QK_WIDTH), lambda bi, c: (0, 0)),
                  pl.BlockSpec((META_ROWS, RET_WIDTH), lambda bi, c: (0, 0))],
        out_specs=rows(RET_WIDTH),
        out_shape=jax.ShapeDtypeStruct((b, seq, RET_WIDTH), BF16),
        scratch_shapes=[pltpu.VMEM((RET_HEADS, LANES, VALUE_DIM), F32)],
        compiler_params=_params(("parallel", "arbitrary")),
        name="retention",
    )(rq, rk, rv, rg, km, vm)


def _route_kernel(x_ref, d_ref, r_ref, wo_ref, g_ref, wr_ref, br_ref,
                  h_ref, xn_ref, idx_ref, gate_ref, rank_ref, cnt_ref, carry_sc, *, tile):
    step = pl.program_id(0)

    @pl.when(step == 0)
    def _():
        carry_sc[...] = jnp.zeros_like(carry_sc)

    h = (x_ref[...]
         + jnp.dot(d_ref[...], wo_ref[0:DIFF_WIDTH, :], preferred_element_type=F32)
         + jnp.dot(r_ref[...], wo_ref[DIFF_WIDTH:, :], preferred_element_type=F32))
    h_ref[...] = h
    xn = h * lax.rsqrt(jnp.mean(h * h, axis=-1, keepdims=True) + NORM_EPS) * g_ref[...]
    xn_ref[...] = xn
    logits = jnp.dot(xn, wr_ref[...], preferred_element_type=F32,
                     precision=lax.Precision.HIGHEST) + br_ref[...]

    lane = lax.broadcasted_iota(jnp.int32, logits.shape, 1)
    work = logits
    vals, hots = [], []
    for k in range(TOP_K):
        v = jnp.max(work, axis=-1, keepdims=True)
        idx = jnp.min(jnp.where(work == v, lane, N_EXPERTS), axis=-1, keepdims=True)
        hot = lane == idx
        work = jnp.where(hot, -jnp.inf, work)
        idx_ref[:, k:k + 1] = idx
        vals.append(v)
        hots.append(hot.astype(F32))
    exps = [jnp.exp(v - vals[0]) for v in vals]
    denom = exps[0] + exps[1] + exps[2] + exps[3]
    for k in range(TOP_K):
        gate_ref[:, k:k + 1] = exps[k] / denom

    chosen = hots[0] + hots[1] + hots[2] + hots[3]
    r = lax.broadcasted_iota(jnp.int32, (tile, tile), 0)
    c = lax.broadcasted_iota(jnp.int32, (tile, tile), 1)
    earlier = (c < r).astype(BF16)
    before = jnp.dot(earlier, chosen.astype(BF16), preferred_element_type=F32) + carry_sc[...]
    for k in range(TOP_K):
        rank_ref[:, k:k + 1] = jnp.sum(hots[k] * before, axis=-1, keepdims=True).astype(jnp.int32)
    carry_sc[...] = carry_sc[...] + jnp.sum(chosen, axis=0, keepdims=True)
    cnt_ref[...] = carry_sc[...].astype(jnp.int32)


def _out_proj_and_route(x, diff, ret, wo, g, wr, br):
    n = x.shape[0]
    tile = ROUTE_TILE
    rows = lambda w: pl.BlockSpec((tile, w), lambda i: (i, 0))
    full = lambda a, b: pl.BlockSpec((a, b), lambda i: (0, 0))
    return pl.pallas_call(
        functools.partial(_route_kernel, tile=tile),
        grid=(n // tile,),
        in_specs=[rows(D_MODEL), rows(DIFF_WIDTH), rows(RET_WIDTH), full(D_MODEL, D_MODEL),
                  full(1, D_MODEL), full(D_MODEL, N_EXPERTS), full(1, N_EXPERTS)],
        out_specs=[rows(D_MODEL), rows(D_MODEL), rows(TOP_K), rows(TOP_K), rows(TOP_K),
                   full(1, N_EXPERTS)],
        out_shape=[jax.ShapeDtypeStruct((n, D_MODEL), F32), jax.ShapeDtypeStruct((n, D_MODEL), F32),
                   jax.ShapeDtypeStruct((n, TOP_K), jnp.int32), jax.ShapeDtypeStruct((n, TOP_K), F32),
                   jax.ShapeDtypeStruct((n, TOP_K), jnp.int32),
                   jax.ShapeDtypeStruct((1, N_EXPERTS), jnp.int32)],
        scratch_shapes=[pltpu.VMEM((1, N_EXPERTS), F32)],
        compiler_params=_params(("arbitrary",)),
        name="out_proj_route",
    )(x, diff, ret, wo, g, wr, br)


def _split_up_kernel(w_ref, o_ref):
    group = 2 * LANES
    src = lax.broadcasted_iota(jnp.int32, (group, group), 0)
    dst = lax.broadcasted_iota(jnp.int32, (group, group), 1)
    select = (src == jnp.where(dst < LANES, 2 * dst, 2 * (dst - LANES) + 1)).astype(BF16)
    for c in range(2 * D_FF // group):
        part = jnp.dot(w_ref[:, c * group:(c + 1) * group].astype(BF16), select,
                       preferred_element_type=F32).astype(BF16)
        o_ref[:, c * LANES:(c + 1) * LANES] = part[:, :LANES]
        o_ref[:, D_FF + c * LANES:D_FF + (c + 1) * LANES] = part[:, LANES:]


def _split_up_weights(w_up):
    n_exp, d, f2 = w_up.shape
    tile = 256
    spec = pl.BlockSpec((None, tile, f2), lambda e, i: (e, i, 0))
    return pl.pallas_call(
        _split_up_kernel,
        grid=(n_exp, d // tile),
        in_specs=[spec],
        out_specs=spec,
        out_shape=jax.ShapeDtypeStruct((n_exp, d, f2), BF16),
        compiler_params=_params(("parallel", "parallel")),
        name="split_up_weights",
    )(w_up)


def _dispatch_kernel(fill_ref, dest_hbm, xn_ref, xs_out, idx_smem, zero_buf, idx_sem, row_sem, fill_sem,
                     *, tile):
    i = pl.program_id(0)

    @pl.when(i == 0)
    def _():
        zero_buf[...] = jnp.zeros_like(zero_buf)

        def fill_copy(e):
            row = pl.multiple_of(fill_ref[e], MOE_TILE)
            return pltpu.make_async_copy(zero_buf, xs_out.at[pl.ds(row, MOE_TILE), :], fill_sem)

        def start(e, carry):
            @pl.when(fill_ref[e] >= 0)
            def _():
                fill_copy(e).start()
            return carry

        def wait(e, carry):
            @pl.when(fill_ref[e] >= 0)
            def _():
                fill_copy(e).wait()
            return carry

        lax.fori_loop(0, N_EXPERTS, start, 0)
        lax.fori_loop(0, N_EXPERTS, wait, 0)

    n_idx = tile * TOP_K
    idx_copy = pltpu.make_async_copy(dest_hbm.at[pl.ds(i * n_idx, n_idx)], idx_smem, idx_sem)
    idx_copy.start()
    idx_copy.wait()

    def row_copy(t, dst_row):
        return pltpu.make_async_copy(xn_ref.at[pl.ds(t, 1), :], xs_out.at[pl.ds(dst_row, 1), :], row_sem)

    def body(t, carry):
        for k in range(TOP_K):
            row_copy(t, idx_smem[t * TOP_K + k]).start()
        return carry

    lax.fori_loop(0, tile, body, 0, unroll=8)
    for k in range(TOP_K):
        pltpu.make_async_copy(xn_ref, xs_out.at[pl.ds(0, tile), :], row_sem).wait()


def _dispatch(fill_rows, dest_flat, xn, n_rows):
    n = xn.shape[0]
    tile = ROW_TILE
    return pl.pallas_call(
        functools.partial(_dispatch_kernel, tile=tile),
        grid_spec=pltpu.PrefetchScalarGridSpec(
            num_scalar_prefetch=1,
            grid=(n // tile,),
            in_specs=[pl.BlockSpec(memory_space=pl.ANY),
                      pl.BlockSpec((tile, D_MODEL), lambda i, fill: (i, 0))],
            out_specs=pl.BlockSpec(memory_space=pl.ANY),
            scratch_shapes=[pltpu.SMEM((tile * TOP_K,), jnp.int32),
                            pltpu.VMEM((MOE_TILE, D_MODEL), F32),
                            pltpu.SemaphoreType.DMA, pltpu.SemaphoreType.DMA,
                            pltpu.SemaphoreType.DMA]),
        out_shape=jax.ShapeDtypeStruct((n_rows, D_MODEL), F32),
        compiler_params=_params(("arbitrary",)),
        name="moe_dispatch",
    )(fill_rows, dest_flat, xn)


def _expert_kernel(be_ref, nu_ref, xs_ref, wu_ref, bu_ref, wd_ref, bd_ref, ys_ref):
    del be_ref

    @pl.when(pl.program_id(0) < nu_ref[0])
    def _():
        x = xs_ref[...].astype(BF16)
        glu = jnp.dot(x, wu_ref[:, 0:D_FF], preferred_element_type=F32) + bu_ref[:, 0:D_FF]
        lin = jnp.dot(x, wu_ref[:, D_FF:], preferred_element_type=F32) + bu_ref[:, D_FF:]
        glu = jnp.minimum(glu, SWIGLU_LIMIT)
        lin = jnp.clip(lin, -SWIGLU_LIMIT, SWIGLU_LIMIT)
        act = glu * jax.nn.sigmoid(SWIGLU_ALPHA * glu) * (lin + 1.0)
        ys_ref[...] = jnp.dot(act.astype(BF16), wd_ref[...], preferred_element_type=F32) + bd_ref[...]

    @pl.when(pl.program_id(0) >= nu_ref[0])
    def _():
        ys_ref[...] = jnp.zeros_like(ys_ref)


def _expert_ffn(block_expert, n_used, xs, wu, bu, wd, bd):
    n_rows = xs.shape[0]
    tile = MOE_TILE
    rows = pl.BlockSpec((tile, D_MODEL), lambda j, be, nu: (jnp.minimum(j, nu[0] - 1), 0))
    per_expert = lambda a, b: pl.BlockSpec((None, a, b), lambda j, be, nu: (be[j], 0, 0))
    return pl.pallas_call(
        _expert_kernel,
        grid_spec=pltpu.PrefetchScalarGridSpec(
            num_scalar_prefetch=2,
            grid=(n_rows // tile,),
            in_specs=[rows, per_expert(D_MODEL, 2 * D_FF), per_expert(1, 2 * D_FF),
                      per_expert(D_FF, D_MODEL), per_expert(1, D_MODEL)],
            out_specs=pl.BlockSpec((tile, D_MODEL), lambda j, be, nu: (j, 0))),
        out_shape=jax.ShapeDtypeStruct((n_rows, D_MODEL), F32),
        compiler_params=_params(("arbitrary",)),
        name="expert_ffn",
    )(block_expert, n_used, xs, wu, bu, wd, bd)


def _combine_kernel(dest_hbm, gate_ref, h_ref, ys_hbm, g_ref, o_ref,
                    idx_smem, buf, idx_sem, row_sem, *, tile):
    i = pl.program_id(0)
    n_idx = tile * TOP_K
    idx_copy = pltpu.make_async_copy(dest_hbm.at[pl.ds(i * n_idx, n_idx)], idx_smem, idx_sem)
    idx_copy.start()
    idx_copy.wait()

    def body(t, carry):
        for k in range(TOP_K):
            pltpu.make_async_copy(ys_hbm.at[pl.ds(idx_smem[t * TOP_K + k], 1), :],
                                  buf.at[k, pl.ds(t, 1), :], row_sem).start()
        return carry

    lax.fori_loop(0, tile, body, 0, unroll=8)
    for k in range(TOP_K):
        pltpu.make_async_copy(ys_hbm.at[pl.ds(0, tile), :], buf.at[k], row_sem).wait()

    gates = gate_ref[...]
    h = h_ref[...]
    for k in range(TOP_K):
        h = h + gates[:, k:k + 1] * buf[k]
    o_ref[...] = h * lax.rsqrt(jnp.mean(h * h, axis=-1, keepdims=True) + NORM_EPS) * g_ref[...]


def _combine(dest_flat, gates, h, ys, g):
    n = h.shape[0]
    tile = ROW_TILE
    return pl.pallas_call(
        functools.partial(_combine_kernel, tile=tile),
        grid=(n // tile,),
        in_specs=[pl.BlockSpec(memory_space=pl.ANY),
                  pl.BlockSpec((tile, TOP_K), lambda i: (i, 0)),
                  pl.BlockSpec((tile, D_MODEL), lambda i: (i, 0)),
                  pl.BlockSpec(memory_space=pl.ANY),
                  pl.BlockSpec((1, D_MODEL), lambda i: (0, 0))],
        out_specs=pl.BlockSpec((tile, D_MODEL), lambda i: (i, 0)),
        out_shape=jax.ShapeDtypeStruct((n, D_MODEL), F32),
        scratch_shapes=[pltpu.SMEM((tile * TOP_K,), jnp.int32),
                        pltpu.VMEM((TOP_K, tile, D_MODEL), F32),
                        pltpu.SemaphoreType.DMA, pltpu.SemaphoreType.DMA],
        compiler_params=_params(("arbitrary",)),
        name="moe_combine",
    )(dest_flat, gates, h, ys, g)


def kernel(x, meta_tokens, attn_norm_g, w_in, diff_lambda, diff_subln_g, w_out, ffn_norm_g,
           w_router, b_router, w_up, b_up, w_down, b_down, final_norm_g):
    b, seq, d = x.shape
    assert d == D_MODEL and seq % PROJ_TILE == 0 and seq % RET_CHUNK == 0 and seq % ATTN_TILE == 0
    assert w_in.shape[0] == 1, "one layer"
    n_tok = b * seq
    assert n_tok % ROUTE_TILE == 0 and n_tok % ROW_TILE == 0

    w_in_b = w_in[0].astype(BF16)
    w_out_b = w_out[0].astype(BF16)
    w_up_b = _split_up_weights(w_up[0])
    b_up_s = jnp.concatenate([b_up[0][..., 0::2], b_up[0][..., 1::2]], axis=-1)[:, None, :]
    w_down_b = w_down[0].astype(BF16)
    b_down_s = b_down[0][:, None, :]

    diff_inv_freq = ROPE_THETA ** (-jnp.arange(0, HEAD_DIM, 2, dtype=F32) / HEAD_DIM)
    ret_inv_freq = ROPE_THETA ** (-jnp.linspace(0.0, 1.0, HEAD_DIM // 2, dtype=F32))
    pos_x = jnp.arange(seq, dtype=F32) + N_META
    pos_m = jnp.arange(META_ROWS, dtype=F32) - (META_ROWS - N_META)
    tables_x = _rope_tables(pos_x, diff_inv_freq) + _rope_tables(pos_x, ret_inv_freq)
    tables_m = _rope_tables(pos_m, diff_inv_freq) + _rope_tables(pos_m, ret_inv_freq)

    g_attn = attn_norm_g[0][None, :]
    x2 = x.reshape(n_tok, D_MODEL)
    meta_rows = jnp.concatenate(
        [jnp.zeros((META_ROWS - N_META, D_MODEL), x.dtype), meta_tokens.astype(x.dtype)], axis=0)
    dq, dk, dv, rq, rk, rv, rg = _in_projection(x2, g_attn, w_in_b, tables_x, PROJ_TILE, seq // PROJ_TILE)
    _, dk_m, dv_m, _, rk_m, rv_m, _ = _in_projection(meta_rows, g_attn, w_in_b, tables_m, META_ROWS, 1)

    per_batch = lambda t: t.reshape(b, seq, t.shape[-1])
    diff_out = _diff_attention(per_batch(dq), per_batch(dk), per_batch(dv), dk_m, dv_m,
                               diff_lambda[0], diff_subln_g[0][:, None])
    ret_out = _retention(per_batch(rq), per_batch(rk), per_batch(rv), per_batch(rg), rk_m, rv_m)

    h, xn, top_idx, gates, rank, counts = _out_proj_and_route(
        x2, diff_out.reshape(n_tok, DIFF_WIDTH), ret_out.reshape(n_tok, RET_WIDTH),
        w_out_b, ffn_norm_g[0][None, :], w_router[0], b_router[0][None, :])

    n_assign = n_tok * TOP_K
    n_blocks = -(-(n_assign + N_EXPERTS * (MOE_TILE - 1)) // MOE_TILE)
    counts = counts[0]
    blocks_per = (counts + MOE_TILE - 1) // MOE_TILE
    block_end = jnp.cumsum(blocks_per)
    group_start = (block_end - blocks_per) * MOE_TILE
    dest = (group_start[top_idx] + rank).reshape(-1).astype(jnp.int32)
    n_used = block_end[-1:].astype(jnp.int32)
    block_ids = jnp.minimum(jnp.arange(n_blocks, dtype=jnp.int32), n_used[0] - 1)
    block_expert = jnp.minimum(jnp.sum(block_end[None, :] <= block_ids[:, None], axis=1),
                               N_EXPERTS - 1).astype(jnp.int32)

    fill_rows = jnp.where(blocks_per > 0, (block_end - 1) * MOE_TILE, -1).astype(jnp.int32)
    xs = _dispatch(fill_rows, dest, xn, n_blocks * MOE_TILE)
    ys = _expert_ffn(block_expert, n_used, xs, w_up_b, b_up_s, w_down_b, b_down_s)
    out = _combine(dest, gates, h, ys, final_norm_g[None, :])
    return out.reshape(b, seq, D_MODEL)
```

```python
import functools
import math

import jax
import jax.numpy as jnp
import numpy as np
from jax import lax
from jax.experimental import pallas as pl
from jax.experimental.pallas import tpu as pltpu

F32 = jnp.float32
BF16 = jnp.bfloat16

D_MODEL = 1024
N_META = 16
ROPE_THETA = 10000.0
NORM_EPS = 1e-5
DIFF_HEADS = 4
HEAD_DIM = 64
VALUE_DIM = 128
DIFF_WIDTH = DIFF_HEADS * VALUE_DIM
RET_HEADS = 4
RET_WIDTH = RET_HEADS * VALUE_DIM
QK_WIDTH = RET_HEADS * HEAD_DIM
IN_PROJ_WIDTH = 3 * DIFF_WIDTH + 2 * QK_WIDTH + 2 * RET_WIDTH
LAMBDA_INIT = 0.8 - 0.6 * math.exp(-0.3 * 0)
N_EXPERTS = 32
TOP_K = 4
D_FF = D_MODEL
SWIGLU_ALPHA = 1.702
SWIGLU_LIMIT = 7.0

LANES = 128
META_ROWS = 128
VMEM_LIMIT = 56 * 1024 * 1024

ONES_ROWS = 16
LOG2_E = math.log2(math.e)

PROJ_TILE = 512
ATTN_TILE = 512
RET_CHUNK = 256
ROUTE_TILE = 512
MOE_TILE = 512
SEG_ALIGN = 8
LOC_ROWS = ROUTE_TILE * TOP_K + N_EXPERTS * SEG_ALIGN

_LOG_GAMMA = [float(v) for v in np.log1p(-np.exp2(-5.0 - np.arange(RET_HEADS, dtype=np.float32)))]


def _params(sem):
    return pltpu.CompilerParams(dimension_semantics=sem, vmem_limit_bytes=VMEM_LIMIT)


def _rope(t, cos, sin_signed):
    lane = lax.broadcasted_iota(jnp.int32, t.shape, 1)
    first_half = (lane % HEAD_DIM) < (HEAD_DIM // 2)
    partner = jnp.where(first_half, pltpu.roll(t, LANES - HEAD_DIM // 2, 1),
                        pltpu.roll(t, HEAD_DIM // 2, 1))
    return t * cos + partner * sin_signed


def _inproj_kernel(x_ref, g_ref, w_ref, cd_ref, sd_ref, cr_ref, sr_ref,
                   dq_ref, dk_ref, dv_ref, rq_ref, rk_ref, rv_ref, rg_ref):
    x = x_ref[...]
    ms = jnp.mean(x * x, axis=-1, keepdims=True)
    hn = (x * lax.rsqrt(ms + NORM_EPS) * g_ref[...]).astype(BF16)
    cd, sd, cr, sr = cd_ref[...], sd_ref[...], cr_ref[...], sr_ref[...]
    scale = HEAD_DIM ** -0.5

    def proj(col, width):
        return jnp.dot(hn, w_ref[:, col:col + width], preferred_element_type=F32)

    col = 0
    for out_ref, width, cos, sin, mul in (
            (dq_ref, DIFF_WIDTH, cd, sd, scale * LOG2_E), (dk_ref, DIFF_WIDTH, cd, sd, None),
            (dv_ref, DIFF_WIDTH, None, None, None),
            (rq_ref, QK_WIDTH, cr, sr, None), (rk_ref, QK_WIDTH, cr, sr, scale),
            (rv_ref, RET_WIDTH, None, None, None), (rg_ref, RET_WIDTH, None, None, None)):
        for c in range(0, width, LANES):
            p = proj(col + c, LANES)
            if cos is not None:
                p = _rope(p, cos, sin)
            if mul is not None:
                p = p * mul
            out_ref[:, c:c + LANES] = p.astype(out_ref.dtype)
        col += width


def _in_projection(rows, g, w_bf16, tables, tile, seq_tiles):
    n = rows.shape[0]
    row_spec = lambda w: pl.BlockSpec((tile, w), lambda i: (i, 0))
    table_spec = pl.BlockSpec((tile, LANES), lambda i: (i % seq_tiles, 0))
    widths = (DIFF_WIDTH, DIFF_WIDTH, DIFF_WIDTH, QK_WIDTH, QK_WIDTH, RET_WIDTH, RET_WIDTH)
    return pl.pallas_call(
        _inproj_kernel,
        grid=(n // tile,),
        in_specs=[row_spec(D_MODEL),
                  pl.BlockSpec((1, D_MODEL), lambda i: (0, 0)),
                  pl.BlockSpec((D_MODEL, IN_PROJ_WIDTH), lambda i: (0, 0)),
                  table_spec, table_spec, table_spec, table_spec],
        out_specs=[row_spec(w) for w in widths],
        out_shape=[jax.ShapeDtypeStruct((n, w), BF16) for w in widths],
        compiler_params=_params(("parallel",)),
        name="in_projection",
    )(rows, g, w_bf16, *tables)


def _rope_tables(pos, inv_freq):
    ang = pos[:, None] * inv_freq[None, :]
    cos = jnp.tile(jnp.cos(ang), (1, LANES // (HEAD_DIM // 2)))
    sin = jnp.sin(ang)
    sin_signed = jnp.tile(jnp.concatenate([-sin, sin], axis=1), (1, LANES // HEAD_DIM))
    return cos, sin_signed


def _transpose_bf16(t):
    return t.astype(F32).T.astype(BF16)


def _with_ones_rows(vt):
    return jnp.concatenate([vt, jnp.ones((ONES_ROWS, vt.shape[1]), vt.dtype)], axis=0)


def _attn_kernel(q_ref, k_ref, v_ref, km_ref, vm_ref, lp_ref, sg_ref, o_ref,
                 vt_sc, m_sc, acc_sc, *, tile, n_chunks):
    i = pl.program_id(2)

    @pl.when(i == 0)
    def _():
        def transpose_chunk(j, carry):
            start = pl.multiple_of(j * tile, tile)
            vt_sc[j] = _with_ones_rows(_transpose_bf16(v_ref[pl.ds(start, tile), :]))
            return carry
        lax.fori_loop(0, n_chunks, transpose_chunk, 0)

    qt = q_ref[...].astype(F32).T
    sub = lax.broadcasted_iota(jnp.int32, qt.shape, 0)
    qcat = jnp.concatenate([jnp.where(sub < HEAD_DIM, qt, 0.0).astype(BF16),
                            jnp.where(sub >= HEAD_DIM, qt, 0.0).astype(BF16)], axis=1)

    meta_row = lax.broadcasted_iota(jnp.int32, (META_ROWS, 2 * tile), 0) >= META_ROWS - N_META
    s = jnp.where(meta_row, jnp.dot(km_ref[...], qcat, preferred_element_type=F32), -jnp.inf)
    m = jnp.max(s, axis=0, keepdims=True)
    m_sc[...] = m
    acc_sc[...] = jnp.dot(_with_ones_rows(_transpose_bf16(vm_ref[...])), jnp.exp2(s - m).astype(BF16),
                          preferred_element_type=F32)

    def update(kc, vtc, mask):
        s = jnp.dot(kc, qcat, preferred_element_type=F32)
        if mask is not None:
            s = jnp.where(mask, s, -jnp.inf)
        m_old = m_sc[...]
        m_new = jnp.maximum(m_old, jnp.max(s, axis=0, keepdims=True))
        p = jnp.exp2(s - m_new).astype(BF16)
        acc_sc[...] = jnp.exp2(m_old - m_new) * acc_sc[...] + jnp.dot(vtc, p, preferred_element_type=F32)
        m_sc[...] = m_new

    def body(j, carry):
        start = pl.multiple_of(j * tile, tile)
        update(k_ref[pl.ds(start, tile), :], vt_sc[j], None)
        return carry

    lax.fori_loop(0, i, body, 0)
    start = pl.multiple_of(i * tile, tile)
    key = lax.broadcasted_iota(jnp.int32, (tile, 2 * tile), 0)
    lane = lax.broadcasted_iota(jnp.int32, (tile, 2 * tile), 1)
    query = jnp.where(lane >= tile, lane - tile, lane)
    update(k_ref[pl.ds(start, tile), :], vt_sc[i], key <= query)

    lp = lp_ref[...]
    lam = (jnp.exp(jnp.sum(lp[0:1] * lp[1:2], axis=-1, keepdims=True))
           - jnp.exp(jnp.sum(lp[2:3] * lp[3:4], axis=-1, keepdims=True)) + LAMBDA_INIT)
    num, den = acc_sc[0:VALUE_DIM, :], acc_sc[VALUE_DIM:VALUE_DIM + 1, :]
    o = num[:, :tile] / den[:, :tile] - lam * (num[:, tile:] / den[:, tile:])
    o = o * lax.rsqrt(jnp.mean(o * o, axis=0, keepdims=True) + NORM_EPS)
    o_ref[...] = (o * sg_ref[...] * (1.0 - LAMBDA_INIT)).T.astype(o_ref.dtype)


def _diff_attention(dq, dk, dv, km, vm, lam_params, subln_g):
    b, seq, _ = dq.shape
    tile = ATTN_TILE
    n_chunks = seq // tile
    head_rows = pl.BlockSpec((None, tile, LANES), lambda bi, h, i: (bi, i, h))
    head_seq = pl.BlockSpec((None, seq, LANES), lambda bi, h, i: (bi, 0, h))
    head_meta = pl.BlockSpec((META_ROWS, LANES), lambda bi, h, i: (0, h))
    return pl.pallas_call(
        functools.partial(_attn_kernel, tile=tile, n_chunks=n_chunks),
        grid=(b, DIFF_HEADS, n_chunks),
        in_specs=[head_rows, head_seq, head_seq, head_meta, head_meta,
                  pl.BlockSpec((4, HEAD_DIM), lambda bi, h, i: (0, 0)),
                  pl.BlockSpec((VALUE_DIM, 1), lambda bi, h, i: (0, 0))],
        out_specs=head_rows,
        out_shape=jax.ShapeDtypeStruct((b, seq, DIFF_WIDTH), BF16),
        scratch_shapes=[pltpu.VMEM((n_chunks, VALUE_DIM + ONES_ROWS, tile), BF16),
                        pltpu.VMEM((1, 2 * tile), F32),
                        pltpu.VMEM((VALUE_DIM + ONES_ROWS, 2 * tile), F32)],
        compiler_params=_params(("parallel", "parallel", "arbitrary")),
        name="diff_attention",
    )(dq, dk, dv, km, vm, lam_params, subln_g)


def _decay_col(log_gamma, n, offset_fn):
    pos = lax.broadcasted_iota(jnp.int32, (n, 1), 0).astype(F32)
    return jnp.exp(log_gamma * offset_fn(pos))


def _ret_kernel(q_ref, k_ref, v_ref, g_ref, km_ref, vm_ref, o_ref, state_sc, *, chunk):
    c = pl.program_id(1)

    @pl.when(c == 0)
    def _():
        km = km_ref[...].astype(F32)
        for h in range(RET_HEADS):
            pair = h // 2
            kt = km[:, pair * LANES:(pair + 1) * LANES].T.astype(BF16)
            kd = _decay_col(_LOG_GAMMA[h], META_ROWS, lambda p: (META_ROWS - 1) - p)
            vd = (vm_ref[:, h * VALUE_DIM:(h + 1) * VALUE_DIM].astype(F32) * kd).astype(BF16)
            state_sc[h] = jnp.dot(kt, vd, preferred_element_type=F32)

    row = lax.broadcasted_iota(jnp.int32, (chunk, chunk), 0)
    col = lax.broadcasted_iota(jnp.int32, (chunk, chunk), 1)
    rel = (row - col).astype(F32)
    lane = lax.broadcasted_iota(jnp.int32, (chunk, LANES), 1)
    for pair in range(RET_HEADS // 2):
        q_pair = q_ref[:, pair * LANES:(pair + 1) * LANES]
        k_pair = k_ref[:, pair * LANES:(pair + 1) * LANES]
        kt = k_pair.astype(F32).T.astype(BF16)
        for h in (2 * pair, 2 * pair + 1):
            lg = _LOG_GAMMA[h]
            own = (lane < HEAD_DIM) if h % 2 == 0 else (lane >= HEAD_DIM)
            qm = jnp.where(own, q_pair, jnp.zeros_like(q_pair))
            v = v_ref[:, h * VALUE_DIM:(h + 1) * VALUE_DIM]
            s = lax.dot_general(qm, k_pair, (((1,), (1,)), ((), ())), preferred_element_type=F32)
            decay = jnp.where(rel >= 0, jnp.exp(lg * jnp.maximum(rel, 0.0)), 0.0)
            y = jnp.dot((s * decay).astype(BF16), v, preferred_element_type=F32)
            qd = _decay_col(lg, chunk, lambda p: p + 1.0)
            state = state_sc[h]
            y = y + jnp.dot((qm.astype(F32) * qd).astype(BF16), state.astype(BF16),
                            preferred_element_type=F32)
            kd = _decay_col(lg, chunk, lambda p: (chunk - 1) - p)
            vd = (v.astype(F32) * kd).astype(BF16)
            state_sc[h] = math.exp(lg * chunk) * state + jnp.dot(kt, vd, preferred_element_type=F32)
            y = y * lax.rsqrt(jnp.mean(y * y, axis=-1, keepdims=True) + NORM_EPS)
            gate = g_ref[:, h * VALUE_DIM:(h + 1) * VALUE_DIM].astype(F32)
            o_ref[:, h * VALUE_DIM:(h + 1) * VALUE_DIM] = (y * gate * jax.nn.sigmoid(gate)).astype(o_ref.dtype)


def _retention(rq, rk, rv, rg, km, vm):
    b, seq, _ = rq.shape
    chunk = RET_CHUNK
    rows = lambda w: pl.BlockSpec((None, chunk, w), lambda bi, c: (bi, c, 0))
    return pl.pallas_call(
        functools.partial(_ret_kernel, chunk=chunk),
        grid=(b, seq // chunk),
        in_specs=[rows(QK_WIDTH), rows(QK_WIDTH), rows(RET_WIDTH), rows(RET_WIDTH),
                  pl.BlockSpec((META_ROWS, QK_WIDTH), lambda bi, c: (0, 0)),
                  pl.BlockSpec((META_ROWS, RET_WIDTH), lambda bi, c: (0, 0))],
        out_specs=rows(RET_WIDTH),
        out_shape=jax.ShapeDtypeStruct((b, seq, RET_WIDTH), BF16),
        scratch_shapes=[pltpu.VMEM((RET_HEADS, LANES, VALUE_DIM), F32)],
        compiler_params=_params(("parallel", "arbitrary")),
        name="retention",
    )(rq, rk, rv, rg, km, vm)


def _route_kernel(x_ref, d_ref, r_ref, wo_ref, g_ref, wrt_ref, brt_ref,
                  h_ref, xn_ref, pos_ref, col_ref, cnt_ref, *, tile):
    h = (x_ref[...]
         + jnp.dot(d_ref[...], wo_ref[0:DIFF_WIDTH, :], preferred_element_type=F32)
         + jnp.dot(r_ref[...], wo_ref[DIFF_WIDTH:, :], preferred_element_type=F32))
    h_ref[...] = h
    xn = h * lax.rsqrt(jnp.mean(h * h, axis=-1, keepdims=True) + NORM_EPS) * g_ref[...]
    xn_ref[...] = xn.astype(BF16)
    logits = lax.dot_general(wrt_ref[...], xn, (((1,), (1,)), ((), ())), preferred_element_type=F32,
                             precision=lax.Precision.HIGHEST) + brt_ref[...]

    expert = lax.broadcasted_iota(jnp.int32, logits.shape, 0)
    work = logits
    vals, hots = [], []
    for k in range(TOP_K):
        v = jnp.max(work, axis=0, keepdims=True)
        idx = jnp.min(jnp.where(work == v, expert, N_EXPERTS), axis=0, keepdims=True)
        hot = expert == idx
        work = jnp.where(hot, -jnp.inf, work)
        vals.append(v)
        hots.append(hot.astype(F32))
    exps = [jnp.exp(v - vals[0]) for v in vals]
    denom = exps[0] + exps[1] + exps[2] + exps[3]
    gates = [e / denom for e in exps]

    chosen = hots[0] + hots[1] + hots[2] + hots[3]
    c = lax.broadcasted_iota(jnp.int32, (tile, tile), 0)
    r = lax.broadcasted_iota(jnp.int32, (tile, tile), 1)
    earlier = (c < r).astype(BF16)
    before = jnp.dot(chosen.astype(BF16), earlier, preferred_element_type=F32)
    cnt = jnp.sum(chosen, axis=1, keepdims=True)
    cnt_pad = jnp.ceil(cnt / SEG_ALIGN) * SEG_ALIGN
    er = lax.broadcasted_iota(jnp.int32, (N_EXPERTS, N_EXPERTS), 0)
    ec = lax.broadcasted_iota(jnp.int32, (N_EXPERTS, N_EXPERTS), 1)
    seg_start = jnp.dot((ec < er).astype(F32), jnp.broadcast_to(cnt_pad, (N_EXPERTS, LANES)),
                        preferred_element_type=F32, precision=lax.Precision.HIGHEST)[:, 0:1]
    base = seg_start + before
    pos = [jnp.sum(hot * base, axis=0, keepdims=True) for hot in hots]
    for k in range(TOP_K):
        pos_ref[k:k + 1, :] = pos[k].astype(jnp.int32)
    cnt_ref[...] = jnp.broadcast_to(cnt, (N_EXPERTS, LANES)).astype(jnp.int32)
    stacked = jnp.concatenate(pos + gates + [jnp.zeros((LANES - 2 * TOP_K, tile), F32)], axis=0)
    col_ref[...] = stacked.T


def _out_proj_and_route(x, diff, ret, wo, g, wrt, brt):
    n = x.shape[0]
    tile = ROUTE_TILE
    rows = lambda w: pl.BlockSpec((tile, w), lambda i: (i, 0))
    full = lambda a, b: pl.BlockSpec((a, b), lambda i: (0, 0))
    return pl.pallas_call(
        functools.partial(_route_kernel, tile=tile),
        grid=(n // tile,),
        in_specs=[rows(D_MODEL), rows(DIFF_WIDTH), rows(RET_WIDTH), full(D_MODEL, D_MODEL),
                  full(1, D_MODEL), full(N_EXPERTS, D_MODEL), full(N_EXPERTS, 1)],
        out_specs=[rows(D_MODEL), rows(D_MODEL),
                   pl.BlockSpec((TOP_K, tile), lambda i: (0, i)), rows(LANES),
                   pl.BlockSpec((None, N_EXPERTS, LANES), lambda i: (i, 0, 0))],
        out_shape=[jax.ShapeDtypeStruct((n, D_MODEL), F32), jax.ShapeDtypeStruct((n, D_MODEL), BF16),
                   jax.ShapeDtypeStruct((TOP_K, n), jnp.int32), jax.ShapeDtypeStruct((n, LANES), F32),
                   jax.ShapeDtypeStruct((n // tile, N_EXPERTS, LANES), jnp.int32)],
        compiler_params=_params(("parallel",)),
        name="out_proj_route",
    )(x, diff, ret, wo, g, wrt, brt)


def _split_up_kernel(w_ref, o_ref):
    group = 2 * LANES
    src = lax.broadcasted_iota(jnp.int32, (group, group), 0)
    dst = lax.broadcasted_iota(jnp.int32, (group, group), 1)
    select = (src == jnp.where(dst < LANES, 2 * dst, 2 * (dst - LANES) + 1)).astype(BF16)
    for c in range(2 * D_FF // group):
        part = jnp.dot(w_ref[:, c * group:(c + 1) * group].astype(BF16), select,
                       preferred_element_type=F32).astype(BF16)
        o_ref[:, c * LANES:(c + 1) * LANES] = part[:, :LANES]
        o_ref[:, D_FF + c * LANES:D_FF + (c + 1) * LANES] = part[:, LANES:]


def _split_up_weights(w_up):
    n_exp, d, f2 = w_up.shape
    tile = 256
    spec = pl.BlockSpec((None, tile, f2), lambda e, i: (e, i, 0))
    return pl.pallas_call(
        _split_up_kernel,
        grid=(n_exp, d // tile),
        in_specs=[spec],
        out_specs=spec,
        out_shape=jax.ShapeDtypeStruct((n_exp, d, f2), BF16),
        compiler_params=_params(("parallel", "parallel")),
        name="split_up_weights",
    )(w_up)


def _segment_copies(tile_index, local_ref, global_ref, n_ref, local_buf, global_hbm, sem, to_global, action):
    first = tile_index * N_EXPERTS

    def per_expert(e, carry):
        local_row, global_row = local_ref[first + e], global_ref[first + e]

        def per_piece(p, inner):
            lo = pl.multiple_of(local_row + p * SEG_ALIGN, SEG_ALIGN)
            gl = pl.multiple_of(global_row + p * SEG_ALIGN, SEG_ALIGN)
            local_piece = local_buf.at[pl.ds(lo, SEG_ALIGN), :]
            global_piece = global_hbm.at[pl.ds(gl, SEG_ALIGN), :]
            copy = (pltpu.make_async_copy(local_piece, global_piece, sem) if to_global
                    else pltpu.make_async_copy(global_piece, local_piece, sem))
            action(copy)
            return inner

        lax.fori_loop(0, n_ref[first + e], per_piece, 0)
        return carry

    lax.fori_loop(0, N_EXPERTS, per_expert, 0)


def _dispatch_kernel(local_ref, global_ref, n_ref, fill_ref, xn_ref, pos_ref, xs_out,
                     loc_buf, zero_buf, seg_sem, fill_sem, *, n_blocks):
    i = pl.program_id(0)

    @pl.when(i == 0)
    def _():
        zero_buf[...] = jnp.zeros_like(zero_buf)

        def fill_copy(j):
            row = pl.multiple_of(j * MOE_TILE, MOE_TILE)
            return pltpu.make_async_copy(zero_buf, xs_out.at[pl.ds(row, MOE_TILE), :], fill_sem)

        def start(j, carry):
            @pl.when(fill_ref[j] > 0)
            def _():
                fill_copy(j).start()
            return carry

        def wait(j, carry):
            @pl.when(fill_ref[j] > 0)
            def _():
                fill_copy(j).wait()
            return carry

        lax.fori_loop(0, n_blocks, start, 0)
        lax.fori_loop(0, n_blocks, wait, 0)

    pos = pos_ref[...]
    row = lax.broadcasted_iota(jnp.int32, (LOC_ROWS, pos.shape[1]), 0)
    hit = (row == pos[0:1]) | (row == pos[1:2]) | (row == pos[2:3]) | (row == pos[3:4])
    loc_buf[...] = jnp.dot(jnp.where(hit, 1.0, 0.0).astype(BF16), xn_ref[...], preferred_element_type=F32)

    copies = functools.partial(_segment_copies, i, local_ref, global_ref, n_ref, loc_buf, xs_out, seg_sem, True)
    copies(lambda c: c.start())
    copies(lambda c: c.wait())


def _dispatch(seg_local, seg_global, seg_pieces, fill_flags, xn, pos, n_blocks):
    n = xn.shape[0]
    tile = ROUTE_TILE
    return pl.pallas_call(
        functools.partial(_dispatch_kernel, n_blocks=n_blocks),
        grid_spec=pltpu.PrefetchScalarGridSpec(
            num_scalar_prefetch=4,
            grid=(n // tile,),
            in_specs=[pl.BlockSpec((tile, D_MODEL), lambda i, *_: (i, 0)),
                      pl.BlockSpec((TOP_K, tile), lambda i, *_: (0, i))],
            out_specs=pl.BlockSpec(memory_space=pl.ANY),
            scratch_shapes=[pltpu.VMEM((LOC_ROWS, D_MODEL), F32),
                            pltpu.VMEM((MOE_TILE, D_MODEL), F32),
                            pltpu.SemaphoreType.DMA, pltpu.SemaphoreType.DMA]),
        out_shape=jax.ShapeDtypeStruct((n_blocks * MOE_TILE, D_MODEL), F32),
        compiler_params=_params(("arbitrary",)),
        name="moe_dispatch",
    )(seg_local, seg_global, seg_pieces, fill_flags, xn, pos)


def _expert_kernel(be_ref, nu_ref, xs_ref, wu_ref, bu_ref, wd_ref, bd_ref, ys_ref):
    del be_ref

    @pl.when(pl.program_id(0) < nu_ref[0])
    def _():
        x = xs_ref[...].astype(BF16)
        glu = jnp.dot(x, wu_ref[:, 0:D_FF], preferred_element_type=F32) + bu_ref[:, 0:D_FF]
        lin = jnp.dot(x, wu_ref[:, D_FF:], preferred_element_type=F32) + bu_ref[:, D_FF:]
        glu = jnp.minimum(glu, SWIGLU_LIMIT)
        lin = jnp.clip(lin, -SWIGLU_LIMIT, SWIGLU_LIMIT)
        act = glu * jax.nn.sigmoid(SWIGLU_ALPHA * glu) * (lin + 1.0)
        ys_ref[...] = jnp.dot(act.astype(BF16), wd_ref[...], preferred_element_type=F32) + bd_ref[...]

    @pl.when(pl.program_id(0) >= nu_ref[0])
    def _():
        ys_ref[...] = jnp.zeros_like(ys_ref)


def _expert_ffn(block_expert, n_used, xs, wu, bu, wd, bd):
    n_rows = xs.shape[0]
    tile = MOE_TILE
    rows = pl.BlockSpec((tile, D_MODEL), lambda j, be, nu: (jnp.minimum(j, nu[0] - 1), 0))
    per_expert = lambda a, b: pl.BlockSpec((None, a, b), lambda j, be, nu: (be[j], 0, 0))
    return pl.pallas_call(
        _expert_kernel,
        grid_spec=pltpu.PrefetchScalarGridSpec(
            num_scalar_prefetch=2,
            grid=(n_rows // tile,),
            in_specs=[rows, per_expert(D_MODEL, 2 * D_FF), per_expert(1, 2 * D_FF),
                      per_expert(D_FF, D_MODEL), per_expert(1, D_MODEL)],
            out_specs=pl.BlockSpec((tile, D_MODEL), lambda j, be, nu: (j, 0))),
        out_shape=jax.ShapeDtypeStruct((n_rows, D_MODEL), F32),
        compiler_params=_params(("arbitrary",)),
        name="expert_ffn",
    )(block_expert, n_used, xs, wu, bu, wd, bd)


def _combine_kernel(local_ref, global_ref, n_ref, col_ref, h_ref, ys_hbm, g_ref, o_ref,
                    loc_buf, seg_sem):
    i = pl.program_id(0)

    @pl.when(i == 0)
    def _():
        loc_buf[...] = jnp.zeros_like(loc_buf)

    copies = functools.partial(_segment_copies, i, local_ref, global_ref, n_ref, loc_buf, ys_hbm, seg_sem, False)
    copies(lambda c: c.start())
    copies(lambda c: c.wait())

    col = col_ref[...]
    lane = lax.broadcasted_iota(jnp.int32, (col.shape[0], LOC_ROWS), 1)
    weights = jnp.zeros((col.shape[0], LOC_ROWS), F32)
    for k in range(TOP_K):
        weights = weights + jnp.where(lane == col[:, k:k + 1].astype(jnp.int32),
                                      col[:, TOP_K + k:TOP_K + k + 1], 0.0)
    h = h_ref[...] + jnp.dot(weights.astype(BF16), loc_buf[...].astype(BF16), preferred_element_type=F32)
    o_ref[...] = h * lax.rsqrt(jnp.mean(h * h, axis=-1, keepdims=True) + NORM_EPS) * g_ref[...]


def _combine(seg_local, seg_global, seg_pieces, col, h, ys, g):
    n = h.shape[0]
    tile = ROUTE_TILE
    return pl.pallas_call(
        _combine_kernel,
        grid_spec=pltpu.PrefetchScalarGridSpec(
            num_scalar_prefetch=3,
            grid=(n // tile,),
            in_specs=[pl.BlockSpec((tile, LANES), lambda i, *_: (i, 0)),
                      pl.BlockSpec((tile, D_MODEL), lambda i, *_: (i, 0)),
                      pl.BlockSpec(memory_space=pl.ANY),
                      pl.BlockSpec((1, D_MODEL), lambda i, *_: (0, 0))],
            out_specs=pl.BlockSpec((tile, D_MODEL), lambda i, *_: (i, 0)),
            scratch_shapes=[pltpu.VMEM((LOC_ROWS, D_MODEL), F32), pltpu.SemaphoreType.DMA]),
        out_shape=jax.ShapeDtypeStruct((n, D_MODEL), F32),
        compiler_params=_params(("arbitrary",)),
        name="moe_combine",
    )(seg_local, seg_global, seg_pieces, col, h, ys, g)


def kernel(x, meta_tokens, attn_norm_g, w_in, diff_lambda, diff_subln_g, w_out, ffn_norm_g,
           w_router, b_router, w_up, b_up, w_down, b_down, final_norm_g):
    b, seq, d = x.shape
    assert d == D_MODEL and seq % PROJ_TILE == 0 and seq % RET_CHUNK == 0 and seq % ATTN_TILE == 0
    assert w_in.shape[0] == 1, "one layer"
    n_tok = b * seq
    assert n_tok % ROUTE_TILE == 0

    w_in_b = w_in[0].astype(BF16)
    w_out_b = w_out[0].astype(BF16)
    w_up_b = _split_up_weights(w_up[0])
    b_up_s = jnp.concatenate([b_up[0][..., 0::2], b_up[0][..., 1::2]], axis=-1)[:, None, :]
    w_down_b = w_down[0].astype(BF16)
    b_down_s = b_down[0][:, None, :]

    diff_inv_freq = ROPE_THETA ** (-jnp.arange(0, HEAD_DIM, 2, dtype=F32) / HEAD_DIM)
    ret_inv_freq = ROPE_THETA ** (-jnp.linspace(0.0, 1.0, HEAD_DIM // 2, dtype=F32))
    pos_x = jnp.arange(seq, dtype=F32) + N_META
    pos_m = jnp.arange(META_ROWS, dtype=F32) - (META_ROWS - N_META)
    tables_x = _rope_tables(pos_x, diff_inv_freq) + _rope_tables(pos_x, ret_inv_freq)
    tables_m = _rope_tables(pos_m, diff_inv_freq) + _rope_tables(pos_m, ret_inv_freq)

    g_attn = attn_norm_g[0][None, :]
    x2 = x.reshape(n_tok, D_MODEL)
    meta_rows = jnp.concatenate(
        [jnp.zeros((META_ROWS - N_META, D_MODEL), x.dtype), meta_tokens.astype(x.dtype)], axis=0)
    dq, dk, dv, rq, rk, rv, rg = _in_projection(x2, g_attn, w_in_b, tables_x, PROJ_TILE, seq // PROJ_TILE)
    _, dk_m, dv_m, _, rk_m, rv_m, _ = _in_projection(meta_rows, g_attn, w_in_b, tables_m, META_ROWS, 1)

    per_batch = lambda t: t.reshape(b, seq, t.shape[-1])
    diff_out = _diff_attention(per_batch(dq), per_batch(dk), per_batch(dv), dk_m, dv_m,
                               diff_lambda[0], diff_subln_g[0][:, None])
    ret_out = _retention(per_batch(rq), per_batch(rk), per_batch(rv), per_batch(rg), rk_m, rv_m)

    h, xn, pos, col, counts = _out_proj_and_route(
        x2, diff_out.reshape(n_tok, DIFF_WIDTH), ret_out.reshape(n_tok, RET_WIDTH),
        w_out_b, ffn_norm_g[0][None, :], w_router[0].T, b_router[0][:, None])

    n_tiles = n_tok // ROUTE_TILE
    n_blocks = -(-(n_tok * TOP_K + n_tiles * N_EXPERTS * (SEG_ALIGN - 1) + N_EXPERTS * (MOE_TILE - 1))
                 // MOE_TILE)
    seg_rows = (counts[:, :, 0] + SEG_ALIGN - 1) // SEG_ALIGN * SEG_ALIGN
    seg_local = jnp.cumsum(seg_rows, axis=1) - seg_rows
    blocks_per = (jnp.sum(seg_rows, axis=0) + MOE_TILE - 1) // MOE_TILE
    block_end = jnp.cumsum(blocks_per)
    group_start = (block_end - blocks_per) * MOE_TILE
    seg_global = group_start[None, :] + jnp.cumsum(seg_rows, axis=0) - seg_rows
    flat = lambda t: t.reshape(-1).astype(jnp.int32)
    seg_local, seg_global, seg_pieces = flat(seg_local), flat(seg_global), flat(seg_rows // SEG_ALIGN)

    n_used = block_end[-1:].astype(jnp.int32)
    all_blocks = jnp.arange(n_blocks, dtype=jnp.int32)
    block_ids = jnp.minimum(all_blocks, n_used[0] - 1)
    block_expert = jnp.minimum(jnp.sum(block_end[None, :] <= block_ids[:, None], axis=1),
                               N_EXPERTS - 1).astype(jnp.int32)
    is_group_end = jnp.any((block_end[None, :] == all_blocks[:, None] + 1) & (blocks_per[None, :] > 0), axis=1)
    fill_flags = (is_group_end | (all_blocks >= n_used[0])).astype(jnp.int32)

    xs = _dispatch(seg_local, seg_global, seg_pieces, fill_flags, xn, pos, n_blocks)
    ys = _expert_ffn(block_expert, n_used, xs, w_up_b, b_up_s, w_down_b, b_down_s)
    out = _combine(seg_local, seg_global, seg_pieces, col, h, ys, final_norm_g[None, :])
    return out.reshape(b, seq, D_MODEL)
```

```python
import functools
import math

import jax
import jax.numpy as jnp
import numpy as np
from jax import lax
from jax.experimental import pallas as pl
from jax.experimental.pallas import tpu as pltpu

F32 = jnp.float32
BF16 = jnp.bfloat16

D_MODEL = 1024
N_META = 16
ROPE_THETA = 10000.0
NORM_EPS = 1e-5
DIFF_HEADS = 4
HEAD_DIM = 64
VALUE_DIM = 128
DIFF_WIDTH = DIFF_HEADS * VALUE_DIM
RET_HEADS = 4
RET_WIDTH = RET_HEADS * VALUE_DIM
QK_WIDTH = RET_HEADS * HEAD_DIM
IN_PROJ_WIDTH = 3 * DIFF_WIDTH + 2 * QK_WIDTH + 2 * RET_WIDTH
LAMBDA_INIT = 0.8 - 0.6 * math.exp(-0.3 * 0)
N_EXPERTS = 32
TOP_K = 4
D_FF = D_MODEL
SWIGLU_ALPHA = 1.702
SWIGLU_LIMIT = 7.0

LANES = 128
MXU_WIDTH = 256
META_ROWS = 128
VMEM_LIMIT = 56 * 1024 * 1024

ONES_ROWS = 16
LOG2_E = math.log2(math.e)

PROJ_TILE = 512
ATTN_TILE = 512
RET_CHUNK = 256
ROUTE_TILE = 512
MOE_TILE = 512
SEG_ALIGN = 8
BIG_PIECE = 4 * SEG_ALIGN
LOC_ROWS = ROUTE_TILE * TOP_K + N_EXPERTS * SEG_ALIGN

_LOG_GAMMA = [float(v) for v in np.log1p(-np.exp2(-5.0 - np.arange(RET_HEADS, dtype=np.float32)))]


def _params(sem):
    return pltpu.CompilerParams(dimension_semantics=sem, vmem_limit_bytes=VMEM_LIMIT)


def _rope(t, cos, sin_signed):
    lane = lax.broadcasted_iota(jnp.int32, t.shape, 1)
    first_half = (lane % HEAD_DIM) < (HEAD_DIM // 2)
    partner = jnp.where(first_half, pltpu.roll(t, LANES - HEAD_DIM // 2, 1),
                        pltpu.roll(t, HEAD_DIM // 2, 1))
    return t * cos + partner * sin_signed


def _inproj_kernel(x_ref, g_ref, w_ref, cd_ref, sd_ref, cr_ref, sr_ref,
                   dq_ref, dk_ref, dv_ref, rq_ref, rk_ref, rv_ref, rg_ref):
    x = x_ref[...]
    ms = jnp.mean(x * x, axis=-1, keepdims=True)
    hn = (x * lax.rsqrt(ms + NORM_EPS) * g_ref[...]).astype(BF16)
    cd, sd, cr, sr = cd_ref[...], sd_ref[...], cr_ref[...], sr_ref[...]
    scale = HEAD_DIM ** -0.5

    def proj(col, width):
        return jnp.dot(hn, w_ref[:, col:col + width], preferred_element_type=F32)

    col = 0
    for out_ref, width, cos, sin, mul in (
            (dq_ref, DIFF_WIDTH, cd, sd, scale * LOG2_E), (dk_ref, DIFF_WIDTH, cd, sd, None),
            (dv_ref, DIFF_WIDTH, None, None, None),
            (rq_ref, QK_WIDTH, cr, sr, None), (rk_ref, QK_WIDTH, cr, sr, scale),
            (rv_ref, RET_WIDTH, None, None, None), (rg_ref, RET_WIDTH, None, None, None)):
        for c in range(0, width, MXU_WIDTH):
            wide = proj(col + c, MXU_WIDTH)
            for half in range(0, MXU_WIDTH, LANES):
                p = wide[:, half:half + LANES]
                if cos is not None:
                    p = _rope(p, cos, sin)
                if mul is not None:
                    p = p * mul
                out_ref[:, c + half:c + half + LANES] = p.astype(out_ref.dtype)
        col += width


def _in_projection(rows, g, w_bf16, tables, tile, seq_tiles):
    n = rows.shape[0]
    row_spec = lambda w: pl.BlockSpec((tile, w), lambda i: (i, 0))
    table_spec = pl.BlockSpec((tile, LANES), lambda i: (i % seq_tiles, 0))
    widths = (DIFF_WIDTH, DIFF_WIDTH, DIFF_WIDTH, QK_WIDTH, QK_WIDTH, RET_WIDTH, RET_WIDTH)
    return pl.pallas_call(
        _inproj_kernel,
        grid=(n // tile,),
        in_specs=[row_spec(D_MODEL),
                  pl.BlockSpec((1, D_MODEL), lambda i: (0, 0)),
                  pl.BlockSpec((D_MODEL, IN_PROJ_WIDTH), lambda i: (0, 0)),
                  table_spec, table_spec, table_spec, table_spec],
        out_specs=[row_spec(w) for w in widths],
        out_shape=[jax.ShapeDtypeStruct((n, w), BF16) for w in widths],
        compiler_params=_params(("parallel",)),
        name="in_projection",
    )(rows, g, w_bf16, *tables)


def _rope_tables(pos, inv_freq):
    ang = pos[:, None] * inv_freq[None, :]
    cos = jnp.tile(jnp.cos(ang), (1, LANES // (HEAD_DIM // 2)))
    sin = jnp.sin(ang)
    sin_signed = jnp.tile(jnp.concatenate([-sin, sin], axis=1), (1, LANES // HEAD_DIM))
    return cos, sin_signed


def _transpose_bf16(t):
    return t.astype(F32).T.astype(BF16)


def _with_ones_rows(vt):
    return jnp.concatenate([vt, jnp.ones((ONES_ROWS, vt.shape[1]), vt.dtype)], axis=0)


def _attn_kernel(q_ref, k_ref, v_ref, km_ref, vm_ref, lp_ref, sg_ref, o_ref,
                 vt_sc, m_sc, acc_sc, sa_sc, sb_sc, *, tile, n_chunks):
    i = pl.program_id(2)

    @pl.when(i == 0)
    def _():
        def transpose_chunk(j, carry):
            start = pl.multiple_of(j * tile, tile)
            vt_sc[j] = _with_ones_rows(_transpose_bf16(v_ref[pl.ds(start, tile), :]))
            return carry
        lax.fori_loop(0, n_chunks, transpose_chunk, 0)

    qt = q_ref[...].astype(F32).T
    sub = lax.broadcasted_iota(jnp.int32, qt.shape, 0)
    qcat = jnp.concatenate([jnp.where(sub < HEAD_DIM, qt, 0.0).astype(BF16),
                            jnp.where(sub >= HEAD_DIM, qt, 0.0).astype(BF16)], axis=1)

    meta_row = lax.broadcasted_iota(jnp.int32, (META_ROWS, 2 * tile), 0) >= META_ROWS - N_META
    s = jnp.where(meta_row, jnp.dot(km_ref[...], qcat, preferred_element_type=F32), -jnp.inf)
    m = jnp.max(s, axis=0, keepdims=True)
    m_sc[...] = m
    acc_sc[...] = jnp.dot(_with_ones_rows(_transpose_bf16(vm_ref[...])), jnp.exp2(s - m).astype(BF16),
                          preferred_element_type=F32)

    def scores(j):
        start = pl.multiple_of(j * tile, tile)
        return jnp.dot(k_ref[pl.ds(start, tile), :], qcat, preferred_element_type=F32)

    def absorb(s_ref, j, masked):
        s = s_ref[...]
        if masked:
            key = lax.broadcasted_iota(jnp.int32, s.shape, 0)
            lane = lax.broadcasted_iota(jnp.int32, s.shape, 1)
            s = jnp.where(key <= jnp.where(lane >= tile, lane - tile, lane), s, -jnp.inf)
        m_old = m_sc[...]
        m_new = jnp.maximum(m_old, jnp.max(s, axis=0, keepdims=True))
        p = jnp.exp2(s - m_new).astype(BF16)
        acc_sc[...] = jnp.exp2(m_old - m_new) * acc_sc[...] + jnp.dot(vt_sc[j], p, preferred_element_type=F32)
        m_sc[...] = m_new

    sa_sc[...] = scores(0)

    def pair(t, carry):
        sb_sc[...] = scores(2 * t + 1)
        absorb(sa_sc, 2 * t, False)
        sa_sc[...] = scores(2 * t + 2)
        absorb(sb_sc, 2 * t + 1, False)
        return carry

    lax.fori_loop(0, i // 2, pair, 0)

    @pl.when(i % 2 == 0)
    def _():
        absorb(sa_sc, i, True)

    @pl.when(i % 2 == 1)
    def _():
        sb_sc[...] = scores(i)
        absorb(sa_sc, i - 1, False)
        absorb(sb_sc, i, True)

    lp = lp_ref[...]
    lam = (jnp.exp(jnp.sum(lp[0:1] * lp[1:2], axis=-1, keepdims=True))
           - jnp.exp(jnp.sum(lp[2:3] * lp[3:4], axis=-1, keepdims=True)) + LAMBDA_INIT)
    num, den = acc_sc[0:VALUE_DIM, :], acc_sc[VALUE_DIM:VALUE_DIM + 1, :]
    o = num[:, :tile] / den[:, :tile] - lam * (num[:, tile:] / den[:, tile:])
    o = o * lax.rsqrt(jnp.mean(o * o, axis=0, keepdims=True) + NORM_EPS)
    o_ref[...] = (o * sg_ref[...] * (1.0 - LAMBDA_INIT)).T.astype(o_ref.dtype)


def _diff_attention(dq, dk, dv, km, vm, lam_params, subln_g):
    b, seq, _ = dq.shape
    tile = ATTN_TILE
    n_chunks = seq // tile
    head_rows = pl.BlockSpec((None, tile, LANES), lambda bi, h, i: (bi, i, h))
    head_seq = pl.BlockSpec((None, seq, LANES), lambda bi, h, i: (bi, 0, h))
    head_meta = pl.BlockSpec((META_ROWS, LANES), lambda bi, h, i: (0, h))
    return pl.pallas_call(
        functools.partial(_attn_kernel, tile=tile, n_chunks=n_chunks),
        grid=(b, DIFF_HEADS, n_chunks),
        in_specs=[head_rows, head_seq, head_seq, head_meta, head_meta,
                  pl.BlockSpec((4, HEAD_DIM), lambda bi, h, i: (0, 0)),
                  pl.BlockSpec((VALUE_DIM, 1), lambda bi, h, i: (0, 0))],
        out_specs=head_rows,
        out_shape=jax.ShapeDtypeStruct((b, seq, DIFF_WIDTH), BF16),
        scratch_shapes=[pltpu.VMEM((n_chunks, VALUE_DIM + ONES_ROWS, tile), BF16),
                        pltpu.VMEM((1, 2 * tile), F32),
                        pltpu.VMEM((VALUE_DIM + ONES_ROWS, 2 * tile), F32),
                        pltpu.VMEM((tile, 2 * tile), F32), pltpu.VMEM((tile, 2 * tile), F32)],
        compiler_params=_params(("parallel", "parallel", "arbitrary")),
        name="diff_attention",
    )(dq, dk, dv, km, vm, lam_params, subln_g)


def _decay_col(log_gamma, n, offset_fn):
    pos = lax.broadcasted_iota(jnp.int32, (n, 1), 0).astype(F32)
    return jnp.exp(log_gamma * offset_fn(pos))


def _ret_kernel(q_ref, k_ref, v_ref, g_ref, km_ref, vm_ref, o_ref, state_sc, *, chunk):
    c = pl.program_id(1)

    @pl.when(c == 0)
    def _():
        km = km_ref[...].astype(F32)
        for h in range(RET_HEADS):
            pair = h // 2
            kt = km[:, pair * LANES:(pair + 1) * LANES].T.astype(BF16)
            kd = _decay_col(_LOG_GAMMA[h], META_ROWS, lambda p: (META_ROWS - 1) - p)
            vd = (vm_ref[:, h * VALUE_DIM:(h + 1) * VALUE_DIM].astype(F32) * kd).astype(BF16)
            state_sc[h] = jnp.dot(kt, vd, preferred_element_type=F32)

    row = lax.broadcasted_iota(jnp.int32, (chunk, chunk), 0)
    col = lax.broadcasted_iota(jnp.int32, (chunk, chunk), 1)
    rel = (row - col).astype(F32)
    lane = lax.broadcasted_iota(jnp.int32, (chunk, LANES), 1)
    for pair in range(RET_HEADS // 2):
        q_pair = q_ref[:, pair * LANES:(pair + 1) * LANES]
        k_pair = k_ref[:, pair * LANES:(pair + 1) * LANES]
        kt = k_pair.astype(F32).T.astype(BF16)
        for h in (2 * pair, 2 * pair + 1):
            lg = _LOG_GAMMA[h]
            own = (lane < HEAD_DIM) if h % 2 == 0 else (lane >= HEAD_DIM)
            qm = jnp.where(own, q_pair, jnp.zeros_like(q_pair))
            v = v_ref[:, h * VALUE_DIM:(h + 1) * VALUE_DIM]
            s = lax.dot_general(qm, k_pair, (((1,), (1,)), ((), ())), preferred_element_type=F32)
            decay = jnp.where(rel >= 0, jnp.exp(lg * jnp.maximum(rel, 0.0)), 0.0)
            y = jnp.dot((s * decay).astype(BF16), v, preferred_element_type=F32)
            qd = _decay_col(lg, chunk, lambda p: p + 1.0)
            state = state_sc[h]
            y = y + jnp.dot((qm.astype(F32) * qd).astype(BF16), state.astype(BF16),
                            preferred_element_type=F32)
            kd = _decay_col(lg, chunk, lambda p: (chunk - 1) - p)
            vd = (v.astype(F32) * kd).astype(BF16)
            state_sc[h] = math.exp(lg * chunk) * state + jnp.dot(kt, vd, preferred_element_type=F32)
            y = y * lax.rsqrt(jnp.mean(y * y, axis=-1, keepdims=True) + NORM_EPS)
            gate = g_ref[:, h * VALUE_DIM:(h + 1) * VALUE_DIM].astype(F32)
            o_ref[:, h * VALUE_DIM:(h + 1) * VALUE_DIM] = (y * gate * jax.nn.sigmoid(gate)).astype(o_ref.dtype)


def _retention(rq, rk, rv, rg, km, vm):
    b, seq, _ = rq.shape
    chunk = RET_CHUNK
    rows = lambda w: pl.BlockSpec((None, chunk, w), lambda bi, c: (bi, c, 0))
    return pl.pallas_call(
        functools.partial(_ret_kernel, chunk=chunk),
        grid=(b, seq // chunk),
        in_specs=[rows(QK_WIDTH), rows(QK_WIDTH), rows(RET_WIDTH), rows(RET_WIDTH),
                  pl.BlockSpec((META_ROWS, QK_WIDTH), lambda bi, c: (0, 0)),
                  pl.BlockSpec((META_ROWS, RET_WIDTH), lambda bi, c: (0, 0))],
        out_specs=rows(RET_WIDTH),
        out_shape=jax.ShapeDtypeStruct((b, seq, RET_WIDTH), BF16),
        scratch_shapes=[pltpu.VMEM((RET_HEADS, LANES, VALUE_DIM), F32)],
        compiler_params=_params(("parallel", "arbitrary")),
        name="retention",
    )(rq, rk, rv, rg, km, vm)


def _route_kernel(x_ref, d_ref, r_ref, wo_ref, g_ref, wrt_ref, brt_ref,
                  h_ref, xn_ref, pos_ref, col_ref, cnt_ref, *, tile):
    h = (x_ref[...]
         + jnp.dot(d_ref[...], wo_ref[0:DIFF_WIDTH, :], preferred_element_type=F32)
         + jnp.dot(r_ref[...], wo_ref[DIFF_WIDTH:, :], preferred_element_type=F32))
    h_ref[...] = h
    xn = h * lax.rsqrt(jnp.mean(h * h, axis=-1, keepdims=True) + NORM_EPS) * g_ref[...]
    xn_ref[...] = xn.astype(BF16)
    logits = lax.dot_general(wrt_ref[...], xn, (((1,), (1,)), ((), ())), preferred_element_type=F32,
                             precision=lax.Precision.HIGHEST) + brt_ref[...]

    expert = lax.broadcasted_iota(jnp.int32, logits.shape, 0)
    work = logits
    vals, hots = [], []
    for k in range(TOP_K):
        v = jnp.max(work, axis=0, keepdims=True)
        idx = jnp.min(jnp.where(work == v, expert, N_EXPERTS), axis=0, keepdims=True)
        hot = expert == idx
        work = jnp.where(hot, -jnp.inf, work)
        vals.append(v)
        hots.append(hot.astype(F32))
    exps = [jnp.exp(v - vals[0]) for v in vals]
    denom = exps[0] + exps[1] + exps[2] + exps[3]
    gates = [e / denom for e in exps]

    chosen = hots[0] + hots[1] + hots[2] + hots[3]
    c = lax.broadcasted_iota(jnp.int32, (tile, tile), 0)
    r = lax.broadcasted_iota(jnp.int32, (tile, tile), 1)
    earlier = (c < r).astype(BF16)
    before = jnp.dot(chosen.astype(BF16), earlier, preferred_element_type=F32)
    cnt = jnp.sum(chosen, axis=1, keepdims=True)
    cnt_pad = jnp.ceil(cnt / SEG_ALIGN) * SEG_ALIGN
    er = lax.broadcasted_iota(jnp.int32, (N_EXPERTS, N_EXPERTS), 0)
    ec = lax.broadcasted_iota(jnp.int32, (N_EXPERTS, N_EXPERTS), 1)
    seg_start = jnp.dot((ec < er).astype(F32), jnp.broadcast_to(cnt_pad, (N_EXPERTS, LANES)),
                        preferred_element_type=F32, precision=lax.Precision.HIGHEST)[:, 0:1]
    base = seg_start + before
    pos = [jnp.sum(hot * base, axis=0, keepdims=True) for hot in hots]
    for k in range(TOP_K):
        pos_ref[k:k + 1, :] = pos[k].astype(jnp.int32)
    cnt_ref[...] = jnp.broadcast_to(cnt, (N_EXPERTS, LANES)).astype(jnp.int32)
    stacked = jnp.concatenate(pos + gates + [jnp.zeros((LANES - 2 * TOP_K, tile), F32)], axis=0)
    col_ref[...] = stacked.T


def _out_proj_and_route(x, diff, ret, wo, g, wrt, brt):
    n = x.shape[0]
    tile = ROUTE_TILE
    rows = lambda w: pl.BlockSpec((tile, w), lambda i: (i, 0))
    full = lambda a, b: pl.BlockSpec((a, b), lambda i: (0, 0))
    return pl.pallas_call(
        functools.partial(_route_kernel, tile=tile),
        grid=(n // tile,),
        in_specs=[rows(D_MODEL), rows(DIFF_WIDTH), rows(RET_WIDTH), full(D_MODEL, D_MODEL),
                  full(1, D_MODEL), full(N_EXPERTS, D_MODEL), full(N_EXPERTS, 1)],
        out_specs=[rows(D_MODEL), rows(D_MODEL),
                   pl.BlockSpec((TOP_K, tile), lambda i: (0, i)), rows(LANES),
                   pl.BlockSpec((None, N_EXPERTS, LANES), lambda i: (i, 0, 0))],
        out_shape=[jax.ShapeDtypeStruct((n, D_MODEL), F32), jax.ShapeDtypeStruct((n, D_MODEL), BF16),
                   jax.ShapeDtypeStruct((TOP_K, n), jnp.int32), jax.ShapeDtypeStruct((n, LANES), F32),
                   jax.ShapeDtypeStruct((n // tile, N_EXPERTS, LANES), jnp.int32)],
        compiler_params=_params(("parallel",)),
        name="out_proj_route",
    )(x, diff, ret, wo, g, wrt, brt)


def _split_up_kernel(w_ref, o_ref):
    group = 2 * LANES
    src = lax.broadcasted_iota(jnp.int32, (group, group), 0)
    dst = lax.broadcasted_iota(jnp.int32, (group, group), 1)
    select = (src == jnp.where(dst < LANES, 2 * dst, 2 * (dst - LANES) + 1)).astype(BF16)
    for c in range(2 * D_FF // group):
        part = jnp.dot(w_ref[:, c * group:(c + 1) * group].astype(BF16), select,
                       preferred_element_type=F32).astype(BF16)
        o_ref[:, c * LANES:(c + 1) * LANES] = part[:, :LANES]
        o_ref[:, D_FF + c * LANES:D_FF + (c + 1) * LANES] = part[:, LANES:]


def _split_up_weights(w_up):
    n_exp, d, f2 = w_up.shape
    tile = 256
    spec = pl.BlockSpec((None, tile, f2), lambda e, i: (e, i, 0))
    return pl.pallas_call(
        _split_up_kernel,
        grid=(n_exp, d // tile),
        in_specs=[spec],
        out_specs=spec,
        out_shape=jax.ShapeDtypeStruct((n_exp, d, f2), BF16),
        compiler_params=_params(("parallel", "parallel")),
        name="split_up_weights",
    )(w_up)


def _segment_copies(tables, tile_index, local_buf, global_hbm, sem, to_global, action):
    local_ref, global_ref, n_big_ref, n_small_ref = tables
    first = tile_index * N_EXPERTS

    def pieces(local_row, global_row, count, rows):
        def per_piece(p, inner):
            lo = pl.multiple_of(local_row + p * rows, SEG_ALIGN)
            gl = pl.multiple_of(global_row + p * rows, SEG_ALIGN)
            local_piece = local_buf.at[pl.ds(lo, rows), :]
            global_piece = global_hbm.at[pl.ds(gl, rows), :]
            copy = (pltpu.make_async_copy(local_piece, global_piece, sem) if to_global
                    else pltpu.make_async_copy(global_piece, local_piece, sem))
            action(copy)
            return inner

        lax.fori_loop(0, count, per_piece, 0)

    def per_expert(e, carry):
        local_row, global_row = local_ref[first + e], global_ref[first + e]
        big_rows = n_big_ref[first + e] * BIG_PIECE
        pieces(local_row, global_row, n_big_ref[first + e], BIG_PIECE)
        pieces(local_row + big_rows, global_row + big_rows, n_small_ref[first + e], SEG_ALIGN)
        return carry

    lax.fori_loop(0, N_EXPERTS, per_expert, 0)


def _dispatch_kernel(local_ref, global_ref, n_big_ref, n_small_ref, fill_ref, xn_ref, pos_ref, xs_out,
                     loc_buf, zero_buf, seg_sem, fill_sem, *, n_blocks, n_tiles):
    i = pl.program_id(0)
    slot = i % 2

    @pl.when(i == 0)
    def _():
        zero_buf[...] = jnp.zeros_like(zero_buf)

        def fill_copy(j):
            row = pl.multiple_of(j * MOE_TILE, MOE_TILE)
            return pltpu.make_async_copy(zero_buf, xs_out.at[pl.ds(row, MOE_TILE), :], fill_sem)

        def start(j, carry):
            @pl.when(fill_ref[j] > 0)
            def _():
                fill_copy(j).start()
            return carry

        def wait(j, carry):
            @pl.when(fill_ref[j] > 0)
            def _():
                fill_copy(j).wait()
            return carry

        lax.fori_loop(0, n_blocks, start, 0)
        lax.fori_loop(0, n_blocks, wait, 0)

    pos = pos_ref[...]
    row = lax.broadcasted_iota(jnp.int32, (LOC_ROWS, pos.shape[1]), 0)
    hit = (row == pos[0:1]) | (row == pos[1:2]) | (row == pos[2:3]) | (row == pos[3:4])
    loc_buf[slot] = jnp.dot(jnp.where(hit, 1.0, 0.0).astype(BF16), xn_ref[...], preferred_element_type=F32)

    tables = (local_ref, global_ref, n_big_ref, n_small_ref)

    def copies(tile_index, buf_slot, action):
        _segment_copies(tables, tile_index, loc_buf.at[buf_slot], xs_out, seg_sem.at[buf_slot], True, action)

    copies(i, slot, lambda c: c.start())

    @pl.when(i > 0)
    def _():
        copies(i - 1, 1 - slot, lambda c: c.wait())

    @pl.when(i == n_tiles - 1)
    def _():
        copies(i, slot, lambda c: c.wait())


def _dispatch(tables, fill_flags, xn, pos, n_blocks):
    n = xn.shape[0]
    tile = ROUTE_TILE
    return pl.pallas_call(
        functools.partial(_dispatch_kernel, n_blocks=n_blocks, n_tiles=n // tile),
        grid_spec=pltpu.PrefetchScalarGridSpec(
            num_scalar_prefetch=5,
            grid=(n // tile,),
            in_specs=[pl.BlockSpec((tile, D_MODEL), lambda i, *_: (i, 0)),
                      pl.BlockSpec((TOP_K, tile), lambda i, *_: (0, i))],
            out_specs=pl.BlockSpec(memory_space=pl.ANY),
            scratch_shapes=[pltpu.VMEM((2, LOC_ROWS, D_MODEL), F32),
                            pltpu.VMEM((MOE_TILE, D_MODEL), F32),
                            pltpu.SemaphoreType.DMA((2,)), pltpu.SemaphoreType.DMA]),
        out_shape=jax.ShapeDtypeStruct((n_blocks * MOE_TILE, D_MODEL), F32),
        compiler_params=_params(("arbitrary",)),
        name="moe_dispatch",
    )(*tables, fill_flags, xn, pos)


def _expert_kernel(be_ref, nu_ref, xs_ref, wu_ref, bu_ref, wd_ref, bd_ref, ys_ref):
    del be_ref

    @pl.when(pl.program_id(0) < nu_ref[0])
    def _():
        x = xs_ref[...].astype(BF16)
        glu = jnp.dot(x, wu_ref[:, 0:D_FF], preferred_element_type=F32) + bu_ref[:, 0:D_FF]
        lin = jnp.dot(x, wu_ref[:, D_FF:], preferred_element_type=F32) + bu_ref[:, D_FF:]
        glu = jnp.minimum(glu, SWIGLU_LIMIT)
        lin = jnp.clip(lin, -SWIGLU_LIMIT, SWIGLU_LIMIT)
        act = glu * jax.nn.sigmoid(SWIGLU_ALPHA * glu) * (lin + 1.0)
        ys_ref[...] = jnp.dot(act.astype(BF16), wd_ref[...], preferred_element_type=F32) + bd_ref[...]

    @pl.when(pl.program_id(0) >= nu_ref[0])
    def _():
        ys_ref[...] = jnp.zeros_like(ys_ref)


def _expert_ffn(block_expert, n_used, xs, wu, bu, wd, bd):
    n_rows = xs.shape[0]
    tile = MOE_TILE
    rows = pl.BlockSpec((tile, D_MODEL), lambda j, be, nu: (jnp.minimum(j, nu[0] - 1), 0))
    per_expert = lambda a, b: pl.BlockSpec((None, a, b), lambda j, be, nu: (be[j], 0, 0))
    return pl.pallas_call(
        _expert_kernel,
        grid_spec=pltpu.PrefetchScalarGridSpec(
            num_scalar_prefetch=2,
            grid=(n_rows // tile,),
            in_specs=[rows, per_expert(D_MODEL, 2 * D_FF), per_expert(1, 2 * D_FF),
                      per_expert(D_FF, D_MODEL), per_expert(1, D_MODEL)],
            out_specs=pl.BlockSpec((tile, D_MODEL), lambda j, be, nu: (j, 0))),
        out_shape=jax.ShapeDtypeStruct((n_rows, D_MODEL), F32),
        compiler_params=_params(("arbitrary",)),
        name="expert_ffn",
    )(block_expert, n_used, xs, wu, bu, wd, bd)


def _combine_kernel(local_ref, global_ref, n_big_ref, n_small_ref, col_ref, h_ref, ys_hbm, g_ref, o_ref,
                    loc_buf, seg_sem, *, n_tiles):
    i = pl.program_id(0)
    slot = i % 2
    tables = (local_ref, global_ref, n_big_ref, n_small_ref)

    def copies(tile_index, buf_slot, action):
        _segment_copies(tables, tile_index, loc_buf.at[buf_slot], ys_hbm, seg_sem.at[buf_slot], False, action)

    @pl.when(i == 0)
    def _():
        loc_buf[...] = jnp.zeros_like(loc_buf)
        copies(0, 0, lambda c: c.start())

    @pl.when(i + 1 < n_tiles)
    def _():
        copies(i + 1, 1 - slot, lambda c: c.start())

    copies(i, slot, lambda c: c.wait())

    col = col_ref[...]
    lane = lax.broadcasted_iota(jnp.int32, (col.shape[0], LOC_ROWS), 1)
    weights = jnp.zeros((col.shape[0], LOC_ROWS), F32)
    for k in range(TOP_K):
        weights = weights + jnp.where(lane == col[:, k:k + 1].astype(jnp.int32),
                                      col[:, TOP_K + k:TOP_K + k + 1], 0.0)
    h = h_ref[...] + jnp.dot(weights.astype(BF16), loc_buf[slot].astype(BF16), preferred_element_type=F32)
    o_ref[...] = h * lax.rsqrt(jnp.mean(h * h, axis=-1, keepdims=True) + NORM_EPS) * g_ref[...]


def _combine(tables, col, h, ys, g):
    n = h.shape[0]
    tile = ROUTE_TILE
    return pl.pallas_call(
        functools.partial(_combine_kernel, n_tiles=n // tile),
        grid_spec=pltpu.PrefetchScalarGridSpec(
            num_scalar_prefetch=4,
            grid=(n // tile,),
            in_specs=[pl.BlockSpec((tile, LANES), lambda i, *_: (i, 0)),
                      pl.BlockSpec((tile, D_MODEL), lambda i, *_: (i, 0)),
                      pl.BlockSpec(memory_space=pl.ANY),
                      pl.BlockSpec((1, D_MODEL), lambda i, *_: (0, 0))],
            out_specs=pl.BlockSpec((tile, D_MODEL), lambda i, *_: (i, 0)),
            scratch_shapes=[pltpu.VMEM((2, LOC_ROWS, D_MODEL), F32), pltpu.SemaphoreType.DMA((2,))]),
        out_shape=jax.ShapeDtypeStruct((n, D_MODEL), F32),
        compiler_params=_params(("arbitrary",)),
        name="moe_combine",
    )(*tables, col, h, ys, g)


def kernel(x, meta_tokens, attn_norm_g, w_in, diff_lambda, diff_subln_g, w_out, ffn_norm_g,
           w_router, b_router, w_up, b_up, w_down, b_down, final_norm_g):
    b, seq, d = x.shape
    assert d == D_MODEL and seq % PROJ_TILE == 0 and seq % RET_CHUNK == 0 and seq % ATTN_TILE == 0
    assert w_in.shape[0] == 1, "one layer"
    n_tok = b * seq
    assert n_tok % ROUTE_TILE == 0

    w_in_b = w_in[0].astype(BF16)
    w_out_b = w_out[0].astype(BF16)
    w_up_b = _split_up_weights(w_up[0])
    b_up_s = jnp.concatenate([b_up[0][..., 0::2], b_up[0][..., 1::2]], axis=-1)[:, None, :]
    w_down_b = w_down[0].astype(BF16)
    b_down_s = b_down[0][:, None, :]

    diff_inv_freq = ROPE_THETA ** (-jnp.arange(0, HEAD_DIM, 2, dtype=F32) / HEAD_DIM)
    ret_inv_freq = ROPE_THETA ** (-jnp.linspace(0.0, 1.0, HEAD_DIM // 2, dtype=F32))
    pos_x = jnp.arange(seq, dtype=F32) + N_META
    pos_m = jnp.arange(META_ROWS, dtype=F32) - (META_ROWS - N_META)
    tables_x = _rope_tables(pos_x, diff_inv_freq) + _rope_tables(pos_x, ret_inv_freq)
    tables_m = _rope_tables(pos_m, diff_inv_freq) + _rope_tables(pos_m, ret_inv_freq)

    g_attn = attn_norm_g[0][None, :]
    x2 = x.reshape(n_tok, D_MODEL)
    meta_rows = jnp.concatenate(
        [jnp.zeros((META_ROWS - N_META, D_MODEL), x.dtype), meta_tokens.astype(x.dtype)], axis=0)
    dq, dk, dv, rq, rk, rv, rg = _in_projection(x2, g_attn, w_in_b, tables_x, PROJ_TILE, seq // PROJ_TILE)
    _, dk_m, dv_m, _, rk_m, rv_m, _ = _in_projection(meta_rows, g_attn, w_in_b, tables_m, META_ROWS, 1)

    per_batch = lambda t: t.reshape(b, seq, t.shape[-1])
    diff_out = _diff_attention(per_batch(dq), per_batch(dk), per_batch(dv), dk_m, dv_m,
                               diff_lambda[0], diff_subln_g[0][:, None])
    ret_out = _retention(per_batch(rq), per_batch(rk), per_batch(rv), per_batch(rg), rk_m, rv_m)

    h, xn, pos, col, counts = _out_proj_and_route(
        x2, diff_out.reshape(n_tok, DIFF_WIDTH), ret_out.reshape(n_tok, RET_WIDTH),
        w_out_b, ffn_norm_g[0][None, :], w_router[0].T, b_router[0][:, None])

    n_tiles = n_tok // ROUTE_TILE
    n_blocks = -(-(n_tok * TOP_K + n_tiles * N_EXPERTS * (SEG_ALIGN - 1) + N_EXPERTS * (MOE_TILE - 1))
                 // MOE_TILE)
    seg_rows = (counts[:, :, 0] + SEG_ALIGN - 1) // SEG_ALIGN * SEG_ALIGN
    seg_local = jnp.cumsum(seg_rows, axis=1) - seg_rows
    blocks_per = (jnp.sum(seg_rows, axis=0) + MOE_TILE - 1) // MOE_TILE
    block_end = jnp.cumsum(blocks_per)
    group_start = (block_end - blocks_per) * MOE_TILE
    seg_global = group_start[None, :] + jnp.cumsum(seg_rows, axis=0) - seg_rows
    flat = lambda t: t.reshape(-1).astype(jnp.int32)
    tables = (flat(seg_local), flat(seg_global), flat(seg_rows // BIG_PIECE),
              flat(seg_rows % BIG_PIECE // SEG_ALIGN))

    n_used = block_end[-1:].astype(jnp.int32)
    all_blocks = jnp.arange(n_blocks, dtype=jnp.int32)
    block_ids = jnp.minimum(all_blocks, n_used[0] - 1)
    block_expert = jnp.minimum(jnp.sum(block_end[None, :] <= block_ids[:, None], axis=1),
                               N_EXPERTS - 1).astype(jnp.int32)
    is_group_end = jnp.any((block_end[None, :] == all_blocks[:, None] + 1) & (blocks_per[None, :] > 0), axis=1)
    fill_flags = (is_group_end | (all_blocks >= n_used[0])).astype(jnp.int32)

    xs = _dispatch(tables, fill_flags, xn, pos, n_blocks)
    ys = _expert_ffn(block_expert, n_used, xs, w_up_b, b_up_s, w_down_b, b_down_s)
    out = _combine(tables, col, h, ys, final_norm_g[None, :])
    return out.reshape(b, seq, D_MODEL)
```

```python
import functools
import math

import jax
import jax.numpy as jnp
import numpy as np
from jax import lax
from jax.experimental import pallas as pl
from jax.experimental.pallas import tpu as pltpu

F32 = jnp.float32
BF16 = jnp.bfloat16

D_MODEL = 1024
N_META = 16
ROPE_THETA = 10000.0
NORM_EPS = 1e-5
DIFF_HEADS = 4
HEAD_DIM = 64
VALUE_DIM = 128
DIFF_WIDTH = DIFF_HEADS * VALUE_DIM
RET_HEADS = 4
RET_WIDTH = RET_HEADS * VALUE_DIM
QK_WIDTH = RET_HEADS * HEAD_DIM
IN_PROJ_WIDTH = 3 * DIFF_WIDTH + 2 * QK_WIDTH + 2 * RET_WIDTH
LAMBDA_INIT = 0.8 - 0.6 * math.exp(-0.3 * 0)
N_EXPERTS = 32
TOP_K = 4
D_FF = D_MODEL
SWIGLU_ALPHA = 1.702
SWIGLU_LIMIT = 7.0

LANES = 128
MXU_WIDTH = 256
META_ROWS = 128
VMEM_LIMIT = 56 * 1024 * 1024

ONES_ROWS = 16
LOG2_E = math.log2(math.e)

PROJ_TILE = 512
ATTN_TILE = 512
RET_CHUNK = 256
ROUTE_TILE = 512
MOE_TILE = 512
SEG_ALIGN = 8
BIG_PIECE = 4 * SEG_ALIGN
COMBINE_ROWS = 128
LOC_ROWS = ROUTE_TILE * TOP_K + N_EXPERTS * SEG_ALIGN

_LOG_GAMMA = [float(v) for v in np.log1p(-np.exp2(-5.0 - np.arange(RET_HEADS, dtype=np.float32)))]


def _params(sem):
    return pltpu.CompilerParams(dimension_semantics=sem, vmem_limit_bytes=VMEM_LIMIT)


def _rope(t, cos, sin_signed):
    lane = lax.broadcasted_iota(jnp.int32, t.shape, 1)
    first_half = (lane % HEAD_DIM) < (HEAD_DIM // 2)
    partner = jnp.where(first_half, pltpu.roll(t, LANES - HEAD_DIM // 2, 1),
                        pltpu.roll(t, HEAD_DIM // 2, 1))
    return t * cos + partner * sin_signed


def _inproj_kernel(x_ref, g_ref, w_ref, cd_ref, sd_ref, cr_ref, sr_ref,
                   dq_ref, dk_ref, dv_ref, rq_ref, rk_ref, rv_ref, rg_ref):
    x = x_ref[...]
    ms = jnp.mean(x * x, axis=-1, keepdims=True)
    hn = (x * lax.rsqrt(ms + NORM_EPS) * g_ref[...]).astype(BF16)
    cd, sd, cr, sr = cd_ref[...], sd_ref[...], cr_ref[...], sr_ref[...]
    scale = HEAD_DIM ** -0.5

    def proj(col, width):
        return jnp.dot(hn, w_ref[:, col:col + width], preferred_element_type=F32)

    col = 0
    for out_ref, width, cos, sin, mul in (
            (dq_ref, DIFF_WIDTH, cd, sd, scale * LOG2_E), (dk_ref, DIFF_WIDTH, cd, sd, None),
            (dv_ref, DIFF_WIDTH, None, None, None),
            (rq_ref, QK_WIDTH, cr, sr, None), (rk_ref, QK_WIDTH, cr, sr, scale),
            (rv_ref, RET_WIDTH, None, None, None), (rg_ref, RET_WIDTH, None, None, None)):
        for c in range(0, width, MXU_WIDTH):
            wide = proj(col + c, MXU_WIDTH)
            for half in range(0, MXU_WIDTH, LANES):
                p = wide[:, half:half + LANES]
                if cos is not None:
                    p = _rope(p, cos, sin)
                if mul is not None:
                    p = p * mul
                out_ref[:, c + half:c + half + LANES] = p.astype(out_ref.dtype)
        col += width


def _in_projection(rows, g, w_bf16, tables, tile, seq_tiles):
    n = rows.shape[0]
    row_spec = lambda w: pl.BlockSpec((tile, w), lambda i: (i, 0))
    table_spec = pl.BlockSpec((tile, LANES), lambda i: (i % seq_tiles, 0))
    widths = (DIFF_WIDTH, DIFF_WIDTH, DIFF_WIDTH, QK_WIDTH, QK_WIDTH, RET_WIDTH, RET_WIDTH)
    return pl.pallas_call(
        _inproj_kernel,
        grid=(n // tile,),
        in_specs=[row_spec(D_MODEL),
                  pl.BlockSpec((1, D_MODEL), lambda i: (0, 0)),
                  pl.BlockSpec((D_MODEL, IN_PROJ_WIDTH), lambda i: (0, 0)),
                  table_spec, table_spec, table_spec, table_spec],
        out_specs=[row_spec(w) for w in widths],
        out_shape=[jax.ShapeDtypeStruct((n, w), BF16) for w in widths],
        compiler_params=_params(("parallel",)),
        name="in_projection",
    )(rows, g, w_bf16, *tables)


def _rope_tables(pos, inv_freq):
    ang = pos[:, None] * inv_freq[None, :]
    cos = jnp.tile(jnp.cos(ang), (1, LANES // (HEAD_DIM // 2)))
    sin = jnp.sin(ang)
    sin_signed = jnp.tile(jnp.concatenate([-sin, sin], axis=1), (1, LANES // HEAD_DIM))
    return cos, sin_signed


def _transpose_bf16(t):
    return t.astype(F32).T.astype(BF16)


def _with_ones_rows(vt):
    return jnp.concatenate([vt, jnp.ones((ONES_ROWS, vt.shape[1]), vt.dtype)], axis=0)


def _attn_kernel(q_ref, k_ref, v_ref, km_ref, vm_ref, lp_ref, sg_ref, o_ref,
                 vt_sc, m_sc, acc_sc, sa_sc, sb_sc, *, tile, n_chunks):
    i = pl.program_id(2)

    @pl.when(i == 0)
    def _():
        def transpose_chunk(j, carry):
            start = pl.multiple_of(j * tile, tile)
            vt_sc[j] = _with_ones_rows(_transpose_bf16(v_ref[pl.ds(start, tile), :]))
            return carry
        lax.fori_loop(0, n_chunks, transpose_chunk, 0)

    qt = q_ref[...].astype(F32).T
    sub = lax.broadcasted_iota(jnp.int32, qt.shape, 0)
    qcat = jnp.concatenate([jnp.where(sub < HEAD_DIM, qt, 0.0).astype(BF16),
                            jnp.where(sub >= HEAD_DIM, qt, 0.0).astype(BF16)], axis=1)

    def scores(j):
        start = pl.multiple_of(j * tile, tile)
        return jnp.dot(k_ref[pl.ds(start, tile), :], qcat, preferred_element_type=F32)

    sa_sc[...] = scores(0)
    m_sc[...] = jnp.full_like(m_sc, -jnp.inf)
    acc_sc[...] = jnp.zeros_like(acc_sc)

    meta_row = lax.broadcasted_iota(jnp.int32, (META_ROWS, 2 * tile), 0) >= META_ROWS - N_META
    s_meta = jnp.where(meta_row, jnp.dot(km_ref[...], qcat, preferred_element_type=F32), -jnp.inf)
    vmt = _with_ones_rows(_transpose_bf16(vm_ref[...]))

    def absorb(s_ref, j, last):
        s = s_ref[...]
        m_old = m_sc[...]
        if last:
            key = lax.broadcasted_iota(jnp.int32, s.shape, 0)
            lane = lax.broadcasted_iota(jnp.int32, s.shape, 1)
            s = jnp.where(key <= jnp.where(lane >= tile, lane - tile, lane), s, -jnp.inf)
            m_old = jnp.maximum(m_old, jnp.max(s_meta, axis=0, keepdims=True))
        m_new = jnp.maximum(m_old, jnp.max(s, axis=0, keepdims=True))
        update = jnp.dot(vt_sc[j], jnp.exp2(s - m_new).astype(BF16), preferred_element_type=F32)
        if last:
            update = update + jnp.dot(vmt, jnp.exp2(s_meta - m_new).astype(BF16), preferred_element_type=F32)
        acc_sc[...] = jnp.exp2(m_sc[...] - m_new) * acc_sc[...] + update
        m_sc[...] = m_new

    def pair(t, carry):
        sb_sc[...] = scores(2 * t + 1)
        absorb(sa_sc, 2 * t, False)
        sa_sc[...] = scores(2 * t + 2)
        absorb(sb_sc, 2 * t + 1, False)
        return carry

    lax.fori_loop(0, i // 2, pair, 0)

    @pl.when(i % 2 == 0)
    def _():
        absorb(sa_sc, i, True)

    @pl.when(i % 2 == 1)
    def _():
        sb_sc[...] = scores(i)
        absorb(sa_sc, i - 1, False)
        absorb(sb_sc, i, True)

    lp = lp_ref[...]
    lam = (jnp.exp(jnp.sum(lp[0:1] * lp[1:2], axis=-1, keepdims=True))
           - jnp.exp(jnp.sum(lp[2:3] * lp[3:4], axis=-1, keepdims=True)) + LAMBDA_INIT)
    num, den = acc_sc[0:VALUE_DIM, :], acc_sc[VALUE_DIM:VALUE_DIM + 1, :]
    o = num[:, :tile] / den[:, :tile] - lam * (num[:, tile:] / den[:, tile:])
    o = o * lax.rsqrt(jnp.mean(o * o, axis=0, keepdims=True) + NORM_EPS)
    o_ref[...] = (o * sg_ref[...] * (1.0 - LAMBDA_INIT)).T.astype(o_ref.dtype)


def _diff_attention(dq, dk, dv, km, vm, lam_params, subln_g):
    b, seq, _ = dq.shape
    tile = ATTN_TILE
    n_chunks = seq // tile
    head_rows = pl.BlockSpec((None, tile, LANES), lambda bi, h, i: (bi, i, h))
    head_seq = pl.BlockSpec((None, seq, LANES), lambda bi, h, i: (bi, 0, h))
    head_meta = pl.BlockSpec((META_ROWS, LANES), lambda bi, h, i: (0, h))
    return pl.pallas_call(
        functools.partial(_attn_kernel, tile=tile, n_chunks=n_chunks),
        grid=(b, DIFF_HEADS, n_chunks),
        in_specs=[head_rows, head_seq, head_seq, head_meta, head_meta,
                  pl.BlockSpec((4, HEAD_DIM), lambda bi, h, i: (0, 0)),
                  pl.BlockSpec((VALUE_DIM, 1), lambda bi, h, i: (0, 0))],
        out_specs=head_rows,
        out_shape=jax.ShapeDtypeStruct((b, seq, DIFF_WIDTH), BF16),
        scratch_shapes=[pltpu.VMEM((n_chunks, VALUE_DIM + ONES_ROWS, tile), BF16),
                        pltpu.VMEM((1, 2 * tile), F32),
                        pltpu.VMEM((VALUE_DIM + ONES_ROWS, 2 * tile), F32),
                        pltpu.VMEM((tile, 2 * tile), F32), pltpu.VMEM((tile, 2 * tile), F32)],
        compiler_params=_params(("parallel", "parallel", "arbitrary")),
        name="diff_attention",
    )(dq, dk, dv, km, vm, lam_params, subln_g)


def _decay_col(log_gamma, n, offset_fn):
    pos = lax.broadcasted_iota(jnp.int32, (n, 1), 0).astype(F32)
    return jnp.exp(log_gamma * offset_fn(pos))


def _ret_kernel(q_ref, k_ref, v_ref, g_ref, km_ref, vm_ref, o_ref, state_sc, *, chunk):
    c = pl.program_id(1)

    @pl.when(c == 0)
    def _():
        km = km_ref[...].astype(F32)
        for h in range(RET_HEADS):
            pair = h // 2
            kt = km[:, pair * LANES:(pair + 1) * LANES].T.astype(BF16)
            kd = _decay_col(_LOG_GAMMA[h], META_ROWS, lambda p: (META_ROWS - 1) - p)
            vd = (vm_ref[:, h * VALUE_DIM:(h + 1) * VALUE_DIM].astype(F32) * kd).astype(BF16)
            state_sc[h] = jnp.dot(kt, vd, preferred_element_type=F32)

    row = lax.broadcasted_iota(jnp.int32, (chunk, chunk), 0)
    col = lax.broadcasted_iota(jnp.int32, (chunk, chunk), 1)
    rel = (row - col).astype(F32)
    lane = lax.broadcasted_iota(jnp.int32, (chunk, LANES), 1)
    for pair in range(RET_HEADS // 2):
        q_pair = q_ref[:, pair * LANES:(pair + 1) * LANES]
        k_pair = k_ref[:, pair * LANES:(pair + 1) * LANES]
        kt = k_pair.astype(F32).T.astype(BF16)
        for h in (2 * pair, 2 * pair + 1):
            lg = _LOG_GAMMA[h]
            own = (lane < HEAD_DIM) if h % 2 == 0 else (lane >= HEAD_DIM)
            qm = jnp.where(own, q_pair, jnp.zeros_like(q_pair))
            v = v_ref[:, h * VALUE_DIM:(h + 1) * VALUE_DIM]
            s = lax.dot_general(qm, k_pair, (((1,), (1,)), ((), ())), preferred_element_type=F32)
            decay = jnp.where(rel >= 0, jnp.exp(lg * jnp.maximum(rel, 0.0)), 0.0)
            y = jnp.dot((s * decay).astype(BF16), v, preferred_element_type=F32)
            qd = _decay_col(lg, chunk, lambda p: p + 1.0)
            state = state_sc[h]
            y = y + jnp.dot((qm.astype(F32) * qd).astype(BF16), state.astype(BF16),
                            preferred_element_type=F32)
            kd = _decay_col(lg, chunk, lambda p: (chunk - 1) - p)
            vd = (v.astype(F32) * kd).astype(BF16)
            state_sc[h] = math.exp(lg * chunk) * state + jnp.dot(kt, vd, preferred_element_type=F32)
            y = y * lax.rsqrt(jnp.mean(y * y, axis=-1, keepdims=True) + NORM_EPS)
            gate = g_ref[:, h * VALUE_DIM:(h + 1) * VALUE_DIM].astype(F32)
            o_ref[:, h * VALUE_DIM:(h + 1) * VALUE_DIM] = (y * gate * jax.nn.sigmoid(gate)).astype(o_ref.dtype)


def _retention(rq, rk, rv, rg, km, vm):
    b, seq, _ = rq.shape
    chunk = RET_CHUNK
    rows = lambda w: pl.BlockSpec((None, chunk, w), lambda bi, c: (bi, c, 0))
    return pl.pallas_call(
        functools.partial(_ret_kernel, chunk=chunk),
        grid=(b, seq // chunk),
        in_specs=[rows(QK_WIDTH), rows(QK_WIDTH), rows(RET_WIDTH), rows(RET_WIDTH),
                  pl.BlockSpec((META_ROWS, QK_WIDTH), lambda bi, c: (0, 0)),
                  pl.BlockSpec((META_ROWS, RET_WIDTH), lambda bi, c: (0, 0))],
        out_specs=rows(RET_WIDTH),
        out_shape=jax.ShapeDtypeStruct((b, seq, RET_WIDTH), BF16),
        scratch_shapes=[pltpu.VMEM((RET_HEADS, LANES, VALUE_DIM), F32)],
        compiler_params=_params(("parallel", "arbitrary")),
        name="retention",
    )(rq, rk, rv, rg, km, vm)


def _route_kernel(x_ref, d_ref, r_ref, wo_ref, g_ref, wrt_ref, brt_ref,
                  h_ref, xn_ref, pos_ref, col_ref, cnt_ref, *, tile):
    h = (x_ref[...]
         + jnp.dot(d_ref[...], wo_ref[0:DIFF_WIDTH, :], preferred_element_type=F32)
         + jnp.dot(r_ref[...], wo_ref[DIFF_WIDTH:, :], preferred_element_type=F32))
    h_ref[...] = h
    xn = h * lax.rsqrt(jnp.mean(h * h, axis=-1, keepdims=True) + NORM_EPS) * g_ref[...]
    xn_ref[...] = xn.astype(BF16)
    logits = lax.dot_general(wrt_ref[...], xn, (((1,), (1,)), ((), ())), preferred_element_type=F32,
                             precision=lax.Precision.HIGHEST) + brt_ref[...]

    expert = lax.broadcasted_iota(jnp.int32, logits.shape, 0)
    work = logits
    vals, hots = [], []
    for k in range(TOP_K):
        v = jnp.max(work, axis=0, keepdims=True)
        idx = jnp.min(jnp.where(work == v, expert, N_EXPERTS), axis=0, keepdims=True)
        hot = expert == idx
        work = jnp.where(hot, -jnp.inf, work)
        vals.append(v)
        hots.append(hot.astype(F32))
    exps = [jnp.exp(v - vals[0]) for v in vals]
    denom = exps[0] + exps[1] + exps[2] + exps[3]
    gates = [e / denom for e in exps]

    chosen = hots[0] + hots[1] + hots[2] + hots[3]
    c = lax.broadcasted_iota(jnp.int32, (tile, tile), 0)
    r = lax.broadcasted_iota(jnp.int32, (tile, tile), 1)
    earlier = (c < r).astype(BF16)
    before = jnp.dot(chosen.astype(BF16), earlier, preferred_element_type=F32)
    cnt = jnp.sum(chosen, axis=1, keepdims=True)
    cnt_pad = jnp.ceil(cnt / SEG_ALIGN) * SEG_ALIGN
    er = lax.broadcasted_iota(jnp.int32, (N_EXPERTS, N_EXPERTS), 0)
    ec = lax.broadcasted_iota(jnp.int32, (N_EXPERTS, N_EXPERTS), 1)
    seg_start = jnp.dot((ec < er).astype(F32), jnp.broadcast_to(cnt_pad, (N_EXPERTS, LANES)),
                        preferred_element_type=F32, precision=lax.Precision.HIGHEST)[:, 0:1]
    base = seg_start + before
    pos = [jnp.sum(hot * base, axis=0, keepdims=True) for hot in hots]
    for k in range(TOP_K):
        pos_ref[k:k + 1, :] = pos[k].astype(jnp.int32)
    cnt_ref[...] = jnp.broadcast_to(cnt, (N_EXPERTS, LANES)).astype(jnp.int32)
    stacked = jnp.concatenate(pos + gates + [jnp.zeros((LANES - 2 * TOP_K, tile), F32)], axis=0)
    col_ref[...] = stacked.T


def _out_proj_and_route(x, diff, ret, wo, g, wrt, brt):
    n = x.shape[0]
    tile = ROUTE_TILE
    rows = lambda w: pl.BlockSpec((tile, w), lambda i: (i, 0))
    full = lambda a, b: pl.BlockSpec((a, b), lambda i: (0, 0))
    return pl.pallas_call(
        functools.partial(_route_kernel, tile=tile),
        grid=(n // tile,),
        in_specs=[rows(D_MODEL), rows(DIFF_WIDTH), rows(RET_WIDTH), full(D_MODEL, D_MODEL),
                  full(1, D_MODEL), full(N_EXPERTS, D_MODEL), full(N_EXPERTS, 1)],
        out_specs=[rows(D_MODEL), rows(D_MODEL),
                   pl.BlockSpec((TOP_K, tile), lambda i: (0, i)), rows(LANES),
                   pl.BlockSpec((None, N_EXPERTS, LANES), lambda i: (i, 0, 0))],
        out_shape=[jax.ShapeDtypeStruct((n, D_MODEL), F32), jax.ShapeDtypeStruct((n, D_MODEL), BF16),
                   jax.ShapeDtypeStruct((TOP_K, n), jnp.int32), jax.ShapeDtypeStruct((n, LANES), F32),
                   jax.ShapeDtypeStruct((n // tile, N_EXPERTS, LANES), jnp.int32)],
        compiler_params=_params(("parallel",)),
        name="out_proj_route",
    )(x, diff, ret, wo, g, wrt, brt)


def _split_up_kernel(w_ref, o_ref):
    group = 2 * LANES
    src = lax.broadcasted_iota(jnp.int32, (group, group), 0)
    dst = lax.broadcasted_iota(jnp.int32, (group, group), 1)
    select = (src == jnp.where(dst < LANES, 2 * dst, 2 * (dst - LANES) + 1)).astype(BF16)
    for c in range(2 * D_FF // group):
        part = jnp.dot(w_ref[:, c * group:(c + 1) * group].astype(BF16), select,
                       preferred_element_type=F32).astype(BF16)
        o_ref[:, c * LANES:(c + 1) * LANES] = part[:, :LANES]
        o_ref[:, D_FF + c * LANES:D_FF + (c + 1) * LANES] = part[:, LANES:]


def _split_up_weights(w_up):
    n_exp, d, f2 = w_up.shape
    tile = 256
    spec = pl.BlockSpec((None, tile, f2), lambda e, i: (e, i, 0))
    return pl.pallas_call(
        _split_up_kernel,
        grid=(n_exp, d // tile),
        in_specs=[spec],
        out_specs=spec,
        out_shape=jax.ShapeDtypeStruct((n_exp, d, f2), BF16),
        compiler_params=_params(("parallel", "parallel")),
        name="split_up_weights",
    )(w_up)


def _segment_copy(local_buf, local_row, global_hbm, global_row, rows, sem, to_global):
    local_piece = local_buf.at[pl.ds(local_row, rows), :]
    global_piece = global_hbm.at[pl.ds(global_row, rows), :]
    if to_global:
        return pltpu.make_async_copy(local_piece, global_piece, sem)
    return pltpu.make_async_copy(global_piece, local_piece, sem)


def _start_segments(tables, tile_index, local_buf, global_hbm, sem, to_global):
    local_ref, global_ref, n_big_ref, n_small_ref, _ = tables
    first = tile_index * N_EXPERTS

    def pieces(local_row, global_row, count, rows):
        def per_piece(p, inner):
            lo = pl.multiple_of(local_row + p * rows, SEG_ALIGN)
            gl = pl.multiple_of(global_row + p * rows, SEG_ALIGN)
            _segment_copy(local_buf, lo, global_hbm, gl, rows, sem, to_global).start()
            return inner

        lax.fori_loop(0, count, per_piece, 0)

    def per_expert(e, carry):
        local_row, global_row = local_ref[first + e], global_ref[first + e]
        big_rows = n_big_ref[first + e] * BIG_PIECE
        pieces(local_row, global_row, n_big_ref[first + e], BIG_PIECE)
        pieces(local_row + big_rows, global_row + big_rows, n_small_ref[first + e], SEG_ALIGN)
        return carry

    lax.fori_loop(0, N_EXPERTS, per_expert, 0)


def _wait_segments(tables, tile_index, local_buf, global_hbm, sem, to_global):
    n_pad_ref = tables[4]
    _segment_copy(local_buf, 0, global_hbm, 0, ROUTE_TILE * TOP_K, sem, to_global).wait()

    def per_pad(p, carry):
        _segment_copy(local_buf, 0, global_hbm, 0, SEG_ALIGN, sem, to_global).wait()
        return carry

    lax.fori_loop(0, n_pad_ref[tile_index], per_pad, 0)


def _dispatch_kernel(local_ref, global_ref, n_big_ref, n_small_ref, n_pad_ref, fill_ref, xn_ref, pos_ref,
                     xs_out, loc_buf, zero_buf, seg_sem, fill_sem, *, n_blocks, n_tiles):
    i = pl.program_id(0)
    slot = i % 2

    @pl.when(i == 0)
    def _():
        zero_buf[...] = jnp.zeros_like(zero_buf)

        def fill_copy(j):
            row = pl.multiple_of(j * MOE_TILE, MOE_TILE)
            return pltpu.make_async_copy(zero_buf, xs_out.at[pl.ds(row, MOE_TILE), :], fill_sem)

        def start(j, carry):
            @pl.when(fill_ref[j] > 0)
            def _():
                fill_copy(j).start()
            return carry

        def wait(j, carry):
            @pl.when(fill_ref[j] > 0)
            def _():
                fill_copy(j).wait()
            return carry

        lax.fori_loop(0, n_blocks, start, 0)
        lax.fori_loop(0, n_blocks, wait, 0)

    pos = pos_ref[...]
    row = lax.broadcasted_iota(jnp.int32, (LOC_ROWS, pos.shape[1]), 0)
    select = jnp.zeros(row.shape, F32)
    for k in range(TOP_K):
        select = jnp.where(row == pos[k:k + 1], 1.0, select)
    loc_buf[slot] = jnp.dot(select.astype(BF16), xn_ref[...], preferred_element_type=F32)

    tables = (local_ref, global_ref, n_big_ref, n_small_ref, n_pad_ref)
    _start_segments(tables, i, loc_buf.at[slot], xs_out, seg_sem.at[slot], True)

    @pl.when(i > 0)
    def _():
        _wait_segments(tables, i - 1, loc_buf.at[1 - slot], xs_out, seg_sem.at[1 - slot], True)

    @pl.when(i == n_tiles - 1)
    def _():
        _wait_segments(tables, i, loc_buf.at[slot], xs_out, seg_sem.at[slot], True)


def _dispatch(tables, fill_flags, xn, pos, n_blocks):
    n = xn.shape[0]
    tile = ROUTE_TILE
    return pl.pallas_call(
        functools.partial(_dispatch_kernel, n_blocks=n_blocks, n_tiles=n // tile),
        grid_spec=pltpu.PrefetchScalarGridSpec(
            num_scalar_prefetch=6,
            grid=(n // tile,),
            in_specs=[pl.BlockSpec((tile, D_MODEL), lambda i, *_: (i, 0)),
                      pl.BlockSpec((TOP_K, tile), lambda i, *_: (0, i))],
            out_specs=pl.BlockSpec(memory_space=pl.ANY),
            scratch_shapes=[pltpu.VMEM((2, LOC_ROWS, D_MODEL), F32),
                            pltpu.VMEM((MOE_TILE, D_MODEL), F32),
                            pltpu.SemaphoreType.DMA((2,)), pltpu.SemaphoreType.DMA]),
        out_shape=jax.ShapeDtypeStruct((n_blocks * MOE_TILE, D_MODEL), F32),
        compiler_params=_params(("arbitrary",)),
        name="moe_dispatch",
    )(*tables, fill_flags, xn, pos)


def _expert_kernel(be_ref, nu_ref, xs_ref, wu_ref, bu_ref, wd_ref, bd_ref, ys_ref):
    del be_ref

    @pl.when(pl.program_id(0) < nu_ref[0])
    def _():
        x = xs_ref[...].astype(BF16)
        glu = jnp.dot(x, wu_ref[:, 0:D_FF], preferred_element_type=F32) + bu_ref[:, 0:D_FF]
        lin = jnp.dot(x, wu_ref[:, D_FF:], preferred_element_type=F32) + bu_ref[:, D_FF:]
        glu = jnp.minimum(glu, SWIGLU_LIMIT)
        lin = jnp.clip(lin, -SWIGLU_LIMIT, SWIGLU_LIMIT)
        act = glu * jax.nn.sigmoid(SWIGLU_ALPHA * glu) * (lin + 1.0)
        ys_ref[...] = (jnp.dot(act.astype(BF16), wd_ref[...].astype(BF16), preferred_element_type=F32)
                       + bd_ref[...])

    @pl.when(pl.program_id(0) >= nu_ref[0])
    def _():
        ys_ref[...] = jnp.zeros_like(ys_ref)


def _expert_ffn(block_expert, n_used, xs, wu, bu, wd, bd):
    n_rows = xs.shape[0]
    tile = MOE_TILE
    rows = pl.BlockSpec((tile, D_MODEL), lambda j, be, nu: (jnp.minimum(j, nu[0] - 1), 0))
    per_expert = lambda a, b: pl.BlockSpec((None, a, b), lambda j, be, nu: (be[j], 0, 0))
    return pl.pallas_call(
        _expert_kernel,
        grid_spec=pltpu.PrefetchScalarGridSpec(
            num_scalar_prefetch=2,
            grid=(n_rows // tile,),
            in_specs=[rows, per_expert(D_MODEL, 2 * D_FF), per_expert(1, 2 * D_FF),
                      per_expert(D_FF, D_MODEL), per_expert(1, D_MODEL)],
            out_specs=pl.BlockSpec((tile, D_MODEL), lambda j, be, nu: (j, 0))),
        out_shape=jax.ShapeDtypeStruct((n_rows, D_MODEL), F32),
        compiler_params=_params(("arbitrary",)),
        name="expert_ffn",
    )(block_expert, n_used, xs, wu, bu, wd, bd)


def _combine_kernel(local_ref, global_ref, n_big_ref, n_small_ref, n_pad_ref, col_ref, h_ref, ys_hbm, g_ref,
                    o_ref, loc_buf, seg_sem, *, n_tiles):
    i = pl.program_id(0)
    slot = i % 2
    tables = (local_ref, global_ref, n_big_ref, n_small_ref, n_pad_ref)

    def fetch(tile_index, buf_slot):
        _start_segments(tables, tile_index, loc_buf.at[buf_slot], ys_hbm, seg_sem.at[buf_slot], False)

    @pl.when(i == 0)
    def _():
        loc_buf[...] = jnp.zeros_like(loc_buf)
        fetch(0, 0)

    @pl.when(i + 1 < n_tiles)
    def _():
        fetch(i + 1, 1 - slot)

    _wait_segments(tables, i, loc_buf.at[slot], ys_hbm, seg_sem.at[slot], False)

    rows = loc_buf[slot].astype(BF16)
    lane = lax.broadcasted_iota(jnp.int32, (COMBINE_ROWS, LOC_ROWS), 1)
    for r in range(0, col_ref.shape[0], COMBINE_ROWS):
        col = col_ref[r:r + COMBINE_ROWS, :]
        weights = jnp.zeros((COMBINE_ROWS, LOC_ROWS), F32)
        for k in range(TOP_K):
            weights = jnp.where(lane == col[:, k:k + 1].astype(jnp.int32),
                                col[:, TOP_K + k:TOP_K + k + 1], weights)
        h = h_ref[r:r + COMBINE_ROWS, :] + jnp.dot(weights.astype(BF16), rows, preferred_element_type=F32)
        o_ref[r:r + COMBINE_ROWS, :] = (h * lax.rsqrt(jnp.mean(h * h, axis=-1, keepdims=True) + NORM_EPS)
                                        * g_ref[...])


def _combine(tables, col, h, ys, g):
    n = h.shape[0]
    tile = ROUTE_TILE
    return pl.pallas_call(
        functools.partial(_combine_kernel, n_tiles=n // tile),
        grid_spec=pltpu.PrefetchScalarGridSpec(
            num_scalar_prefetch=5,
            grid=(n // tile,),
            in_specs=[pl.BlockSpec((tile, LANES), lambda i, *_: (i, 0)),
                      pl.BlockSpec((tile, D_MODEL), lambda i, *_: (i, 0)),
                      pl.BlockSpec(memory_space=pl.ANY),
                      pl.BlockSpec((1, D_MODEL), lambda i, *_: (0, 0))],
            out_specs=pl.BlockSpec((tile, D_MODEL), lambda i, *_: (i, 0)),
            scratch_shapes=[pltpu.VMEM((2, LOC_ROWS, D_MODEL), F32), pltpu.SemaphoreType.DMA((2,))]),
        out_shape=jax.ShapeDtypeStruct((n, D_MODEL), F32),
        compiler_params=_params(("arbitrary",)),
        name="moe_combine",
    )(*tables, col, h, ys, g)


def kernel(x, meta_tokens, attn_norm_g, w_in, diff_lambda, diff_subln_g, w_out, ffn_norm_g,
           w_router, b_router, w_up, b_up, w_down, b_down, final_norm_g):
    b, seq, d = x.shape
    assert d == D_MODEL and seq % PROJ_TILE == 0 and seq % RET_CHUNK == 0 and seq % ATTN_TILE == 0
    assert w_in.shape[0] == 1, "one layer"
    n_tok = b * seq
    assert n_tok % ROUTE_TILE == 0

    w_in_b = w_in[0].astype(BF16)
    w_out_b = w_out[0].astype(BF16)
    w_up_b = _split_up_weights(w_up[0])
    b_up_s = jnp.concatenate([b_up[0][..., 0::2], b_up[0][..., 1::2]], axis=-1)[:, None, :]
    w_down_b = w_down[0]
    b_down_s = b_down[0][:, None, :]

    diff_inv_freq = ROPE_THETA ** (-jnp.arange(0, HEAD_DIM, 2, dtype=F32) / HEAD_DIM)
    ret_inv_freq = ROPE_THETA ** (-jnp.linspace(0.0, 1.0, HEAD_DIM // 2, dtype=F32))
    pos_x = jnp.arange(seq, dtype=F32) + N_META
    pos_m = jnp.arange(META_ROWS, dtype=F32) - (META_ROWS - N_META)
    tables_x = _rope_tables(pos_x, diff_inv_freq) + _rope_tables(pos_x, ret_inv_freq)
    tables_m = _rope_tables(pos_m, diff_inv_freq) + _rope_tables(pos_m, ret_inv_freq)

    g_attn = attn_norm_g[0][None, :]
    x2 = x.reshape(n_tok, D_MODEL)
    meta_rows = jnp.concatenate(
        [jnp.zeros((META_ROWS - N_META, D_MODEL), x.dtype), meta_tokens.astype(x.dtype)], axis=0)
    dq, dk, dv, rq, rk, rv, rg = _in_projection(x2, g_attn, w_in_b, tables_x, PROJ_TILE, seq // PROJ_TILE)
    _, dk_m, dv_m, _, rk_m, rv_m, _ = _in_projection(meta_rows, g_attn, w_in_b, tables_m, META_ROWS, 1)

    per_batch = lambda t: t.reshape(b, seq, t.shape[-1])
    diff_out = _diff_attention(per_batch(dq), per_batch(dk), per_batch(dv), dk_m, dv_m,
                               diff_lambda[0], diff_subln_g[0][:, None])
    ret_out = _retention(per_batch(rq), per_batch(rk), per_batch(rv), per_batch(rg), rk_m, rv_m)

    h, xn, pos, col, counts = _out_proj_and_route(
        x2, diff_out.reshape(n_tok, DIFF_WIDTH), ret_out.reshape(n_tok, RET_WIDTH),
        w_out_b, ffn_norm_g[0][None, :], w_router[0].T, b_router[0][:, None])

    n_tiles = n_tok // ROUTE_TILE
    n_blocks = -(-(n_tok * TOP_K + n_tiles * N_EXPERTS * (SEG_ALIGN - 1) + N_EXPERTS * (MOE_TILE - 1))
                 // MOE_TILE)
    seg_rows = (counts[:, :, 0] + SEG_ALIGN - 1) // SEG_ALIGN * SEG_ALIGN
    seg_local = jnp.cumsum(seg_rows, axis=1) - seg_rows
    blocks_per = (jnp.sum(seg_rows, axis=0) + MOE_TILE - 1) // MOE_TILE
    block_end = jnp.cumsum(blocks_per)
    group_start = (block_end - blocks_per) * MOE_TILE
    seg_global = group_start[None, :] + jnp.cumsum(seg_rows, axis=0) - seg_rows
    flat = lambda t: t.reshape(-1).astype(jnp.int32)
    pad_pieces = (jnp.sum(seg_rows, axis=1) - ROUTE_TILE * TOP_K) // SEG_ALIGN
    tables = (flat(seg_local), flat(seg_global), flat(seg_rows // BIG_PIECE),
              flat(seg_rows % BIG_PIECE // SEG_ALIGN), flat(pad_pieces))

    n_used = block_end[-1:].astype(jnp.int32)
    all_blocks = jnp.arange(n_blocks, dtype=jnp.int32)
    block_ids = jnp.minimum(all_blocks, n_used[0] - 1)
    block_expert = jnp.minimum(jnp.sum(block_end[None, :] <= block_ids[:, None], axis=1),
                               N_EXPERTS - 1).astype(jnp.int32)
    is_group_end = jnp.any((block_end[None, :] == all_blocks[:, None] + 1) & (blocks_per[None, :] > 0), axis=1)
    fill_flags = (is_group_end | (all_blocks >= n_used[0])).astype(jnp.int32)

    xs = _dispatch(tables, fill_flags, xn, pos, n_blocks)
    ys = _expert_ffn(block_expert, n_used, xs, w_up_b, b_up_s, w_down_b, b_down_s)
    out = _combine(tables, col, h, ys, final_norm_g[None, :])
    return out.reshape(b, seq, D_MODEL)
```

```python
import functools
import math

import jax
import jax.numpy as jnp
import numpy as np
from jax import lax
from jax.experimental import pallas as pl
from jax.experimental.pallas import tpu as pltpu

F32 = jnp.float32
BF16 = jnp.bfloat16

D_MODEL = 1024
N_META = 16
ROPE_THETA = 10000.0
NORM_EPS = 1e-5
DIFF_HEADS = 4
HEAD_DIM = 64
VALUE_DIM = 128
DIFF_WIDTH = DIFF_HEADS * VALUE_DIM
RET_HEADS = 4
RET_WIDTH = RET_HEADS * VALUE_DIM
QK_WIDTH = RET_HEADS * HEAD_DIM
IN_PROJ_WIDTH = 3 * DIFF_WIDTH + 2 * QK_WIDTH + 2 * RET_WIDTH
LAMBDA_INIT = 0.8 - 0.6 * math.exp(-0.3 * 0)
N_EXPERTS = 32
TOP_K = 4
D_FF = D_MODEL
SWIGLU_ALPHA = 1.702
SWIGLU_LIMIT = 7.0

LANES = 128
MXU_WIDTH = 256
META_ROWS = 128
VMEM_LIMIT = 56 * 1024 * 1024

ONES_ROWS = 16
LOG2_E = math.log2(math.e)

PROJ_TILE = 512
ATTN_TILE = 512
HEADS_PER_STEP = 2
RET_CHUNK = 256
ROUTE_TILE = 512
MOE_TILE = 512
SEG_ALIGN = 8
BIG_PIECE = 4 * SEG_ALIGN
COMBINE_ROWS = 128
LOC_ROWS = ROUTE_TILE * TOP_K + N_EXPERTS * SEG_ALIGN

_LOG_GAMMA = [float(v) for v in np.log1p(-np.exp2(-5.0 - np.arange(RET_HEADS, dtype=np.float32)))]


def _params(sem):
    return pltpu.CompilerParams(dimension_semantics=sem, vmem_limit_bytes=VMEM_LIMIT)


def _rope(t, cos, sin_signed):
    lane = lax.broadcasted_iota(jnp.int32, t.shape, 1)
    first_half = (lane % HEAD_DIM) < (HEAD_DIM // 2)
    partner = jnp.where(first_half, pltpu.roll(t, LANES - HEAD_DIM // 2, 1),
                        pltpu.roll(t, HEAD_DIM // 2, 1))
    return t * cos + partner * sin_signed


def _inproj_kernel(x_ref, g_ref, w_ref, cd_ref, sd_ref, cr_ref, sr_ref,
                   dq_ref, dk_ref, dv_ref, rq_ref, rk_ref, rv_ref, rg_ref):
    x = x_ref[...]
    ms = jnp.mean(x * x, axis=-1, keepdims=True)
    hn = (x * lax.rsqrt(ms + NORM_EPS) * g_ref[...]).astype(BF16)
    cd, sd, cr, sr = cd_ref[...], sd_ref[...], cr_ref[...], sr_ref[...]
    scale = HEAD_DIM ** -0.5

    def proj(col, width):
        return jnp.dot(hn, w_ref[:, col:col + width], preferred_element_type=F32)

    col = 0
    for out_ref, width, cos, sin, mul in (
            (dq_ref, DIFF_WIDTH, cd, sd, scale * LOG2_E), (dk_ref, DIFF_WIDTH, cd, sd, None),
            (dv_ref, DIFF_WIDTH, None, None, None),
            (rq_ref, QK_WIDTH, cr, sr, None), (rk_ref, QK_WIDTH, cr, sr, scale),
            (rv_ref, RET_WIDTH, None, None, None), (rg_ref, RET_WIDTH, None, None, None)):
        for c in range(0, width, MXU_WIDTH):
            wide = proj(col + c, MXU_WIDTH)
            for half in range(0, MXU_WIDTH, LANES):
                p = wide[:, half:half + LANES]
                if cos is not None:
                    p = _rope(p, cos, sin)
                if mul is not None:
                    p = p * mul
                out_ref[:, c + half:c + half + LANES] = p.astype(out_ref.dtype)
        col += width


def _in_projection(rows, g, w_bf16, tables, tile, seq_tiles):
    n = rows.shape[0]
    row_spec = lambda w: pl.BlockSpec((tile, w), lambda i: (i, 0))
    table_spec = pl.BlockSpec((tile, LANES), lambda i: (i % seq_tiles, 0))
    widths = (DIFF_WIDTH, DIFF_WIDTH, DIFF_WIDTH, QK_WIDTH, QK_WIDTH, RET_WIDTH, RET_WIDTH)
    return pl.pallas_call(
        _inproj_kernel,
        grid=(n // tile,),
        in_specs=[row_spec(D_MODEL),
                  pl.BlockSpec((1, D_MODEL), lambda i: (0, 0)),
                  pl.BlockSpec((D_MODEL, IN_PROJ_WIDTH), lambda i: (0, 0)),
                  table_spec, table_spec, table_spec, table_spec],
        out_specs=[row_spec(w) for w in widths],
        out_shape=[jax.ShapeDtypeStruct((n, w), BF16) for w in widths],
        compiler_params=_params(("parallel",)),
        name="in_projection",
    )(rows, g, w_bf16, *tables)


def _rope_tables(pos, inv_freq):
    ang = pos[:, None] * inv_freq[None, :]
    cos = jnp.tile(jnp.cos(ang), (1, LANES // (HEAD_DIM // 2)))
    sin = jnp.sin(ang)
    sin_signed = jnp.tile(jnp.concatenate([-sin, sin], axis=1), (1, LANES // HEAD_DIM))
    return cos, sin_signed


def _transpose_bf16(t):
    return t.astype(F32).T.astype(BF16)


def _with_ones_rows(vt):
    return jnp.concatenate([vt, jnp.ones((ONES_ROWS, vt.shape[1]), vt.dtype)], axis=0)


def _attn_kernel(q_ref, k_ref, v_ref, km_ref, vm_ref, lp_ref, sg_ref, o_ref,
                 vt_sc, m_sc, acc_sc, sa_sc, sb_sc, *, tile, n_chunks):
    i = pl.program_id(2)
    heads = range(HEADS_PER_STEP)
    lanes = lambda hh: slice(hh * LANES, (hh + 1) * LANES)

    @pl.when(i == 0)
    def _():
        def transpose_chunk(j, carry):
            start = pl.multiple_of(j * tile, tile)
            for hh in heads:
                vt_sc[hh, j] = _with_ones_rows(_transpose_bf16(v_ref[pl.ds(start, tile), lanes(hh)]))
            return carry
        lax.fori_loop(0, n_chunks, transpose_chunk, 0)

    qcat = []
    for hh in heads:
        qt = q_ref[:, lanes(hh)].astype(F32).T
        sub = lax.broadcasted_iota(jnp.int32, qt.shape, 0)
        qcat.append(jnp.concatenate([jnp.where(sub < HEAD_DIM, qt, 0.0).astype(BF16),
                                     jnp.where(sub >= HEAD_DIM, qt, 0.0).astype(BF16)], axis=1))

    def scores(hh, j):
        start = pl.multiple_of(j * tile, tile)
        return jnp.dot(k_ref[pl.ds(start, tile), lanes(hh)], qcat[hh], preferred_element_type=F32)

    for hh in heads:
        sa_sc[hh] = scores(hh, 0)
        m_sc[hh] = jnp.full(m_sc.shape[1:], -jnp.inf, F32)
        acc_sc[hh] = jnp.zeros(acc_sc.shape[1:], F32)

    meta_row = lax.broadcasted_iota(jnp.int32, (META_ROWS, 2 * tile), 0) >= META_ROWS - N_META
    s_meta = [jnp.where(meta_row, jnp.dot(km_ref[:, lanes(hh)], qcat[hh], preferred_element_type=F32), -jnp.inf)
              for hh in heads]
    vmt = [_with_ones_rows(_transpose_bf16(vm_ref[:, lanes(hh)])) for hh in heads]

    def absorb(hh, s_ref, j, last):
        s = s_ref[hh]
        m_old = m_sc[hh]
        if last:
            key = lax.broadcasted_iota(jnp.int32, s.shape, 0)
            lane = lax.broadcasted_iota(jnp.int32, s.shape, 1)
            s = jnp.where(key <= jnp.where(lane >= tile, lane - tile, lane), s, -jnp.inf)
            m_old = jnp.maximum(m_old, jnp.max(s_meta[hh], axis=0, keepdims=True))
        m_new = jnp.maximum(m_old, jnp.max(s, axis=0, keepdims=True))
        update = jnp.dot(vt_sc[hh, j], jnp.exp2(s - m_new).astype(BF16), preferred_element_type=F32)
        if last:
            update = update + jnp.dot(vmt[hh], jnp.exp2(s_meta[hh] - m_new).astype(BF16),
                                      preferred_element_type=F32)
        acc_sc[hh] = jnp.exp2(m_sc[hh] - m_new) * acc_sc[hh] + update
        m_sc[hh] = m_new

    def pair(t, carry):
        for hh in heads:
            sb_sc[hh] = scores(hh, 2 * t + 1)
            absorb(hh, sa_sc, 2 * t, False)
        for hh in heads:
            sa_sc[hh] = scores(hh, 2 * t + 2)
            absorb(hh, sb_sc, 2 * t + 1, False)
        return carry

    lax.fori_loop(0, i // 2, pair, 0)

    @pl.when(i % 2 == 0)
    def _():
        for hh in heads:
            absorb(hh, sa_sc, i, True)

    @pl.when(i % 2 == 1)
    def _():
        for hh in heads:
            sb_sc[hh] = scores(hh, i)
            absorb(hh, sa_sc, i - 1, False)
        for hh in heads:
            absorb(hh, sb_sc, i, True)

    lp = lp_ref[...]
    lam = (jnp.exp(jnp.sum(lp[0:1] * lp[1:2], axis=-1, keepdims=True))
           - jnp.exp(jnp.sum(lp[2:3] * lp[3:4], axis=-1, keepdims=True)) + LAMBDA_INIT)
    for hh in heads:
        num, den = acc_sc[hh, 0:VALUE_DIM, :], acc_sc[hh, VALUE_DIM:VALUE_DIM + 1, :]
        o = num[:, :tile] / den[:, :tile] - lam * (num[:, tile:] / den[:, tile:])
        o = o * lax.rsqrt(jnp.mean(o * o, axis=0, keepdims=True) + NORM_EPS)
        o_ref[:, lanes(hh)] = (o * sg_ref[...] * (1.0 - LAMBDA_INIT)).T.astype(o_ref.dtype)


def _diff_attention(dq, dk, dv, km, vm, lam_params, subln_g):
    b, seq, _ = dq.shape
    tile = ATTN_TILE
    n_chunks = seq // tile
    width = HEADS_PER_STEP * LANES
    head_rows = pl.BlockSpec((None, tile, width), lambda bi, h, i: (bi, i, h))
    head_seq = pl.BlockSpec((None, seq, width), lambda bi, h, i: (bi, 0, h))
    head_meta = pl.BlockSpec((META_ROWS, width), lambda bi, h, i: (0, h))
    per_head = lambda *shape: pltpu.VMEM((HEADS_PER_STEP,) + shape, F32)
    return pl.pallas_call(
        functools.partial(_attn_kernel, tile=tile, n_chunks=n_chunks),
        grid=(b, DIFF_HEADS // HEADS_PER_STEP, n_chunks),
        in_specs=[head_rows, head_seq, head_seq, head_meta, head_meta,
                  pl.BlockSpec((4, HEAD_DIM), lambda bi, h, i: (0, 0)),
                  pl.BlockSpec((VALUE_DIM, 1), lambda bi, h, i: (0, 0))],
        out_specs=head_rows,
        out_shape=jax.ShapeDtypeStruct((b, seq, DIFF_WIDTH), BF16),
        scratch_shapes=[pltpu.VMEM((HEADS_PER_STEP, n_chunks, VALUE_DIM + ONES_ROWS, tile), BF16),
                        per_head(1, 2 * tile), per_head(VALUE_DIM + ONES_ROWS, 2 * tile),
                        per_head(tile, 2 * tile), per_head(tile, 2 * tile)],
        compiler_params=_params(("parallel", "parallel", "arbitrary")),
        name="diff_attention",
    )(dq, dk, dv, km, vm, lam_params, subln_g)


def _decay_col(log_gamma, n, offset_fn):
    pos = lax.broadcasted_iota(jnp.int32, (n, 1), 0).astype(F32)
    return jnp.exp(log_gamma * offset_fn(pos))


def _ret_kernel(q_ref, k_ref, v_ref, g_ref, km_ref, vm_ref, o_ref, state_sc, decay_sc, *, chunk):
    c = pl.program_id(1)

    @pl.when(c == 0)
    def _():
        km = km_ref[...].astype(F32)
        for h in range(RET_HEADS):
            pair = h // 2
            kt = km[:, pair * LANES:(pair + 1) * LANES].T.astype(BF16)
            kd = _decay_col(_LOG_GAMMA[h], META_ROWS, lambda p: (META_ROWS - 1) - p)
            vd = (vm_ref[:, h * VALUE_DIM:(h + 1) * VALUE_DIM].astype(F32) * kd).astype(BF16)
            state_sc[h] = jnp.dot(kt, vd, preferred_element_type=F32)
        row = lax.broadcasted_iota(jnp.int32, (chunk, chunk), 0)
        col = lax.broadcasted_iota(jnp.int32, (chunk, chunk), 1)
        rel = (row - col).astype(F32)
        for h in range(RET_HEADS):
            decay_sc[h] = jnp.where(rel >= 0, jnp.exp(_LOG_GAMMA[h] * jnp.maximum(rel, 0.0)), 0.0)

    lane = lax.broadcasted_iota(jnp.int32, (chunk, LANES), 1)
    for pair in range(RET_HEADS // 2):
        q_pair = q_ref[:, pair * LANES:(pair + 1) * LANES]
        k_pair = k_ref[:, pair * LANES:(pair + 1) * LANES]
        kt = k_pair.astype(F32).T.astype(BF16)
        for h in (2 * pair, 2 * pair + 1):
            lg = _LOG_GAMMA[h]
            own = (lane < HEAD_DIM) if h % 2 == 0 else (lane >= HEAD_DIM)
            qm = jnp.where(own, q_pair, jnp.zeros_like(q_pair))
            v = v_ref[:, h * VALUE_DIM:(h + 1) * VALUE_DIM]
            s = lax.dot_general(qm, k_pair, (((1,), (1,)), ((), ())), preferred_element_type=F32)
            y = jnp.dot((s * decay_sc[h]).astype(BF16), v, preferred_element_type=F32)
            qd = _decay_col(lg, chunk, lambda p: p + 1.0)
            state = state_sc[h]
            y = y + jnp.dot((qm.astype(F32) * qd).astype(BF16), state.astype(BF16),
                            preferred_element_type=F32)
            kd = _decay_col(lg, chunk, lambda p: (chunk - 1) - p)
            vd = (v.astype(F32) * kd).astype(BF16)
            state_sc[h] = math.exp(lg * chunk) * state + jnp.dot(kt, vd, preferred_element_type=F32)
            y = y * lax.rsqrt(jnp.mean(y * y, axis=-1, keepdims=True) + NORM_EPS)
            gate = g_ref[:, h * VALUE_DIM:(h + 1) * VALUE_DIM].astype(F32)
            o_ref[:, h * VALUE_DIM:(h + 1) * VALUE_DIM] = (y * gate * jax.nn.sigmoid(gate)).astype(o_ref.dtype)


def _retention(rq, rk, rv, rg, km, vm):
    b, seq, _ = rq.shape
    chunk = RET_CHUNK
    rows = lambda w: pl.BlockSpec((None, chunk, w), lambda bi, c: (bi, c, 0))
    return pl.pallas_call(
        functools.partial(_ret_kernel, chunk=chunk),
        grid=(b, seq // chunk),
        in_specs=[rows(QK_WIDTH), rows(QK_WIDTH), rows(RET_WIDTH), rows(RET_WIDTH),
                  pl.BlockSpec((META_ROWS, QK_WIDTH), lambda bi, c: (0, 0)),
                  pl.BlockSpec((META_ROWS, RET_WIDTH), lambda bi, c: (0, 0))],
        out_specs=rows(RET_WIDTH),
        out_shape=jax.ShapeDtypeStruct((b, seq, RET_WIDTH), BF16),
        scratch_shapes=[pltpu.VMEM((RET_HEADS, LANES, VALUE_DIM), F32),
                        pltpu.VMEM((RET_HEADS, chunk, chunk), F32)],
        compiler_params=_params(("parallel", "arbitrary")),
        name="retention",
    )(rq, rk, rv, rg, km, vm)


def _route_kernel(x_ref, d_ref, r_ref, wo_ref, g_ref, wrt_ref, brt_ref,
                  h_ref, xn_ref, pos_ref, col_ref, cnt_ref, *, tile):
    h = (x_ref[...]
         + jnp.dot(d_ref[...], wo_ref[0:DIFF_WIDTH, :], preferred_element_type=F32)
         + jnp.dot(r_ref[...], wo_ref[DIFF_WIDTH:, :], preferred_element_type=F32))
    h_ref[...] = h
    xn = h * lax.rsqrt(jnp.mean(h * h, axis=-1, keepdims=True) + NORM_EPS) * g_ref[...]
    xn_ref[...] = xn.astype(BF16)
    logits = lax.dot_general(wrt_ref[...], xn, (((1,), (1,)), ((), ())), preferred_element_type=F32,
                             precision=lax.Precision.HIGHEST) + brt_ref[...]

    expert = lax.broadcasted_iota(jnp.int32, logits.shape, 0)
    work = logits
    vals, hots = [], []
    for k in range(TOP_K):
        v = jnp.max(work, axis=0, keepdims=True)
        idx = jnp.min(jnp.where(work == v, expert, N_EXPERTS), axis=0, keepdims=True)
        hot = expert == idx
        work = jnp.where(hot, -jnp.inf, work)
        vals.append(v)
        hots.append(hot.astype(F32))
    exps = [jnp.exp(v - vals[0]) for v in vals]
    denom = exps[0] + exps[1] + exps[2] + exps[3]
    gates = [e / denom for e in exps]

    chosen = hots[0] + hots[1] + hots[2] + hots[3]
    c = lax.broadcasted_iota(jnp.int32, (tile, tile), 0)
    r = lax.broadcasted_iota(jnp.int32, (tile, tile), 1)
    earlier = (c < r).astype(BF16)
    before = jnp.dot(chosen.astype(BF16), earlier, preferred_element_type=F32)
    cnt = jnp.sum(chosen, axis=1, keepdims=True)
    cnt_pad = jnp.ceil(cnt / SEG_ALIGN) * SEG_ALIGN
    er = lax.broadcasted_iota(jnp.int32, (N_EXPERTS, N_EXPERTS), 0)
    ec = lax.broadcasted_iota(jnp.int32, (N_EXPERTS, N_EXPERTS), 1)
    seg_start = jnp.dot((ec < er).astype(F32), jnp.broadcast_to(cnt_pad, (N_EXPERTS, LANES)),
                        preferred_element_type=F32, precision=lax.Precision.HIGHEST)[:, 0:1]
    base = seg_start + before
    pos = [jnp.sum(hot * base, axis=0, keepdims=True) for hot in hots]
    for k in range(TOP_K):
        pos_ref[k:k + 1, :] = pos[k].astype(jnp.int32)
    cnt_ref[...] = jnp.broadcast_to(cnt, (N_EXPERTS, LANES)).astype(jnp.int32)
    stacked = jnp.concatenate(pos + gates + [jnp.zeros((LANES - 2 * TOP_K, tile), F32)], axis=0)
    col_ref[...] = stacked.T


def _out_proj_and_route(x, diff, ret, wo, g, wrt, brt):
    n = x.shape[0]
    tile = ROUTE_TILE
    rows = lambda w: pl.BlockSpec((tile, w), lambda i: (i, 0))
    full = lambda a, b: pl.BlockSpec((a, b), lambda i: (0, 0))
    return pl.pallas_call(
        functools.partial(_route_kernel, tile=tile),
        grid=(n // tile,),
        in_specs=[rows(D_MODEL), rows(DIFF_WIDTH), rows(RET_WIDTH), full(D_MODEL, D_MODEL),
                  full(1, D_MODEL), full(N_EXPERTS, D_MODEL), full(N_EXPERTS, 1)],
        out_specs=[rows(D_MODEL), rows(D_MODEL),
                   pl.BlockSpec((TOP_K, tile), lambda i: (0, i)), rows(LANES),
                   pl.BlockSpec((None, N_EXPERTS, LANES), lambda i: (i, 0, 0))],
        out_shape=[jax.ShapeDtypeStruct((n, D_MODEL), F32), jax.ShapeDtypeStruct((n, D_MODEL), BF16),
                   jax.ShapeDtypeStruct((TOP_K, n), jnp.int32), jax.ShapeDtypeStruct((n, LANES), F32),
                   jax.ShapeDtypeStruct((n // tile, N_EXPERTS, LANES), jnp.int32)],
        compiler_params=_params(("parallel",)),
        name="out_proj_route",
    )(x, diff, ret, wo, g, wrt, brt)


def _split_up_kernel(w_ref, o_ref):
    group = 2 * LANES
    src = lax.broadcasted_iota(jnp.int32, (group, group), 0)
    dst = lax.broadcasted_iota(jnp.int32, (group, group), 1)
    select = (src == jnp.where(dst < LANES, 2 * dst, 2 * (dst - LANES) + 1)).astype(BF16)
    for c in range(2 * D_FF // group):
        part = jnp.dot(w_ref[:, c * group:(c + 1) * group].astype(BF16), select,
                       preferred_element_type=F32).astype(BF16)
        o_ref[:, c * LANES:(c + 1) * LANES] = part[:, :LANES]
        o_ref[:, D_FF + c * LANES:D_FF + (c + 1) * LANES] = part[:, LANES:]


def _split_up_weights(w_up):
    n_exp, d, f2 = w_up.shape
    tile = 256
    spec = pl.BlockSpec((None, tile, f2), lambda e, i: (e, i, 0))
    return pl.pallas_call(
        _split_up_kernel,
        grid=(n_exp, d // tile),
        in_specs=[spec],
        out_specs=spec,
        out_shape=jax.ShapeDtypeStruct((n_exp, d, f2), BF16),
        compiler_params=_params(("parallel", "parallel")),
        name="split_up_weights",
    )(w_up)


def _segment_copy(local_buf, local_row, global_hbm, global_row, rows, sem, to_global):
    local_piece = local_buf.at[pl.ds(local_row, rows), :]
    global_piece = global_hbm.at[pl.ds(global_row, rows), :]
    if to_global:
        return pltpu.make_async_copy(local_piece, global_piece, sem)
    return pltpu.make_async_copy(global_piece, local_piece, sem)


def _start_segments(tables, tile_index, local_buf, global_hbm, sem, to_global):
    local_ref, global_ref, n_big_ref, n_small_ref, _ = tables
    first = tile_index * N_EXPERTS

    def pieces(local_row, global_row, count, rows):
        def per_piece(p, inner):
            lo = pl.multiple_of(local_row + p * rows, SEG_ALIGN)
            gl = pl.multiple_of(global_row + p * rows, SEG_ALIGN)
            _segment_copy(local_buf, lo, global_hbm, gl, rows, sem, to_global).start()
            return inner

        lax.fori_loop(0, count, per_piece, 0)

    def per_expert(e, carry):
        local_row, global_row = local_ref[first + e], global_ref[first + e]
        big_rows = n_big_ref[first + e] * BIG_PIECE
        pieces(local_row, global_row, n_big_ref[first + e], BIG_PIECE)
        pieces(local_row + big_rows, global_row + big_rows, n_small_ref[first + e], SEG_ALIGN)
        return carry

    lax.fori_loop(0, N_EXPERTS, per_expert, 0)


def _wait_segments(tables, tile_index, local_buf, global_hbm, sem, to_global):
    n_pad_ref = tables[4]
    _segment_copy(local_buf, 0, global_hbm, 0, ROUTE_TILE * TOP_K, sem, to_global).wait()

    def per_pad(p, carry):
        _segment_copy(local_buf, 0, global_hbm, 0, SEG_ALIGN, sem, to_global).wait()
        return carry

    lax.fori_loop(0, n_pad_ref[tile_index], per_pad, 0)


def _dispatch_kernel(local_ref, global_ref, n_big_ref, n_small_ref, n_pad_ref, fill_ref, xn_ref, pos_ref,
                     xs_out, loc_buf, zero_buf, seg_sem, fill_sem, *, n_blocks, n_tiles):
    i = pl.program_id(0)
    slot = i % 2

    @pl.when(i == 0)
    def _():
        zero_buf[...] = jnp.zeros_like(zero_buf)

        def fill_copy(j):
            row = pl.multiple_of(j * MOE_TILE, MOE_TILE)
            return pltpu.make_async_copy(zero_buf, xs_out.at[pl.ds(row, MOE_TILE), :], fill_sem)

        def start(j, carry):
            @pl.when(fill_ref[j] > 0)
            def _():
                fill_copy(j).start()
            return carry

        def wait(j, carry):
            @pl.when(fill_ref[j] > 0)
            def _():
                fill_copy(j).wait()
            return carry

        lax.fori_loop(0, n_blocks, start, 0)
        lax.fori_loop(0, n_blocks, wait, 0)

    pos = pos_ref[...]
    row = lax.broadcasted_iota(jnp.int32, (LOC_ROWS, pos.shape[1]), 0)
    select = jnp.zeros(row.shape, F32)
    for k in range(TOP_K):
        select = jnp.where(row == pos[k:k + 1], 1.0, select)
    loc_buf[slot] = jnp.dot(select.astype(BF16), xn_ref[...], preferred_element_type=F32)

    tables = (local_ref, global_ref, n_big_ref, n_small_ref, n_pad_ref)
    _start_segments(tables, i, loc_buf.at[slot], xs_out, seg_sem.at[slot], True)

    @pl.when(i > 0)
    def _():
        _wait_segments(tables, i - 1, loc_buf.at[1 - slot], xs_out, seg_sem.at[1 - slot], True)

    @pl.when(i == n_tiles - 1)
    def _():
        _wait_segments(tables, i, loc_buf.at[slot], xs_out, seg_sem.at[slot], True)


def _dispatch(tables, fill_flags, xn, pos, n_blocks):
    n = xn.shape[0]
    tile = ROUTE_TILE
    return pl.pallas_call(
        functools.partial(_dispatch_kernel, n_blocks=n_blocks, n_tiles=n // tile),
        grid_spec=pltpu.PrefetchScalarGridSpec(
            num_scalar_prefetch=6,
            grid=(n // tile,),
            in_specs=[pl.BlockSpec((tile, D_MODEL), lambda i, *_: (i, 0)),
                      pl.BlockSpec((TOP_K, tile), lambda i, *_: (0, i))],
            out_specs=pl.BlockSpec(memory_space=pl.ANY),
            scratch_shapes=[pltpu.VMEM((2, LOC_ROWS, D_MODEL), F32),
                            pltpu.VMEM((MOE_TILE, D_MODEL), F32),
                            pltpu.SemaphoreType.DMA((2,)), pltpu.SemaphoreType.DMA]),
        out_shape=jax.ShapeDtypeStruct((n_blocks * MOE_TILE, D_MODEL), F32),
        compiler_params=_params(("arbitrary",)),
        name="moe_dispatch",
    )(*tables, fill_flags, xn, pos)


def _expert_kernel(be_ref, nu_ref, xs_ref, wu_ref, bu_ref, wd_ref, bd_ref, ys_ref):
    del be_ref

    @pl.when(pl.program_id(0) < nu_ref[0])
    def _():
        x = xs_ref[...].astype(BF16)
        glu = jnp.dot(x, wu_ref[:, 0:D_FF], preferred_element_type=F32) + bu_ref[:, 0:D_FF]
        lin = jnp.dot(x, wu_ref[:, D_FF:], preferred_element_type=F32) + bu_ref[:, D_FF:]
        glu = jnp.minimum(glu, SWIGLU_LIMIT)
        lin = jnp.clip(lin, -SWIGLU_LIMIT, SWIGLU_LIMIT)
        act = glu * jax.nn.sigmoid(SWIGLU_ALPHA * glu) * (lin + 1.0)
        ys_ref[...] = (jnp.dot(act.astype(BF16), wd_ref[...].astype(BF16), preferred_element_type=F32)
                       + bd_ref[...])

    @pl.when(pl.program_id(0) >= nu_ref[0])
    def _():
        ys_ref[...] = jnp.zeros_like(ys_ref)


def _expert_ffn(block_expert, n_used, xs, wu, bu, wd, bd):
    n_rows = xs.shape[0]
    tile = MOE_TILE
    rows = pl.BlockSpec((tile, D_MODEL), lambda j, be, nu: (jnp.minimum(j, nu[0] - 1), 0))
    per_expert = lambda a, b: pl.BlockSpec((None, a, b), lambda j, be, nu: (be[j], 0, 0))
    return pl.pallas_call(
        _expert_kernel,
        grid_spec=pltpu.PrefetchScalarGridSpec(
            num_scalar_prefetch=2,
            grid=(n_rows // tile,),
            in_specs=[rows, per_expert(D_MODEL, 2 * D_FF), per_expert(1, 2 * D_FF),
                      per_expert(D_FF, D_MODEL), per_expert(1, D_MODEL)],
            out_specs=pl.BlockSpec((tile, D_MODEL), lambda j, be, nu: (j, 0))),
        out_shape=jax.ShapeDtypeStruct((n_rows, D_MODEL), F32),
        compiler_params=_params(("arbitrary",)),
        name="expert_ffn",
    )(block_expert, n_used, xs, wu, bu, wd, bd)


def _combine_kernel(local_ref, global_ref, n_big_ref, n_small_ref, n_pad_ref, col_ref, h_ref, ys_hbm, g_ref,
                    o_ref, loc_buf, seg_sem, *, n_tiles):
    i = pl.program_id(0)
    slot = i % 2
    tables = (local_ref, global_ref, n_big_ref, n_small_ref, n_pad_ref)

    def fetch(tile_index, buf_slot):
        _start_segments(tables, tile_index, loc_buf.at[buf_slot], ys_hbm, seg_sem.at[buf_slot], False)

    @pl.when(i == 0)
    def _():
        loc_buf[...] = jnp.zeros_like(loc_buf)
        fetch(0, 0)

    @pl.when(i + 1 < n_tiles)
    def _():
        fetch(i + 1, 1 - slot)

    _wait_segments(tables, i, loc_buf.at[slot], ys_hbm, seg_sem.at[slot], False)

    rows = loc_buf[slot].astype(BF16)
    lane = lax.broadcasted_iota(jnp.int32, (COMBINE_ROWS, LOC_ROWS), 1)
    for r in range(0, col_ref.shape[0], COMBINE_ROWS):
        col = col_ref[r:r + COMBINE_ROWS, :]
        weights = jnp.zeros((COMBINE_ROWS, LOC_ROWS), F32)
        for k in range(TOP_K):
            weights = jnp.where(lane == col[:, k:k + 1].astype(jnp.int32),
                                col[:, TOP_K + k:TOP_K + k + 1], weights)
        h = h_ref[r:r + COMBINE_ROWS, :] + jnp.dot(weights.astype(BF16), rows, preferred_element_type=F32)
        o_ref[r:r + COMBINE_ROWS, :] = (h * lax.rsqrt(jnp.mean(h * h, axis=-1, keepdims=True) + NORM_EPS)
                                        * g_ref[...])


def _combine(tables, col, h, ys, g):
    n = h.shape[0]
    tile = ROUTE_TILE
    return pl.pallas_call(
        functools.partial(_combine_kernel, n_tiles=n // tile),
        grid_spec=pltpu.PrefetchScalarGridSpec(
            num_scalar_prefetch=5,
            grid=(n // tile,),
            in_specs=[pl.BlockSpec((tile, LANES), lambda i, *_: (i, 0)),
                      pl.BlockSpec((tile, D_MODEL), lambda i, *_: (i, 0)),
                      pl.BlockSpec(memory_space=pl.ANY),
                      pl.BlockSpec((1, D_MODEL), lambda i, *_: (0, 0))],
            out_specs=pl.BlockSpec((tile, D_MODEL), lambda i, *_: (i, 0)),
            scratch_shapes=[pltpu.VMEM((2, LOC_ROWS, D_MODEL), F32), pltpu.SemaphoreType.DMA((2,))]),
        out_shape=jax.ShapeDtypeStruct((n, D_MODEL), F32),
        compiler_params=_params(("arbitrary",)),
        name="moe_combine",
    )(*tables, col, h, ys, g)


def kernel(x, meta_tokens, attn_norm_g, w_in, diff_lambda, diff_subln_g, w_out, ffn_norm_g,
           w_router, b_router, w_up, b_up, w_down, b_down, final_norm_g):
    b, seq, d = x.shape
    assert d == D_MODEL and seq % PROJ_TILE == 0 and seq % RET_CHUNK == 0 and seq % ATTN_TILE == 0
    assert w_in.shape[0] == 1, "one layer"
    n_tok = b * seq
    assert n_tok % ROUTE_TILE == 0

    w_in_b = w_in[0].astype(BF16)
    w_out_b = w_out[0].astype(BF16)
    w_up_b = _split_up_weights(w_up[0])
    b_up_s = jnp.concatenate([b_up[0][..., 0::2], b_up[0][..., 1::2]], axis=-1)[:, None, :]
    w_down_b = w_down[0]
    b_down_s = b_down[0][:, None, :]

    diff_inv_freq = ROPE_THETA ** (-jnp.arange(0, HEAD_DIM, 2, dtype=F32) / HEAD_DIM)
    ret_inv_freq = ROPE_THETA ** (-jnp.linspace(0.0, 1.0, HEAD_DIM // 2, dtype=F32))
    pos_x = jnp.arange(seq, dtype=F32) + N_META
    pos_m = jnp.arange(META_ROWS, dtype=F32) - (META_ROWS - N_META)
    tables_x = _rope_tables(pos_x, diff_inv_freq) + _rope_tables(pos_x, ret_inv_freq)
    tables_m = _rope_tables(pos_m, diff_inv_freq) + _rope_tables(pos_m, ret_inv_freq)

    g_attn = attn_norm_g[0][None, :]
    x2 = x.reshape(n_tok, D_MODEL)
    meta_rows = jnp.concatenate(
        [jnp.zeros((META_ROWS - N_META, D_MODEL), x.dtype), meta_tokens.astype(x.dtype)], axis=0)
    dq, dk, dv, rq, rk, rv, rg = _in_projection(x2, g_attn, w_in_b, tables_x, PROJ_TILE, seq // PROJ_TILE)
    _, dk_m, dv_m, _, rk_m, rv_m, _ = _in_projection(meta_rows, g_attn, w_in_b, tables_m, META_ROWS, 1)

    per_batch = lambda t: t.reshape(b, seq, t.shape[-1])
    diff_out = _diff_attention(per_batch(dq), per_batch(dk), per_batch(dv), dk_m, dv_m,
                               diff_lambda[0], diff_subln_g[0][:, None])
    ret_out = _retention(per_batch(rq), per_batch(rk), per_batch(rv), per_batch(rg), rk_m, rv_m)

    h, xn, pos, col, counts = _out_proj_and_route(
        x2, diff_out.reshape(n_tok, DIFF_WIDTH), ret_out.reshape(n_tok, RET_WIDTH),
        w_out_b, ffn_norm_g[0][None, :], w_router[0].T, b_router[0][:, None])

    n_tiles = n_tok // ROUTE_TILE
    n_blocks = -(-(n_tok * TOP_K + n_tiles * N_EXPERTS * (SEG_ALIGN - 1) + N_EXPERTS * (MOE_TILE - 1))
                 // MOE_TILE)
    seg_rows = (counts[:, :, 0] + SEG_ALIGN - 1) // SEG_ALIGN * SEG_ALIGN
    seg_local = jnp.cumsum(seg_rows, axis=1) - seg_rows
    blocks_per = (jnp.sum(seg_rows, axis=0) + MOE_TILE - 1) // MOE_TILE
    block_end = jnp.cumsum(blocks_per)
    group_start = (block_end - blocks_per) * MOE_TILE
    seg_global = group_start[None, :] + jnp.cumsum(seg_rows, axis=0) - seg_rows
    flat = lambda t: t.reshape(-1).astype(jnp.int32)
    pad_pieces = (jnp.sum(seg_rows, axis=1) - ROUTE_TILE * TOP_K) // SEG_ALIGN
    tables = (flat(seg_local), flat(seg_global), flat(seg_rows // BIG_PIECE),
              flat(seg_rows % BIG_PIECE // SEG_ALIGN), flat(pad_pieces))

    n_used = block_end[-1:].astype(jnp.int32)
    all_blocks = jnp.arange(n_blocks, dtype=jnp.int32)
    block_ids = jnp.minimum(all_blocks, n_used[0] - 1)
    block_expert = jnp.minimum(jnp.sum(block_end[None, :] <= block_ids[:, None], axis=1),
                               N_EXPERTS - 1).astype(jnp.int32)
    is_group_end = jnp.any((block_end[None, :] == all_blocks[:, None] + 1) & (blocks_per[None, :] > 0), axis=1)
    fill_flags = (is_group_end | (all_blocks >= n_used[0])).astype(jnp.int32)

    xs = _dispatch(tables, fill_flags, xn, pos, n_blocks)
    ys = _expert_ffn(block_expert, n_used, xs, w_up_b, b_up_s, w_down_b, b_down_s)
    out = _combine(tables, col, h, ys, final_norm_g[None, :])
    return out.reshape(b, seq, D_MODEL)
```

```python
import functools
import math

import jax
import jax.numpy as jnp
import numpy as np
from jax import lax
from jax.experimental import pallas as pl
from jax.experimental.pallas import tpu as pltpu

F32 = jnp.float32
BF16 = jnp.bfloat16

D_MODEL = 1024
N_META = 16
ROPE_THETA = 10000.0
NORM_EPS = 1e-5
DIFF_HEADS = 4
HEAD_DIM = 64
VALUE_DIM = 128
DIFF_WIDTH = DIFF_HEADS * VALUE_DIM
RET_HEADS = 4
RET_WIDTH = RET_HEADS * VALUE_DIM
QK_WIDTH = RET_HEADS * HEAD_DIM
IN_PROJ_WIDTH = 3 * DIFF_WIDTH + 2 * QK_WIDTH + 2 * RET_WIDTH
LAMBDA_INIT = 0.8 - 0.6 * math.exp(-0.3 * 0)
N_EXPERTS = 32
TOP_K = 4
D_FF = D_MODEL
SWIGLU_ALPHA = 1.702
SWIGLU_LIMIT = 7.0

LANES = 128
MXU_WIDTH = 256
META_ROWS = 128
VMEM_LIMIT = 56 * 1024 * 1024

ONES_ROWS = 16
LOG2_E = math.log2(math.e)

PROJ_TILE = 512
ATTN_TILE = 512
HEADS_PER_STEP = 4
RET_CHUNK = 256
ROUTE_TILE = 512
ROUTE_GROUP = 256
MOE_TILE = 512
SEG_ALIGN = 8
BIG_PIECE = 4 * SEG_ALIGN
COMBINE_ROWS = 128
LOC_ROWS = ROUTE_TILE * TOP_K + N_EXPERTS * SEG_ALIGN

_LOG_GAMMA = [float(v) for v in np.log1p(-np.exp2(-5.0 - np.arange(RET_HEADS, dtype=np.float32)))]


def _params(sem):
    return pltpu.CompilerParams(dimension_semantics=sem, vmem_limit_bytes=VMEM_LIMIT)


def _rope(t, cos, sin_signed):
    lane = lax.broadcasted_iota(jnp.int32, t.shape, 1)
    first_half = (lane % HEAD_DIM) < (HEAD_DIM // 2)
    partner = jnp.where(first_half, pltpu.roll(t, LANES - HEAD_DIM // 2, 1),
                        pltpu.roll(t, HEAD_DIM // 2, 1))
    return t * cos + partner * sin_signed


def _inproj_kernel(x_ref, g_ref, w_ref, cd_ref, sd_ref, cr_ref, sr_ref,
                   dq_ref, dk_ref, dv_ref, rq_ref, rk_ref, rv_ref, rg_ref):
    x = x_ref[...]
    ms = jnp.mean(x * x, axis=-1, keepdims=True)
    hn = (x * lax.rsqrt(ms + NORM_EPS) * g_ref[...]).astype(BF16)
    cd, sd, cr, sr = cd_ref[...], sd_ref[...], cr_ref[...], sr_ref[...]
    scale = HEAD_DIM ** -0.5

    def proj(col, width):
        return jnp.dot(hn, w_ref[:, col:col + width], preferred_element_type=F32)

    col = 0
    for out_ref, width, cos, sin, mul in (
            (dq_ref, DIFF_WIDTH, cd, sd, scale * LOG2_E), (dk_ref, DIFF_WIDTH, cd, sd, None),
            (dv_ref, DIFF_WIDTH, None, None, None),
            (rq_ref, QK_WIDTH, cr, sr, None), (rk_ref, QK_WIDTH, cr, sr, scale),
            (rv_ref, RET_WIDTH, None, None, None), (rg_ref, RET_WIDTH, None, None, None)):
        for c in range(0, width, MXU_WIDTH):
            wide = proj(col + c, MXU_WIDTH)
            for half in range(0, MXU_WIDTH, LANES):
                p = wide[:, half:half + LANES]
                if cos is not None:
                    p = _rope(p, cos, sin)
                if mul is not None:
                    p = p * mul
                out_ref[:, c + half:c + half + LANES] = p.astype(out_ref.dtype)
        col += width


def _in_projection(rows, g, w_bf16, tables, tile, seq_tiles):
    n = rows.shape[0]
    row_spec = lambda w: pl.BlockSpec((tile, w), lambda i: (i, 0))
    table_spec = pl.BlockSpec((tile, LANES), lambda i: (i % seq_tiles, 0))
    widths = (DIFF_WIDTH, DIFF_WIDTH, DIFF_WIDTH, QK_WIDTH, QK_WIDTH, RET_WIDTH, RET_WIDTH)
    return pl.pallas_call(
        _inproj_kernel,
        grid=(n // tile,),
        in_specs=[row_spec(D_MODEL),
                  pl.BlockSpec((1, D_MODEL), lambda i: (0, 0)),
                  pl.BlockSpec((D_MODEL, IN_PROJ_WIDTH), lambda i: (0, 0)),
                  table_spec, table_spec, table_spec, table_spec],
        out_specs=[row_spec(w) for w in widths],
        out_shape=[jax.ShapeDtypeStruct((n, w), BF16) for w in widths],
        compiler_params=_params(("parallel",)),
        name="in_projection",
    )(rows, g, w_bf16, *tables)


def _rope_tables(pos, inv_freq):
    ang = pos[:, None] * inv_freq[None, :]
    cos = jnp.tile(jnp.cos(ang), (1, LANES // (HEAD_DIM // 2)))
    sin = jnp.sin(ang)
    sin_signed = jnp.tile(jnp.concatenate([-sin, sin], axis=1), (1, LANES // HEAD_DIM))
    return cos, sin_signed


def _transpose_bf16(t):
    return t.astype(F32).T.astype(BF16)


def _with_ones_rows(vt):
    return jnp.concatenate([vt, jnp.ones((ONES_ROWS, vt.shape[1]), vt.dtype)], axis=0)


def _attn_kernel(q_ref, k_ref, v_ref, km_ref, vm_ref, lp_ref, sg_ref, o_ref,
                 vt_sc, m_sc, acc_sc, sa_sc, sb_sc, *, tile, n_chunks):
    i = pl.program_id(2)
    heads = range(HEADS_PER_STEP)
    lanes = lambda hh: slice(hh * LANES, (hh + 1) * LANES)

    @pl.when(i == 0)
    def _():
        def transpose_chunk(j, carry):
            start = pl.multiple_of(j * tile, tile)
            for hh in heads:
                vt_sc[hh, j] = _with_ones_rows(_transpose_bf16(v_ref[pl.ds(start, tile), lanes(hh)]))
            return carry
        lax.fori_loop(0, n_chunks, transpose_chunk, 0)

    qcat = []
    for hh in heads:
        qt = q_ref[:, lanes(hh)].astype(F32).T
        sub = lax.broadcasted_iota(jnp.int32, qt.shape, 0)
        qcat.append(jnp.concatenate([jnp.where(sub < HEAD_DIM, qt, 0.0).astype(BF16),
                                     jnp.where(sub >= HEAD_DIM, qt, 0.0).astype(BF16)], axis=1))

    def scores(hh, j):
        start = pl.multiple_of(j * tile, tile)
        return jnp.dot(k_ref[pl.ds(start, tile), lanes(hh)], qcat[hh], preferred_element_type=F32)

    for hh in heads:
        sa_sc[hh] = scores(hh, 0)
        m_sc[hh] = jnp.full(m_sc.shape[1:], -jnp.inf, F32)
        acc_sc[hh] = jnp.zeros(acc_sc.shape[1:], F32)

    meta_row = lax.broadcasted_iota(jnp.int32, (META_ROWS, 2 * tile), 0) >= META_ROWS - N_META
    s_meta = [jnp.where(meta_row, jnp.dot(km_ref[:, lanes(hh)], qcat[hh], preferred_element_type=F32), -jnp.inf)
              for hh in heads]
    vmt = [_with_ones_rows(_transpose_bf16(vm_ref[:, lanes(hh)])) for hh in heads]

    def absorb(hh, s_ref, j, last):
        s = s_ref[hh]
        m_old = m_sc[hh]
        if last:
            key = lax.broadcasted_iota(jnp.int32, s.shape, 0)
            lane = lax.broadcasted_iota(jnp.int32, s.shape, 1)
            s = jnp.where(key <= jnp.where(lane >= tile, lane - tile, lane), s, -jnp.inf)
            m_old = jnp.maximum(m_old, jnp.max(s_meta[hh], axis=0, keepdims=True))
        m_new = jnp.maximum(m_old, jnp.max(s, axis=0, keepdims=True))
        update = jnp.dot(vt_sc[hh, j], jnp.exp2(s - m_new).astype(BF16), preferred_element_type=F32)
        if last:
            update = update + jnp.dot(vmt[hh], jnp.exp2(s_meta[hh] - m_new).astype(BF16),
                                      preferred_element_type=F32)
        acc_sc[hh] = jnp.exp2(m_sc[hh] - m_new) * acc_sc[hh] + update
        m_sc[hh] = m_new

    def pair(t, carry):
        for hh in heads:
            sb_sc[hh] = scores(hh, 2 * t + 1)
            absorb(hh, sa_sc, 2 * t, False)
        for hh in heads:
            sa_sc[hh] = scores(hh, 2 * t + 2)
            absorb(hh, sb_sc, 2 * t + 1, False)
        return carry

    lax.fori_loop(0, i // 2, pair, 0)

    @pl.when(i % 2 == 0)
    def _():
        for hh in heads:
            absorb(hh, sa_sc, i, True)

    @pl.when(i % 2 == 1)
    def _():
        for hh in heads:
            sb_sc[hh] = scores(hh, i)
            absorb(hh, sa_sc, i - 1, False)
        for hh in heads:
            absorb(hh, sb_sc, i, True)

    lp = lp_ref[...]
    lam = (jnp.exp(jnp.sum(lp[0:1] * lp[1:2], axis=-1, keepdims=True))
           - jnp.exp(jnp.sum(lp[2:3] * lp[3:4], axis=-1, keepdims=True)) + LAMBDA_INIT)
    for hh in heads:
        num, den = acc_sc[hh, 0:VALUE_DIM, :], acc_sc[hh, VALUE_DIM:VALUE_DIM + 1, :]
        o = num[:, :tile] / den[:, :tile] - lam * (num[:, tile:] / den[:, tile:])
        o = o * lax.rsqrt(jnp.mean(o * o, axis=0, keepdims=True) + NORM_EPS)
        o_ref[:, lanes(hh)] = (o * sg_ref[...] * (1.0 - LAMBDA_INIT)).T.astype(o_ref.dtype)


def _diff_attention(dq, dk, dv, km, vm, lam_params, subln_g):
    b, seq, _ = dq.shape
    tile = ATTN_TILE
    n_chunks = seq // tile
    width = HEADS_PER_STEP * LANES
    head_rows = pl.BlockSpec((None, tile, width), lambda bi, h, i: (bi, i, h))
    head_seq = pl.BlockSpec((None, seq, width), lambda bi, h, i: (bi, 0, h))
    head_meta = pl.BlockSpec((META_ROWS, width), lambda bi, h, i: (0, h))
    per_head = lambda *shape: pltpu.VMEM((HEADS_PER_STEP,) + shape, F32)
    return pl.pallas_call(
        functools.partial(_attn_kernel, tile=tile, n_chunks=n_chunks),
        grid=(b, DIFF_HEADS // HEADS_PER_STEP, n_chunks),
        in_specs=[head_rows, head_seq, head_seq, head_meta, head_meta,
                  pl.BlockSpec((4, HEAD_DIM), lambda bi, h, i: (0, 0)),
                  pl.BlockSpec((VALUE_DIM, 1), lambda bi, h, i: (0, 0))],
        out_specs=head_rows,
        out_shape=jax.ShapeDtypeStruct((b, seq, DIFF_WIDTH), BF16),
        scratch_shapes=[pltpu.VMEM((HEADS_PER_STEP, n_chunks, VALUE_DIM + ONES_ROWS, tile), BF16),
                        per_head(1, 2 * tile), per_head(VALUE_DIM + ONES_ROWS, 2 * tile),
                        per_head(tile, 2 * tile), per_head(tile, 2 * tile)],
        compiler_params=_params(("parallel", "parallel", "arbitrary")),
        name="diff_attention",
    )(dq, dk, dv, km, vm, lam_params, subln_g)


def _decay_col(log_gamma, n, offset_fn):
    pos = lax.broadcasted_iota(jnp.int32, (n, 1), 0).astype(F32)
    return jnp.exp(log_gamma * offset_fn(pos))


def _ret_kernel(q_ref, k_ref, v_ref, g_ref, km_ref, vm_ref, o_ref, state_sc, decay_sc, *, chunk):
    c = pl.program_id(1)

    @pl.when(c == 0)
    def _():
        km = km_ref[...].astype(F32)
        for h in range(RET_HEADS):
            pair = h // 2
            kt = km[:, pair * LANES:(pair + 1) * LANES].T.astype(BF16)
            kd = _decay_col(_LOG_GAMMA[h], META_ROWS, lambda p: (META_ROWS - 1) - p)
            vd = (vm_ref[:, h * VALUE_DIM:(h + 1) * VALUE_DIM].astype(F32) * kd).astype(BF16)
            state_sc[h] = jnp.dot(kt, vd, preferred_element_type=F32)
        row = lax.broadcasted_iota(jnp.int32, (chunk, chunk), 0)
        col = lax.broadcasted_iota(jnp.int32, (chunk, chunk), 1)
        rel = (row - col).astype(F32)
        for h in range(RET_HEADS):
            decay_sc[h] = jnp.where(rel >= 0, jnp.exp(_LOG_GAMMA[h] * jnp.maximum(rel, 0.0)), 0.0)

    lane = lax.broadcasted_iota(jnp.int32, (chunk, LANES), 1)
    for pair in range(RET_HEADS // 2):
        q_pair = q_ref[:, pair * LANES:(pair + 1) * LANES]
        k_pair = k_ref[:, pair * LANES:(pair + 1) * LANES]
        kt = k_pair.astype(F32).T.astype(BF16)
        for h in (2 * pair, 2 * pair + 1):
            lg = _LOG_GAMMA[h]
            own = (lane < HEAD_DIM) if h % 2 == 0 else (lane >= HEAD_DIM)
            qm = jnp.where(own, q_pair, jnp.zeros_like(q_pair))
            v = v_ref[:, h * VALUE_DIM:(h + 1) * VALUE_DIM]
            s = lax.dot_general(qm, k_pair, (((1,), (1,)), ((), ())), preferred_element_type=F32)
            y = jnp.dot((s * decay_sc[h]).astype(BF16), v, preferred_element_type=F32)
            qd = _decay_col(lg, chunk, lambda p: p + 1.0)
            state = state_sc[h]
            y = y + jnp.dot((qm.astype(F32) * qd).astype(BF16), state.astype(BF16),
                            preferred_element_type=F32)
            kd = _decay_col(lg, chunk, lambda p: (chunk - 1) - p)
            vd = (v.astype(F32) * kd).astype(BF16)
            state_sc[h] = math.exp(lg * chunk) * state + jnp.dot(kt, vd, preferred_element_type=F32)
            y = y * lax.rsqrt(jnp.mean(y * y, axis=-1, keepdims=True) + NORM_EPS)
            gate = g_ref[:, h * VALUE_DIM:(h + 1) * VALUE_DIM].astype(F32)
            o_ref[:, h * VALUE_DIM:(h + 1) * VALUE_DIM] = (y * gate * jax.nn.sigmoid(gate)).astype(o_ref.dtype)


def _retention(rq, rk, rv, rg, km, vm):
    b, seq, _ = rq.shape
    chunk = RET_CHUNK
    rows = lambda w: pl.BlockSpec((None, chunk, w), lambda bi, c: (bi, c, 0))
    return pl.pallas_call(
        functools.partial(_ret_kernel, chunk=chunk),
        grid=(b, seq // chunk),
        in_specs=[rows(QK_WIDTH), rows(QK_WIDTH), rows(RET_WIDTH), rows(RET_WIDTH),
                  pl.BlockSpec((META_ROWS, QK_WIDTH), lambda bi, c: (0, 0)),
                  pl.BlockSpec((META_ROWS, RET_WIDTH), lambda bi, c: (0, 0))],
        out_specs=rows(RET_WIDTH),
        out_shape=jax.ShapeDtypeStruct((b, seq, RET_WIDTH), BF16),
        scratch_shapes=[pltpu.VMEM((RET_HEADS, LANES, VALUE_DIM), F32),
                        pltpu.VMEM((RET_HEADS, chunk, chunk), F32)],
        compiler_params=_params(("parallel", "arbitrary")),
        name="retention",
    )(rq, rk, rv, rg, km, vm)


def _route_kernel(x_ref, d_ref, r_ref, wo_ref, g_ref, wrh_ref, wrl_ref, brt_ref,
                  h_ref, xn_ref, pos_ref, col_ref, cnt_ref, *, tile):
    group_hots, group_gates = [], []
    for r0 in range(0, tile, ROUTE_GROUP):
        rows = slice(r0, r0 + ROUTE_GROUP)
        h = (x_ref[rows, :]
             + jnp.dot(d_ref[rows, :], wo_ref[0:DIFF_WIDTH, :], preferred_element_type=F32)
             + jnp.dot(r_ref[rows, :], wo_ref[DIFF_WIDTH:, :], preferred_element_type=F32))
        h_ref[rows, :] = h
        xn = h * lax.rsqrt(jnp.mean(h * h, axis=-1, keepdims=True) + NORM_EPS) * g_ref[...]
        xn_hi = xn.astype(BF16)
        xn_ref[rows, :] = xn_hi
        xn_lo = (xn - xn_hi.astype(F32)).astype(BF16)
        logits = (jnp.dot(xn_hi, wrh_ref[...], preferred_element_type=F32)
                  + jnp.dot(xn_hi, wrl_ref[...], preferred_element_type=F32)
                  + jnp.dot(xn_lo, wrh_ref[...], preferred_element_type=F32))
        logits = logits.T[0:N_EXPERTS, :] + brt_ref[...]
        expert = lax.broadcasted_iota(jnp.int32, logits.shape, 0)
        work = logits
        vals, hots = [], []
        for k in range(TOP_K):
            v = jnp.max(work, axis=0, keepdims=True)
            idx = jnp.min(jnp.where(work == v, expert, N_EXPERTS), axis=0, keepdims=True)
            hot = expert == idx
            work = jnp.where(hot, -jnp.inf, work)
            vals.append(v)
            hots.append(hot.astype(F32))
        exps = [jnp.exp(v - vals[0]) for v in vals]
        denom = exps[0] + exps[1] + exps[2] + exps[3]
        group_hots.append(hots)
        group_gates.append([e / denom for e in exps])
    hots = [jnp.concatenate([g[k] for g in group_hots], axis=1) for k in range(TOP_K)]
    gates = [jnp.concatenate([g[k] for g in group_gates], axis=1) for k in range(TOP_K)]

    chosen = hots[0] + hots[1] + hots[2] + hots[3]
    c = lax.broadcasted_iota(jnp.int32, (tile, tile), 0)
    r = lax.broadcasted_iota(jnp.int32, (tile, tile), 1)
    earlier = (c < r).astype(BF16)
    before = jnp.dot(chosen.astype(BF16), earlier, preferred_element_type=F32)
    cnt = jnp.sum(chosen, axis=1, keepdims=True)
    cnt_pad = jnp.ceil(cnt / SEG_ALIGN) * SEG_ALIGN
    er = lax.broadcasted_iota(jnp.int32, (N_EXPERTS, N_EXPERTS), 0)
    ec = lax.broadcasted_iota(jnp.int32, (N_EXPERTS, N_EXPERTS), 1)
    seg_start = jnp.dot((ec < er).astype(F32), jnp.broadcast_to(cnt_pad, (N_EXPERTS, LANES)),
                        preferred_element_type=F32, precision=lax.Precision.HIGHEST)[:, 0:1]
    base = seg_start + before
    pos = [jnp.sum(hot * base, axis=0, keepdims=True) for hot in hots]
    for k in range(TOP_K):
        pos_ref[k:k + 1, :] = pos[k].astype(jnp.int32)
    cnt_ref[...] = jnp.broadcast_to(cnt, (N_EXPERTS, LANES)).astype(jnp.int32)
    stacked = jnp.concatenate(pos + gates + [jnp.zeros((LANES - 2 * TOP_K, tile), F32)], axis=0)
    col_ref[...] = stacked.T


def _out_proj_and_route(x, diff, ret, wo, g, wr, brt):
    n = x.shape[0]
    tile = ROUTE_TILE
    rows = lambda w: pl.BlockSpec((tile, w), lambda i: (i, 0))
    full = lambda a, b: pl.BlockSpec((a, b), lambda i: (0, 0))
    wr = jnp.pad(wr, ((0, 0), (0, LANES - N_EXPERTS)))
    wr_hi = wr.astype(BF16)
    wr_lo = (wr - wr_hi.astype(F32)).astype(BF16)
    return pl.pallas_call(
        functools.partial(_route_kernel, tile=tile),
        grid=(n // tile,),
        in_specs=[rows(D_MODEL), rows(DIFF_WIDTH), rows(RET_WIDTH), full(D_MODEL, D_MODEL),
                  full(1, D_MODEL), full(D_MODEL, LANES), full(D_MODEL, LANES), full(N_EXPERTS, 1)],
        out_specs=[rows(D_MODEL), rows(D_MODEL),
                   pl.BlockSpec((TOP_K, tile), lambda i: (0, i)), rows(LANES),
                   pl.BlockSpec((None, N_EXPERTS, LANES), lambda i: (i, 0, 0))],
        out_shape=[jax.ShapeDtypeStruct((n, D_MODEL), F32), jax.ShapeDtypeStruct((n, D_MODEL), BF16),
                   jax.ShapeDtypeStruct((TOP_K, n), jnp.int32), jax.ShapeDtypeStruct((n, LANES), F32),
                   jax.ShapeDtypeStruct((n // tile, N_EXPERTS, LANES), jnp.int32)],
        compiler_params=_params(("parallel",)),
        name="out_proj_route",
    )(x, diff, ret, wo, g, wr_hi, wr_lo, brt)


def _split_up_kernel(w_ref, o_ref):
    group = 2 * LANES
    src = lax.broadcasted_iota(jnp.int32, (group, group), 0)
    dst = lax.broadcasted_iota(jnp.int32, (group, group), 1)
    select = (src == jnp.where(dst < LANES, 2 * dst, 2 * (dst - LANES) + 1)).astype(BF16)
    for c in range(2 * D_FF // group):
        part = jnp.dot(w_ref[:, c * group:(c + 1) * group].astype(BF16), select,
                       preferred_element_type=F32).astype(BF16)
        o_ref[:, c * LANES:(c + 1) * LANES] = part[:, :LANES]
        o_ref[:, D_FF + c * LANES:D_FF + (c + 1) * LANES] = part[:, LANES:]


def _split_up_weights(w_up):
    n_exp, d, f2 = w_up.shape
    tile = 256
    spec = pl.BlockSpec((None, tile, f2), lambda e, i: (e, i, 0))
    return pl.pallas_call(
        _split_up_kernel,
        grid=(n_exp, d // tile),
        in_specs=[spec],
        out_specs=spec,
        out_shape=jax.ShapeDtypeStruct((n_exp, d, f2), BF16),
        compiler_params=_params(("parallel", "parallel")),
        name="split_up_weights",
    )(w_up)


def _segment_copy(local_buf, local_row, global_hbm, global_row, rows, sem, to_global):
    local_piece = local_buf.at[pl.ds(local_row, rows), :]
    global_piece = global_hbm.at[pl.ds(global_row, rows), :]
    if to_global:
        return pltpu.make_async_copy(local_piece, global_piece, sem)
    return pltpu.make_async_copy(global_piece, local_piece, sem)


def _start_segments(tables, tile_index, local_buf, global_hbm, sem, to_global):
    local_ref, global_ref, n_big_ref, n_small_ref, _ = tables
    first = tile_index * N_EXPERTS

    def pieces(local_row, global_row, count, rows):
        def per_piece(p, inner):
            lo = pl.multiple_of(local_row + p * rows, SEG_ALIGN)
            gl = pl.multiple_of(global_row + p * rows, SEG_ALIGN)
            _segment_copy(local_buf, lo, global_hbm, gl, rows, sem, to_global).start()
            return inner

        lax.fori_loop(0, count, per_piece, 0)

    def per_expert(e, carry):
        local_row, global_row = local_ref[first + e], global_ref[first + e]
        big_rows = n_big_ref[first + e] * BIG_PIECE
        pieces(local_row, global_row, n_big_ref[first + e], BIG_PIECE)
        pieces(local_row + big_rows, global_row + big_rows, n_small_ref[first + e], SEG_ALIGN)
        return carry

    lax.fori_loop(0, N_EXPERTS, per_expert, 0)


def _wait_segments(tables, tile_index, local_buf, global_hbm, sem, to_global):
    n_pad_ref = tables[4]
    _segment_copy(local_buf, 0, global_hbm, 0, ROUTE_TILE * TOP_K, sem, to_global).wait()

    def per_pad(p, carry):
        _segment_copy(local_buf, 0, global_hbm, 0, SEG_ALIGN, sem, to_global).wait()
        return carry

    lax.fori_loop(0, n_pad_ref[tile_index], per_pad, 0)


def _dispatch_kernel(local_ref, global_ref, n_big_ref, n_small_ref, n_pad_ref, fill_ref, xn_ref, pos_ref,
                     xs_out, loc_buf, zero_buf, seg_sem, fill_sem, *, n_blocks, n_tiles):
    i = pl.program_id(0)
    slot = i % 2

    @pl.when(i == 0)
    def _():
        zero_buf[...] = jnp.zeros_like(zero_buf)

        def fill_copy(j):
            row = pl.multiple_of(j * MOE_TILE, MOE_TILE)
            return pltpu.make_async_copy(zero_buf, xs_out.at[pl.ds(row, MOE_TILE), :], fill_sem)

        def start(j, carry):
            @pl.when(fill_ref[j] > 0)
            def _():
                fill_copy(j).start()
            return carry

        def wait(j, carry):
            @pl.when(fill_ref[j] > 0)
            def _():
                fill_copy(j).wait()
            return carry

        lax.fori_loop(0, n_blocks, start, 0)
        lax.fori_loop(0, n_blocks, wait, 0)

    pos = pos_ref[...]
    row = lax.broadcasted_iota(jnp.int32, (LOC_ROWS, pos.shape[1]), 0)
    select = jnp.zeros(row.shape, F32)
    for k in range(TOP_K):
        select = jnp.where(row == pos[k:k + 1], 1.0, select)
    loc_buf[slot] = jnp.dot(select.astype(BF16), xn_ref[...], preferred_element_type=F32)

    tables = (local_ref, global_ref, n_big_ref, n_small_ref, n_pad_ref)
    _start_segments(tables, i, loc_buf.at[slot], xs_out, seg_sem.at[slot], True)

    @pl.when(i > 0)
    def _():
        _wait_segments(tables, i - 1, loc_buf.at[1 - slot], xs_out, seg_sem.at[1 - slot], True)

    @pl.when(i == n_tiles - 1)
    def _():
        _wait_segments(tables, i, loc_buf.at[slot], xs_out, seg_sem.at[slot], True)


def _dispatch(tables, fill_flags, xn, pos, n_blocks):
    n = xn.shape[0]
    tile = ROUTE_TILE
    return pl.pallas_call(
        functools.partial(_dispatch_kernel, n_blocks=n_blocks, n_tiles=n // tile),
        grid_spec=pltpu.PrefetchScalarGridSpec(
            num_scalar_prefetch=6,
            grid=(n // tile,),
            in_specs=[pl.BlockSpec((tile, D_MODEL), lambda i, *_: (i, 0)),
                      pl.BlockSpec((TOP_K, tile), lambda i, *_: (0, i))],
            out_specs=pl.BlockSpec(memory_space=pl.ANY),
            scratch_shapes=[pltpu.VMEM((2, LOC_ROWS, D_MODEL), F32),
                            pltpu.VMEM((MOE_TILE, D_MODEL), F32),
                            pltpu.SemaphoreType.DMA((2,)), pltpu.SemaphoreType.DMA]),
        out_shape=jax.ShapeDtypeStruct((n_blocks * MOE_TILE, D_MODEL), F32),
        compiler_params=_params(("arbitrary",)),
        name="moe_dispatch",
    )(*tables, fill_flags, xn, pos)


def _expert_kernel(be_ref, nu_ref, xs_ref, wu_ref, bu_ref, wd_ref, bd_ref, ys_ref):
    del be_ref

    @pl.when(pl.program_id(0) < nu_ref[0])
    def _():
        x = xs_ref[...].astype(BF16)
        glu = jnp.dot(x, wu_ref[:, 0:D_FF], preferred_element_type=F32) + bu_ref[:, 0:D_FF]
        lin = jnp.dot(x, wu_ref[:, D_FF:], preferred_element_type=F32) + bu_ref[:, D_FF:]
        glu = jnp.minimum(glu, SWIGLU_LIMIT)
        lin = jnp.clip(lin, -SWIGLU_LIMIT, SWIGLU_LIMIT)
        act = glu * jax.nn.sigmoid(SWIGLU_ALPHA * glu) * (lin + 1.0)
        ys_ref[...] = (jnp.dot(act.astype(BF16), wd_ref[...].astype(BF16), preferred_element_type=F32)
                       + bd_ref[...])

    @pl.when(pl.program_id(0) >= nu_ref[0])
    def _():
        ys_ref[...] = jnp.zeros_like(ys_ref)


def _expert_ffn(block_expert, n_used, xs, wu, bu, wd, bd):
    n_rows = xs.shape[0]
    tile = MOE_TILE
    rows = pl.BlockSpec((tile, D_MODEL), lambda j, be, nu: (jnp.minimum(j, nu[0] - 1), 0))
    per_expert = lambda a, b: pl.BlockSpec((None, a, b), lambda j, be, nu: (be[j], 0, 0))
    return pl.pallas_call(
        _expert_kernel,
        grid_spec=pltpu.PrefetchScalarGridSpec(
            num_scalar_prefetch=2,
            grid=(n_rows // tile,),
            in_specs=[rows, per_expert(D_MODEL, 2 * D_FF), per_expert(1, 2 * D_FF),
                      per_expert(D_FF, D_MODEL), per_expert(1, D_MODEL)],
            out_specs=pl.BlockSpec((tile, D_MODEL), lambda j, be, nu: (j, 0))),
        out_shape=jax.ShapeDtypeStruct((n_rows, D_MODEL), F32),
        compiler_params=_params(("arbitrary",)),
        name="expert_ffn",
    )(block_expert, n_used, xs, wu, bu, wd, bd)


def _combine_kernel(local_ref, global_ref, n_big_ref, n_small_ref, n_pad_ref, col_ref, h_ref, ys_hbm, g_ref,
                    o_ref, loc_buf, seg_sem, *, n_tiles):
    i = pl.program_id(0)
    slot = i % 2
    tables = (local_ref, global_ref, n_big_ref, n_small_ref, n_pad_ref)

    def fetch(tile_index, buf_slot):
        _start_segments(tables, tile_index, loc_buf.at[buf_slot], ys_hbm, seg_sem.at[buf_slot], False)

    @pl.when(i == 0)
    def _():
        loc_buf[...] = jnp.zeros_like(loc_buf)
        fetch(0, 0)

    @pl.when(i + 1 < n_tiles)
    def _():
        fetch(i + 1, 1 - slot)

    _wait_segments(tables, i, loc_buf.at[slot], ys_hbm, seg_sem.at[slot], False)

    rows = loc_buf[slot].astype(BF16)
    lane = lax.broadcasted_iota(jnp.int32, (COMBINE_ROWS, LOC_ROWS), 1)
    for r in range(0, col_ref.shape[0], COMBINE_ROWS):
        col = col_ref[r:r + COMBINE_ROWS, :]
        weights = jnp.zeros((COMBINE_ROWS, LOC_ROWS), F32)
        for k in range(TOP_K):
            weights = jnp.where(lane == col[:, k:k + 1].astype(jnp.int32),
                                col[:, TOP_K + k:TOP_K + k + 1], weights)
        h = h_ref[r:r + COMBINE_ROWS, :] + jnp.dot(weights.astype(BF16), rows, preferred_element_type=F32)
        o_ref[r:r + COMBINE_ROWS, :] = (h * lax.rsqrt(jnp.mean(h * h, axis=-1, keepdims=True) + NORM_EPS)
                                        * g_ref[...])


def _combine(tables, col, h, ys, g):
    n = h.shape[0]
    tile = ROUTE_TILE
    return pl.pallas_call(
        functools.partial(_combine_kernel, n_tiles=n // tile),
        grid_spec=pltpu.PrefetchScalarGridSpec(
            num_scalar_prefetch=5,
            grid=(n // tile,),
            in_specs=[pl.BlockSpec((tile, LANES), lambda i, *_: (i, 0)),
                      pl.BlockSpec((tile, D_MODEL), lambda i, *_: (i, 0)),
                      pl.BlockSpec(memory_space=pl.ANY),
                      pl.BlockSpec((1, D_MODEL), lambda i, *_: (0, 0))],
            out_specs=pl.BlockSpec((tile, D_MODEL), lambda i, *_: (i, 0)),
            scratch_shapes=[pltpu.VMEM((2, LOC_ROWS, D_MODEL), F32), pltpu.SemaphoreType.DMA((2,))]),
        out_shape=jax.ShapeDtypeStruct((n, D_MODEL), F32),
        compiler_params=_params(("arbitrary",)),
        name="moe_combine",
    )(*tables, col, h, ys, g)


def kernel(x, meta_tokens, attn_norm_g, w_in, diff_lambda, diff_subln_g, w_out, ffn_norm_g,
           w_router, b_router, w_up, b_up, w_down, b_down, final_norm_g):
    b, seq, d = x.shape
    assert d == D_MODEL and seq % PROJ_TILE == 0 and seq % RET_CHUNK == 0 and seq % ATTN_TILE == 0
    assert w_in.shape[0] == 1, "one layer"
    n_tok = b * seq
    assert n_tok % ROUTE_TILE == 0

    w_in_b = w_in[0].astype(BF16)
    w_out_b = w_out[0].astype(BF16)
    w_up_b = _split_up_weights(w_up[0])
    b_up_s = jnp.concatenate([b_up[0][..., 0::2], b_up[0][..., 1::2]], axis=-1)[:, None, :]
    w_down_b = w_down[0]
    b_down_s = b_down[0][:, None, :]

    diff_inv_freq = ROPE_THETA ** (-jnp.arange(0, HEAD_DIM, 2, dtype=F32) / HEAD_DIM)
    ret_inv_freq = ROPE_THETA ** (-jnp.linspace(0.0, 1.0, HEAD_DIM // 2, dtype=F32))
    pos_x = jnp.arange(seq, dtype=F32) + N_META
    pos_m = jnp.arange(META_ROWS, dtype=F32) - (META_ROWS - N_META)
    tables_x = _rope_tables(pos_x, diff_inv_freq) + _rope_tables(pos_x, ret_inv_freq)
    tables_m = _rope_tables(pos_m, diff_inv_freq) + _rope_tables(pos_m, ret_inv_freq)

    g_attn = attn_norm_g[0][None, :]
    x2 = x.reshape(n_tok, D_MODEL)
    meta_rows = jnp.concatenate(
        [jnp.zeros((META_ROWS - N_META, D_MODEL), x.dtype), meta_tokens.astype(x.dtype)], axis=0)
    dq, dk, dv, rq, rk, rv, rg = _in_projection(x2, g_attn, w_in_b, tables_x, PROJ_TILE, seq // PROJ_TILE)
    _, dk_m, dv_m, _, rk_m, rv_m, _ = _in_projection(meta_rows, g_attn, w_in_b, tables_m, META_ROWS, 1)

    per_batch = lambda t: t.reshape(b, seq, t.shape[-1])
    diff_out = _diff_attention(per_batch(dq), per_batch(dk), per_batch(dv), dk_m, dv_m,
                               diff_lambda[0], diff_subln_g[0][:, None])
    ret_out = _retention(per_batch(rq), per_batch(rk), per_batch(rv), per_batch(rg), rk_m, rv_m)

    h, xn, pos, col, counts = _out_proj_and_route(
        x2, diff_out.reshape(n_tok, DIFF_WIDTH), ret_out.reshape(n_tok, RET_WIDTH),
        w_out_b, ffn_norm_g[0][None, :], w_router[0], b_router[0][:, None])

    n_tiles = n_tok // ROUTE_TILE
    n_blocks = -(-(n_tok * TOP_K + n_tiles * N_EXPERTS * (SEG_ALIGN - 1) + N_EXPERTS * (MOE_TILE - 1))
                 // MOE_TILE)
    seg_rows = (counts[:, :, 0] + SEG_ALIGN - 1) // SEG_ALIGN * SEG_ALIGN
    seg_local = jnp.cumsum(seg_rows, axis=1) - seg_rows
    blocks_per = (jnp.sum(seg_rows, axis=0) + MOE_TILE - 1) // MOE_TILE
    block_end = jnp.cumsum(blocks_per)
    group_start = (block_end - blocks_per) * MOE_TILE
    seg_global = group_start[None, :] + jnp.cumsum(seg_rows, axis=0) - seg_rows
    flat = lambda t: t.reshape(-1).astype(jnp.int32)
    pad_pieces = (jnp.sum(seg_rows, axis=1) - ROUTE_TILE * TOP_K) // SEG_ALIGN
    tables = (flat(seg_local), flat(seg_global), flat(seg_rows // BIG_PIECE),
              flat(seg_rows % BIG_PIECE // SEG_ALIGN), flat(pad_pieces))

    n_used = block_end[-1:].astype(jnp.int32)
    all_blocks = jnp.arange(n_blocks, dtype=jnp.int32)
    block_ids = jnp.minimum(all_blocks, n_used[0] - 1)
    block_expert = jnp.minimum(jnp.sum(block_end[None, :] <= block_ids[:, None], axis=1),
                               N_EXPERTS - 1).astype(jnp.int32)
    is_group_end = jnp.any((block_end[None, :] == all_blocks[:, None] + 1) & (blocks_per[None, :] > 0), axis=1)
    fill_flags = (is_group_end | (all_blocks >= n_used[0])).astype(jnp.int32)

    xs = _dispatch(tables, fill_flags, xn, pos, n_blocks)
    ys = _expert_ffn(block_expert, n_used, xs, w_up_b, b_up_s, w_down_b, b_down_s)
    out = _combine(tables, col, h, ys, final_norm_g[None, :])
    return out.reshape(b, seq, D_MODEL)
```

```python
import functools
import math

import jax
import jax.numpy as jnp
import numpy as np
from jax import lax
from jax.experimental import pallas as pl
from jax.experimental.pallas import tpu as pltpu

F32 = jnp.float32
BF16 = jnp.bfloat16

D_MODEL = 1024
N_META = 16
ROPE_THETA = 10000.0
NORM_EPS = 1e-5
DIFF_HEADS = 4
HEAD_DIM = 64
VALUE_DIM = 128
DIFF_WIDTH = DIFF_HEADS * VALUE_DIM
RET_HEADS = 4
RET_WIDTH = RET_HEADS * VALUE_DIM
QK_WIDTH = RET_HEADS * HEAD_DIM
IN_PROJ_WIDTH = 3 * DIFF_WIDTH + 2 * QK_WIDTH + 2 * RET_WIDTH
LAMBDA_INIT = 0.8 - 0.6 * math.exp(-0.3 * 0)
N_EXPERTS = 32
TOP_K = 4
D_FF = D_MODEL
SWIGLU_ALPHA = 1.702
SWIGLU_LIMIT = 7.0

LANES = 128
MXU_WIDTH = 256
META_ROWS = 128
VMEM_LIMIT = 56 * 1024 * 1024

ONES_ROWS = 16
LOG2_E = math.log2(math.e)

PROJ_TILE = 512
ATTN_TILE = 512
HEADS_PER_STEP = 4
RET_CHUNK = 256
ROUTE_TILE = 512
ROUTE_GROUP = 256
MOE_TILE = 512
SEG_ALIGN = 8
BIG_PIECE = 4 * SEG_ALIGN
COMBINE_ROWS = 128
LOC_ROWS = ROUTE_TILE * TOP_K + N_EXPERTS * SEG_ALIGN
MAX_BIG_PIECES = LOC_ROWS // BIG_PIECE
MAX_SMALL_PIECES = N_EXPERTS * (BIG_PIECE // SEG_ALIGN - 1)

_LOG_GAMMA = [float(v) for v in np.log1p(-np.exp2(-5.0 - np.arange(RET_HEADS, dtype=np.float32)))]


def _params(sem):
    return pltpu.CompilerParams(dimension_semantics=sem, vmem_limit_bytes=VMEM_LIMIT)


def _rope(t, cos, sin_signed):
    lane = lax.broadcasted_iota(jnp.int32, t.shape, 1)
    first_half = (lane % HEAD_DIM) < (HEAD_DIM // 2)
    partner = jnp.where(first_half, pltpu.roll(t, LANES - HEAD_DIM // 2, 1),
                        pltpu.roll(t, HEAD_DIM // 2, 1))
    return t * cos + partner * sin_signed


def _inproj_kernel(x_ref, g_ref, w_ref, cd_ref, sd_ref, cr_ref, sr_ref,
                   dq_ref, dk_ref, dv_ref, rq_ref, rk_ref, rv_ref, rg_ref):
    x = x_ref[...]
    ms = jnp.mean(x * x, axis=-1, keepdims=True)
    hn = (x * lax.rsqrt(ms + NORM_EPS) * g_ref[...]).astype(BF16)
    cd, sd, cr, sr = cd_ref[...], sd_ref[...], cr_ref[...], sr_ref[...]
    scale = HEAD_DIM ** -0.5

    def proj(col, width):
        return jnp.dot(hn, w_ref[:, col:col + width], preferred_element_type=F32)

    col = 0
    for out_ref, width, cos, sin, mul in (
            (dq_ref, DIFF_WIDTH, cd, sd, scale * LOG2_E), (dk_ref, DIFF_WIDTH, cd, sd, None),
            (dv_ref, DIFF_WIDTH, None, None, None),
            (rq_ref, QK_WIDTH, cr, sr, None), (rk_ref, QK_WIDTH, cr, sr, scale),
            (rv_ref, RET_WIDTH, None, None, None), (rg_ref, RET_WIDTH, None, None, None)):
        for c in range(0, width, MXU_WIDTH):
            wide = proj(col + c, MXU_WIDTH)
            for half in range(0, MXU_WIDTH, LANES):
                p = wide[:, half:half + LANES]
                if cos is not None:
                    p = _rope(p, cos, sin)
                if mul is not None:
                    p = p * mul
                out_ref[:, c + half:c + half + LANES] = p.astype(out_ref.dtype)
        col += width


def _in_projection(rows, g, w_bf16, tables, tile, seq_tiles):
    n = rows.shape[0]
    row_spec = lambda w: pl.BlockSpec((tile, w), lambda i: (i, 0))
    table_spec = pl.BlockSpec((tile, LANES), lambda i: (i % seq_tiles, 0))
    widths = (DIFF_WIDTH, DIFF_WIDTH, DIFF_WIDTH, QK_WIDTH, QK_WIDTH, RET_WIDTH, RET_WIDTH)
    return pl.pallas_call(
        _inproj_kernel,
        grid=(n // tile,),
        in_specs=[row_spec(D_MODEL),
                  pl.BlockSpec((1, D_MODEL), lambda i: (0, 0)),
                  pl.BlockSpec((D_MODEL, IN_PROJ_WIDTH), lambda i: (0, 0)),
                  table_spec, table_spec, table_spec, table_spec],
        out_specs=[row_spec(w) for w in widths],
        out_shape=[jax.ShapeDtypeStruct((n, w), BF16) for w in widths],
        compiler_params=_params(("parallel",)),
        name="in_projection",
    )(rows, g, w_bf16, *tables)


def _rope_tables(pos, inv_freq):
    ang = pos[:, None] * inv_freq[None, :]
    cos = jnp.tile(jnp.cos(ang), (1, LANES // (HEAD_DIM // 2)))
    sin = jnp.sin(ang)
    sin_signed = jnp.tile(jnp.concatenate([-sin, sin], axis=1), (1, LANES // HEAD_DIM))
    return cos, sin_signed


def _transpose_bf16(t):
    return t.astype(F32).T.astype(BF16)


def _with_ones_rows(vt):
    return jnp.concatenate([vt, jnp.ones((ONES_ROWS, vt.shape[1]), vt.dtype)], axis=0)


def _attn_kernel(q_ref, k_ref, v_ref, km_ref, vm_ref, lp_ref, sg_ref, o_ref,
                 vt_sc, m_sc, acc_sc, sa_sc, sb_sc, *, tile, n_chunks):
    i = pl.program_id(2)
    heads = range(HEADS_PER_STEP)
    lanes = lambda hh: slice(hh * LANES, (hh + 1) * LANES)

    @pl.when(i == 0)
    def _():
        def transpose_chunk(j, carry):
            start = pl.multiple_of(j * tile, tile)
            for hh in heads:
                vt_sc[hh, j] = _with_ones_rows(_transpose_bf16(v_ref[pl.ds(start, tile), lanes(hh)]))
            return carry
        lax.fori_loop(0, n_chunks, transpose_chunk, 0)

    qcat = []
    for hh in heads:
        qt = q_ref[:, lanes(hh)].astype(F32).T
        sub = lax.broadcasted_iota(jnp.int32, qt.shape, 0)
        qcat.append(jnp.concatenate([jnp.where(sub < HEAD_DIM, qt, 0.0).astype(BF16),
                                     jnp.where(sub >= HEAD_DIM, qt, 0.0).astype(BF16)], axis=1))

    def scores(hh, j):
        start = pl.multiple_of(j * tile, tile)
        return jnp.dot(k_ref[pl.ds(start, tile), lanes(hh)], qcat[hh], preferred_element_type=F32)

    for hh in heads:
        sa_sc[hh] = scores(hh, 0)
        m_sc[hh] = jnp.full(m_sc.shape[1:], -jnp.inf, F32)
        acc_sc[hh] = jnp.zeros(acc_sc.shape[1:], F32)

    meta_row = lax.broadcasted_iota(jnp.int32, (META_ROWS, 2 * tile), 0) >= META_ROWS - N_META
    s_meta = [jnp.where(meta_row, jnp.dot(km_ref[:, lanes(hh)], qcat[hh], preferred_element_type=F32), -jnp.inf)
              for hh in heads]
    vmt = [_with_ones_rows(_transpose_bf16(vm_ref[:, lanes(hh)])) for hh in heads]

    def absorb(hh, s_ref, j, last):
        s = s_ref[hh]
        m_old = m_sc[hh]
        if last:
            key = lax.broadcasted_iota(jnp.int32, s.shape, 0)
            lane = lax.broadcasted_iota(jnp.int32, s.shape, 1)
            s = jnp.where(key <= jnp.where(lane >= tile, lane - tile, lane), s, -jnp.inf)
            m_old = jnp.maximum(m_old, jnp.max(s_meta[hh], axis=0, keepdims=True))
        m_new = jnp.maximum(m_old, jnp.max(s, axis=0, keepdims=True))
        update = jnp.dot(vt_sc[hh, j], jnp.exp2(s - m_new).astype(BF16), preferred_element_type=F32)
        if last:
            update = update + jnp.dot(vmt[hh], jnp.exp2(s_meta[hh] - m_new).astype(BF16),
                                      preferred_element_type=F32)
        acc_sc[hh] = jnp.exp2(m_sc[hh] - m_new) * acc_sc[hh] + update
        m_sc[hh] = m_new

    def pair(t, carry):
        for hh in heads:
            sb_sc[hh] = scores(hh, 2 * t + 1)
            absorb(hh, sa_sc, 2 * t, False)
        for hh in heads:
            sa_sc[hh] = scores(hh, 2 * t + 2)
            absorb(hh, sb_sc, 2 * t + 1, False)
        return carry

    lax.fori_loop(0, i // 2, pair, 0)

    @pl.when(i % 2 == 0)
    def _():
        for hh in heads:
            absorb(hh, sa_sc, i, True)

    @pl.when(i % 2 == 1)
    def _():
        for hh in heads:
            sb_sc[hh] = scores(hh, i)
            absorb(hh, sa_sc, i - 1, False)
        for hh in heads:
            absorb(hh, sb_sc, i, True)

    lp = lp_ref[...]
    lam = (jnp.exp(jnp.sum(lp[0:1] * lp[1:2], axis=-1, keepdims=True))
           - jnp.exp(jnp.sum(lp[2:3] * lp[3:4], axis=-1, keepdims=True)) + LAMBDA_INIT)
    for hh in heads:
        num, den = acc_sc[hh, 0:VALUE_DIM, :], acc_sc[hh, VALUE_DIM:VALUE_DIM + 1, :]
        o = num[:, :tile] / den[:, :tile] - lam * (num[:, tile:] / den[:, tile:])
        o = o * lax.rsqrt(jnp.mean(o * o, axis=0, keepdims=True) + NORM_EPS)
        o_ref[:, lanes(hh)] = (o * sg_ref[...] * (1.0 - LAMBDA_INIT)).T.astype(o_ref.dtype)


def _diff_attention(dq, dk, dv, km, vm, lam_params, subln_g):
    b, seq, _ = dq.shape
    tile = ATTN_TILE
    n_chunks = seq // tile
    width = HEADS_PER_STEP * LANES
    head_rows = pl.BlockSpec((None, tile, width), lambda bi, h, i: (bi, i, h))
    head_seq = pl.BlockSpec((None, seq, width), lambda bi, h, i: (bi, 0, h))
    head_meta = pl.BlockSpec((META_ROWS, width), lambda bi, h, i: (0, h))
    per_head = lambda *shape: pltpu.VMEM((HEADS_PER_STEP,) + shape, F32)
    return pl.pallas_call(
        functools.partial(_attn_kernel, tile=tile, n_chunks=n_chunks),
        grid=(b, DIFF_HEADS // HEADS_PER_STEP, n_chunks),
        in_specs=[head_rows, head_seq, head_seq, head_meta, head_meta,
                  pl.BlockSpec((4, HEAD_DIM), lambda bi, h, i: (0, 0)),
                  pl.BlockSpec((VALUE_DIM, 1), lambda bi, h, i: (0, 0))],
        out_specs=head_rows,
        out_shape=jax.ShapeDtypeStruct((b, seq, DIFF_WIDTH), BF16),
        scratch_shapes=[pltpu.VMEM((HEADS_PER_STEP, n_chunks, VALUE_DIM + ONES_ROWS, tile), BF16),
                        per_head(1, 2 * tile), per_head(VALUE_DIM + ONES_ROWS, 2 * tile),
                        per_head(tile, 2 * tile), per_head(tile, 2 * tile)],
        compiler_params=_params(("parallel", "parallel", "arbitrary")),
        name="diff_attention",
    )(dq, dk, dv, km, vm, lam_params, subln_g)


def _decay_col(log_gamma, n, offset_fn):
    pos = lax.broadcasted_iota(jnp.int32, (n, 1), 0).astype(F32)
    return jnp.exp(log_gamma * offset_fn(pos))


def _ret_kernel(q_ref, k_ref, v_ref, g_ref, km_ref, vm_ref, o_ref, state_sc, decay_sc, *, chunk):
    c = pl.program_id(1)

    @pl.when(c == 0)
    def _():
        km = km_ref[...].astype(F32)
        for h in range(RET_HEADS):
            pair = h // 2
            kt = km[:, pair * LANES:(pair + 1) * LANES].T.astype(BF16)
            kd = _decay_col(_LOG_GAMMA[h], META_ROWS, lambda p: (META_ROWS - 1) - p)
            vd = (vm_ref[:, h * VALUE_DIM:(h + 1) * VALUE_DIM].astype(F32) * kd).astype(BF16)
            state_sc[h] = jnp.dot(kt, vd, preferred_element_type=F32)
        row = lax.broadcasted_iota(jnp.int32, (chunk, chunk), 0)
        col = lax.broadcasted_iota(jnp.int32, (chunk, chunk), 1)
        rel = (row - col).astype(F32)
        for h in range(RET_HEADS):
            decay_sc[h] = jnp.where(rel >= 0, jnp.exp(_LOG_GAMMA[h] * jnp.maximum(rel, 0.0)), 0.0)

    lane = lax.broadcasted_iota(jnp.int32, (chunk, LANES), 1)
    for pair in range(RET_HEADS // 2):
        q_pair = q_ref[:, pair * LANES:(pair + 1) * LANES]
        k_pair = k_ref[:, pair * LANES:(pair + 1) * LANES]
        kt = k_pair.astype(F32).T.astype(BF16)
        for h in (2 * pair, 2 * pair + 1):
            lg = _LOG_GAMMA[h]
            own = (lane < HEAD_DIM) if h % 2 == 0 else (lane >= HEAD_DIM)
            qm = jnp.where(own, q_pair, jnp.zeros_like(q_pair))
            v = v_ref[:, h * VALUE_DIM:(h + 1) * VALUE_DIM]
            s = lax.dot_general(qm, k_pair, (((1,), (1,)), ((), ())), preferred_element_type=F32)
            y = jnp.dot((s * decay_sc[h]).astype(BF16), v, preferred_element_type=F32)
            qd = _decay_col(lg, chunk, lambda p: p + 1.0)
            state = state_sc[h]
            y = y + jnp.dot((qm.astype(F32) * qd).astype(BF16), state.astype(BF16),
                            preferred_element_type=F32)
            kd = _decay_col(lg, chunk, lambda p: (chunk - 1) - p)
            vd = (v.astype(F32) * kd).astype(BF16)
            state_sc[h] = math.exp(lg * chunk) * state + jnp.dot(kt, vd, preferred_element_type=F32)
            y = y * lax.rsqrt(jnp.mean(y * y, axis=-1, keepdims=True) + NORM_EPS)
            gate = g_ref[:, h * VALUE_DIM:(h + 1) * VALUE_DIM].astype(F32)
            o_ref[:, h * VALUE_DIM:(h + 1) * VALUE_DIM] = (y * gate * jax.nn.sigmoid(gate)).astype(o_ref.dtype)


def _retention(rq, rk, rv, rg, km, vm):
    b, seq, _ = rq.shape
    chunk = RET_CHUNK
    rows = lambda w: pl.BlockSpec((None, chunk, w), lambda bi, c: (bi, c, 0))
    return pl.pallas_call(
        functools.partial(_ret_kernel, chunk=chunk),
        grid=(b, seq // chunk),
        in_specs=[rows(QK_WIDTH), rows(QK_WIDTH), rows(RET_WIDTH), rows(RET_WIDTH),
                  pl.BlockSpec((META_ROWS, QK_WIDTH), lambda bi, c: (0, 0)),
                  pl.BlockSpec((META_ROWS, RET_WIDTH), lambda bi, c: (0, 0))],
        out_specs=rows(RET_WIDTH),
        out_shape=jax.ShapeDtypeStruct((b, seq, RET_WIDTH), BF16),
        scratch_shapes=[pltpu.VMEM((RET_HEADS, LANES, VALUE_DIM), F32),
                        pltpu.VMEM((RET_HEADS, chunk, chunk), F32)],
        compiler_params=_params(("parallel", "arbitrary")),
        name="retention",
    )(rq, rk, rv, rg, km, vm)


def _route_kernel(x_ref, d_ref, r_ref, wo_ref, g_ref, wrh_ref, wrl_ref, brt_ref,
                  h_ref, xn_ref, pos_ref, col_ref, cnt_ref, *, tile):
    group_hots, group_gates = [], []
    for r0 in range(0, tile, ROUTE_GROUP):
        rows = slice(r0, r0 + ROUTE_GROUP)
        h = (x_ref[rows, :]
             + jnp.dot(d_ref[rows, :], wo_ref[0:DIFF_WIDTH, :], preferred_element_type=F32)
             + jnp.dot(r_ref[rows, :], wo_ref[DIFF_WIDTH:, :], preferred_element_type=F32))
        h_ref[rows, :] = h
        xn = h * lax.rsqrt(jnp.mean(h * h, axis=-1, keepdims=True) + NORM_EPS) * g_ref[...]
        xn_hi = xn.astype(BF16)
        xn_ref[rows, :] = xn_hi
        xn_lo = (xn - xn_hi.astype(F32)).astype(BF16)
        logits = (jnp.dot(xn_hi, wrh_ref[...], preferred_element_type=F32)
                  + jnp.dot(xn_hi, wrl_ref[...], preferred_element_type=F32)
                  + jnp.dot(xn_lo, wrh_ref[...], preferred_element_type=F32))
        logits = logits.T[0:N_EXPERTS, :] + brt_ref[...]
        expert = lax.broadcasted_iota(jnp.int32, logits.shape, 0)
        work = logits
        vals, hots = [], []
        for k in range(TOP_K):
            v = jnp.max(work, axis=0, keepdims=True)
            idx = jnp.min(jnp.where(work == v, expert, N_EXPERTS), axis=0, keepdims=True)
            hot = expert == idx
            work = jnp.where(hot, -jnp.inf, work)
            vals.append(v)
            hots.append(hot.astype(F32))
        exps = [jnp.exp(v - vals[0]) for v in vals]
        denom = exps[0] + exps[1] + exps[2] + exps[3]
        group_hots.append(hots)
        group_gates.append([e / denom for e in exps])
    hots = [jnp.concatenate([g[k] for g in group_hots], axis=1) for k in range(TOP_K)]
    gates = [jnp.concatenate([g[k] for g in group_gates], axis=1) for k in range(TOP_K)]

    chosen = hots[0] + hots[1] + hots[2] + hots[3]
    c = lax.broadcasted_iota(jnp.int32, (tile, tile), 0)
    r = lax.broadcasted_iota(jnp.int32, (tile, tile), 1)
    earlier = (c < r).astype(BF16)
    before = jnp.dot(chosen.astype(BF16), earlier, preferred_element_type=F32)
    cnt = jnp.sum(chosen, axis=1, keepdims=True)
    cnt_pad = jnp.ceil(cnt / SEG_ALIGN) * SEG_ALIGN
    er = lax.broadcasted_iota(jnp.int32, (N_EXPERTS, N_EXPERTS), 0)
    ec = lax.broadcasted_iota(jnp.int32, (N_EXPERTS, N_EXPERTS), 1)
    seg_start = jnp.dot((ec < er).astype(F32), jnp.broadcast_to(cnt_pad, (N_EXPERTS, LANES)),
                        preferred_element_type=F32, precision=lax.Precision.HIGHEST)[:, 0:1]
    base = seg_start + before
    pos = [jnp.sum(hot * base, axis=0, keepdims=True) for hot in hots]
    for k in range(TOP_K):
        pos_ref[k:k + 1, :] = pos[k].astype(jnp.int32)
    cnt_ref[...] = jnp.broadcast_to(cnt, (N_EXPERTS, LANES)).astype(jnp.int32)
    stacked = jnp.concatenate(pos + gates + [jnp.zeros((LANES - 2 * TOP_K, tile), F32)], axis=0)
    col_ref[...] = stacked.T


def _out_proj_and_route(x, diff, ret, wo, g, wr, brt):
    n = x.shape[0]
    tile = ROUTE_TILE
    rows = lambda w: pl.BlockSpec((tile, w), lambda i: (i, 0))
    full = lambda a, b: pl.BlockSpec((a, b), lambda i: (0, 0))
    wr = jnp.pad(wr, ((0, 0), (0, LANES - N_EXPERTS)))
    wr_hi = wr.astype(BF16)
    wr_lo = (wr - wr_hi.astype(F32)).astype(BF16)
    return pl.pallas_call(
        functools.partial(_route_kernel, tile=tile),
        grid=(n // tile,),
        in_specs=[rows(D_MODEL), rows(DIFF_WIDTH), rows(RET_WIDTH), full(D_MODEL, D_MODEL),
                  full(1, D_MODEL), full(D_MODEL, LANES), full(D_MODEL, LANES), full(N_EXPERTS, 1)],
        out_specs=[rows(D_MODEL), rows(D_MODEL),
                   pl.BlockSpec((TOP_K, tile), lambda i: (0, i)), rows(LANES),
                   pl.BlockSpec((None, N_EXPERTS, LANES), lambda i: (i, 0, 0))],
        out_shape=[jax.ShapeDtypeStruct((n, D_MODEL), F32), jax.ShapeDtypeStruct((n, D_MODEL), BF16),
                   jax.ShapeDtypeStruct((TOP_K, n), jnp.int32), jax.ShapeDtypeStruct((n, LANES), F32),
                   jax.ShapeDtypeStruct((n // tile, N_EXPERTS, LANES), jnp.int32)],
        compiler_params=_params(("parallel",)),
        name="out_proj_route",
    )(x, diff, ret, wo, g, wr_hi, wr_lo, brt)


def _split_gate_linear(w_ref, o_ref):
    group = 2 * LANES
    src = lax.broadcasted_iota(jnp.int32, (group, group), 0)
    dst = lax.broadcasted_iota(jnp.int32, (group, group), 1)
    select = (src == jnp.where(dst < LANES, 2 * dst, 2 * (dst - LANES) + 1)).astype(BF16)
    for c in range(2 * D_FF // group):
        part = jnp.dot(w_ref[:, c * group:(c + 1) * group].astype(BF16), select,
                       preferred_element_type=F32).astype(BF16)
        o_ref[:, c * LANES:(c + 1) * LANES] = part[:, :LANES]
        o_ref[:, D_FF + c * LANES:D_FF + (c + 1) * LANES] = part[:, LANES:]


def _segment_copy(local_buf, local_row, global_hbm, global_row, rows, sem, to_global):
    local_piece = local_buf.at[pl.ds(local_row, rows), :]
    global_piece = global_hbm.at[pl.ds(global_row, rows), :]
    if to_global:
        return pltpu.make_async_copy(local_piece, global_piece, sem)
    return pltpu.make_async_copy(global_piece, local_piece, sem)


def _piece_list(seg_local, seg_global, n_pieces, skip_rows, rows_per_piece, max_pieces):
    cum = jnp.cumsum(n_pieces, axis=1)
    j = jnp.arange(max_pieces, dtype=jnp.int32)
    owner = jnp.sum(cum[:, None, :] <= j[None, :, None], axis=2)
    onehot = owner[:, :, None] == jnp.arange(N_EXPERTS, dtype=jnp.int32)[None, None, :]
    pick = lambda t: jnp.sum(jnp.where(onehot, t[:, None, :], 0), axis=2)
    within = (j[None, :] - pick(cum - n_pieces)) * rows_per_piece
    return pick(seg_local + skip_rows) + within, pick(seg_global + skip_rows) + within, cum[:, -1]


def _start_segments(tables, tile_index, local_buf, global_hbm, sem, to_global):
    big_local, big_global, small_local, small_global, counts = tables

    def run(local_ref, global_ref, count, rows, stride):
        first = tile_index * stride

        def per_piece(p, carry):
            lo = pl.multiple_of(local_ref[first + p], SEG_ALIGN)
            gl = pl.multiple_of(global_ref[first + p], SEG_ALIGN)
            _segment_copy(local_buf, lo, global_hbm, gl, rows, sem, to_global).start()
            return carry

        lax.fori_loop(0, count, per_piece, 0)

    run(big_local, big_global, counts[3 * tile_index], BIG_PIECE, MAX_BIG_PIECES)
    run(small_local, small_global, counts[3 * tile_index + 1], SEG_ALIGN, MAX_SMALL_PIECES)


def _wait_segments(tables, tile_index, local_buf, global_hbm, sem, to_global):
    counts = tables[4]
    _segment_copy(local_buf, 0, global_hbm, 0, ROUTE_TILE * TOP_K, sem, to_global).wait()

    def per_pad(p, carry):
        _segment_copy(local_buf, 0, global_hbm, 0, SEG_ALIGN, sem, to_global).wait()
        return carry

    lax.fori_loop(0, counts[3 * tile_index + 2], per_pad, 0)


def _dispatch_kernel(big_local, big_global, small_local, small_global, counts, fill_ref, xn_ref, pos_ref,
                     xs_out, loc_buf, zero_buf, seg_sem, fill_sem, *, n_blocks, n_tiles):
    i = pl.program_id(0)
    slot = i % 2

    @pl.when(i == 0)
    def _():
        zero_buf[...] = jnp.zeros_like(zero_buf)

        def fill_copy(j):
            row = pl.multiple_of(j * MOE_TILE, MOE_TILE)
            return pltpu.make_async_copy(zero_buf, xs_out.at[pl.ds(row, MOE_TILE), :], fill_sem)

        def start(j, carry):
            @pl.when(fill_ref[j] > 0)
            def _():
                fill_copy(j).start()
            return carry

        def wait(j, carry):
            @pl.when(fill_ref[j] > 0)
            def _():
                fill_copy(j).wait()
            return carry

        lax.fori_loop(0, n_blocks, start, 0)
        lax.fori_loop(0, n_blocks, wait, 0)

    pos = pos_ref[...]
    row = lax.broadcasted_iota(jnp.int32, (LOC_ROWS, pos.shape[1]), 0)
    select = jnp.zeros(row.shape, F32)
    for k in range(TOP_K):
        select = jnp.where(row == pos[k:k + 1], 1.0, select)
    loc_buf[slot] = jnp.dot(select.astype(BF16), xn_ref[...], preferred_element_type=F32)

    tables = (big_local, big_global, small_local, small_global, counts)
    _start_segments(tables, i, loc_buf.at[slot], xs_out, seg_sem.at[slot], True)

    @pl.when(i > 0)
    def _():
        _wait_segments(tables, i - 1, loc_buf.at[1 - slot], xs_out, seg_sem.at[1 - slot], True)

    @pl.when(i == n_tiles - 1)
    def _():
        _wait_segments(tables, i, loc_buf.at[slot], xs_out, seg_sem.at[slot], True)


def _dispatch(tables, fill_flags, xn, pos, n_blocks):
    n = xn.shape[0]
    tile = ROUTE_TILE
    return pl.pallas_call(
        functools.partial(_dispatch_kernel, n_blocks=n_blocks, n_tiles=n // tile),
        grid_spec=pltpu.PrefetchScalarGridSpec(
            num_scalar_prefetch=6,
            grid=(n // tile,),
            in_specs=[pl.BlockSpec((tile, D_MODEL), lambda i, *_: (i, 0)),
                      pl.BlockSpec((TOP_K, tile), lambda i, *_: (0, i))],
            out_specs=pl.BlockSpec(memory_space=pl.ANY),
            scratch_shapes=[pltpu.VMEM((2, LOC_ROWS, D_MODEL), F32),
                            pltpu.VMEM((MOE_TILE, D_MODEL), F32),
                            pltpu.SemaphoreType.DMA((2,)), pltpu.SemaphoreType.DMA]),
        out_shape=jax.ShapeDtypeStruct((n_blocks * MOE_TILE, D_MODEL), F32),
        compiler_params=_params(("arbitrary",)),
        name="moe_dispatch",
    )(*tables, fill_flags, xn, pos)


def _expert_kernel(be_ref, nu_ref, xs_ref, wu_ref, bu_ref, wd_ref, bd_ref, ys_ref, wu_sc, wd_sc):
    j = pl.program_id(0)

    @pl.when((j == 0) | (be_ref[j] != be_ref[jnp.maximum(j - 1, 0)]))
    def _():
        _split_gate_linear(wu_ref, wu_sc)
        wd_sc[...] = wd_ref[...].astype(BF16)

    @pl.when(j < nu_ref[0])
    def _():
        x = xs_ref[...].astype(BF16)
        glu = jnp.dot(x, wu_sc[:, 0:D_FF], preferred_element_type=F32) + bu_ref[:, 0:D_FF]
        lin = jnp.dot(x, wu_sc[:, D_FF:], preferred_element_type=F32) + bu_ref[:, D_FF:]
        glu = jnp.minimum(glu, SWIGLU_LIMIT)
        lin = jnp.clip(lin, -SWIGLU_LIMIT, SWIGLU_LIMIT)
        act = glu * jax.nn.sigmoid(SWIGLU_ALPHA * glu) * (lin + 1.0)
        ys_ref[...] = jnp.dot(act.astype(BF16), wd_sc[...], preferred_element_type=F32) + bd_ref[...]

    @pl.when(j >= nu_ref[0])
    def _():
        ys_ref[...] = jnp.zeros_like(ys_ref)


def _expert_ffn(block_expert, n_used, xs, wu, bu, wd, bd):
    n_rows = xs.shape[0]
    tile = MOE_TILE
    rows = pl.BlockSpec((tile, D_MODEL), lambda j, be, nu: (jnp.minimum(j, nu[0] - 1), 0))
    per_expert = lambda a, b: pl.BlockSpec((None, a, b), lambda j, be, nu: (be[j], 0, 0))
    return pl.pallas_call(
        _expert_kernel,
        grid_spec=pltpu.PrefetchScalarGridSpec(
            num_scalar_prefetch=2,
            grid=(n_rows // tile,),
            in_specs=[rows, per_expert(D_MODEL, 2 * D_FF), per_expert(1, 2 * D_FF),
                      per_expert(D_FF, D_MODEL), per_expert(1, D_MODEL)],
            out_specs=pl.BlockSpec((tile, D_MODEL), lambda j, be, nu: (j, 0)),
            scratch_shapes=[pltpu.VMEM((D_MODEL, 2 * D_FF), BF16), pltpu.VMEM((D_FF, D_MODEL), BF16)]),
        out_shape=jax.ShapeDtypeStruct((n_rows, D_MODEL), F32),
        compiler_params=_params(("arbitrary",)),
        name="expert_ffn",
    )(block_expert, n_used, xs, wu, bu, wd, bd)


def _combine_kernel(big_local, big_global, small_local, small_global, counts, col_ref, h_ref, ys_hbm, g_ref,
                    o_ref, loc_buf, seg_sem, *, n_tiles):
    i = pl.program_id(0)
    slot = i % 2
    tables = (big_local, big_global, small_local, small_global, counts)

    def fetch(tile_index, buf_slot):
        _start_segments(tables, tile_index, loc_buf.at[buf_slot], ys_hbm, seg_sem.at[buf_slot], False)

    @pl.when(i == 0)
    def _():
        loc_buf[...] = jnp.zeros_like(loc_buf)
        fetch(0, 0)

    @pl.when(i + 1 < n_tiles)
    def _():
        fetch(i + 1, 1 - slot)

    _wait_segments(tables, i, loc_buf.at[slot], ys_hbm, seg_sem.at[slot], False)

    rows = loc_buf[slot].astype(BF16)
    lane = lax.broadcasted_iota(jnp.int32, (COMBINE_ROWS, LOC_ROWS), 1)
    for r in range(0, col_ref.shape[0], COMBINE_ROWS):
        col = col_ref[r:r + COMBINE_ROWS, :]
        weights = jnp.zeros((COMBINE_ROWS, LOC_ROWS), F32)
        for k in range(TOP_K):
            weights = jnp.where(lane == col[:, k:k + 1].astype(jnp.int32),
                                col[:, TOP_K + k:TOP_K + k + 1], weights)
        h = h_ref[r:r + COMBINE_ROWS, :] + jnp.dot(weights.astype(BF16), rows, preferred_element_type=F32)
        o_ref[r:r + COMBINE_ROWS, :] = (h * lax.rsqrt(jnp.mean(h * h, axis=-1, keepdims=True) + NORM_EPS)
                                        * g_ref[...])


def _combine(tables, col, h, ys, g):
    n = h.shape[0]
    tile = ROUTE_TILE
    return pl.pallas_call(
        functools.partial(_combine_kernel, n_tiles=n // tile),
        grid_spec=pltpu.PrefetchScalarGridSpec(
            num_scalar_prefetch=5,
            grid=(n // tile,),
            in_specs=[pl.BlockSpec((tile, LANES), lambda i, *_: (i, 0)),
                      pl.BlockSpec((tile, D_MODEL), lambda i, *_: (i, 0)),
                      pl.BlockSpec(memory_space=pl.ANY),
                      pl.BlockSpec((1, D_MODEL), lambda i, *_: (0, 0))],
            out_specs=pl.BlockSpec((tile, D_MODEL), lambda i, *_: (i, 0)),
            scratch_shapes=[pltpu.VMEM((2, LOC_ROWS, D_MODEL), F32), pltpu.SemaphoreType.DMA((2,))]),
        out_shape=jax.ShapeDtypeStruct((n, D_MODEL), F32),
        compiler_params=_params(("arbitrary",)),
        name="moe_combine",
    )(*tables, col, h, ys, g)


def kernel(x, meta_tokens, attn_norm_g, w_in, diff_lambda, diff_subln_g, w_out, ffn_norm_g,
           w_router, b_router, w_up, b_up, w_down, b_down, final_norm_g):
    b, seq, d = x.shape
    assert d == D_MODEL and seq % PROJ_TILE == 0 and seq % RET_CHUNK == 0 and seq % ATTN_TILE == 0
    assert w_in.shape[0] == 1, "one layer"
    n_tok = b * seq
    assert n_tok % ROUTE_TILE == 0

    w_in_b = w_in[0].astype(BF16)
    w_out_b = w_out[0].astype(BF16)
    w_up_b = w_up[0]
    b_up_s = jnp.concatenate([b_up[0][..., 0::2], b_up[0][..., 1::2]], axis=-1)[:, None, :]
    w_down_b = w_down[0]
    b_down_s = b_down[0][:, None, :]

    diff_inv_freq = ROPE_THETA ** (-jnp.arange(0, HEAD_DIM, 2, dtype=F32) / HEAD_DIM)
    ret_inv_freq = ROPE_THETA ** (-jnp.linspace(0.0, 1.0, HEAD_DIM // 2, dtype=F32))
    pos_x = jnp.arange(seq, dtype=F32) + N_META
    pos_m = jnp.arange(META_ROWS, dtype=F32) - (META_ROWS - N_META)
    tables_x = _rope_tables(pos_x, diff_inv_freq) + _rope_tables(pos_x, ret_inv_freq)
    tables_m = _rope_tables(pos_m, diff_inv_freq) + _rope_tables(pos_m, ret_inv_freq)

    g_attn = attn_norm_g[0][None, :]
    x2 = x.reshape(n_tok, D_MODEL)
    meta_rows = jnp.concatenate(
        [jnp.zeros((META_ROWS - N_META, D_MODEL), x.dtype), meta_tokens.astype(x.dtype)], axis=0)
    dq, dk, dv, rq, rk, rv, rg = _in_projection(x2, g_attn, w_in_b, tables_x, PROJ_TILE, seq // PROJ_TILE)
    _, dk_m, dv_m, _, rk_m, rv_m, _ = _in_projection(meta_rows, g_attn, w_in_b, tables_m, META_ROWS, 1)

    per_batch = lambda t: t.reshape(b, seq, t.shape[-1])
    diff_out = _diff_attention(per_batch(dq), per_batch(dk), per_batch(dv), dk_m, dv_m,
                               diff_lambda[0], diff_subln_g[0][:, None])
    ret_out = _retention(per_batch(rq), per_batch(rk), per_batch(rv), per_batch(rg), rk_m, rv_m)

    h, xn, pos, col, counts = _out_proj_and_route(
        x2, diff_out.reshape(n_tok, DIFF_WIDTH), ret_out.reshape(n_tok, RET_WIDTH),
        w_out_b, ffn_norm_g[0][None, :], w_router[0], b_router[0][:, None])

    n_tiles = n_tok // ROUTE_TILE
    n_blocks = -(-(n_tok * TOP_K + n_tiles * N_EXPERTS * (SEG_ALIGN - 1) + N_EXPERTS * (MOE_TILE - 1))
                 // MOE_TILE)
    seg_rows = (counts[:, :, 0] + SEG_ALIGN - 1) // SEG_ALIGN * SEG_ALIGN
    seg_local = jnp.cumsum(seg_rows, axis=1) - seg_rows
    blocks_per = (jnp.sum(seg_rows, axis=0) + MOE_TILE - 1) // MOE_TILE
    block_end = jnp.cumsum(blocks_per)
    group_start = (block_end - blocks_per) * MOE_TILE
    seg_global = group_start[None, :] + jnp.cumsum(seg_rows, axis=0) - seg_rows
    flat = lambda t: t.reshape(-1).astype(jnp.int32)
    n_big = seg_rows // BIG_PIECE
    big_local, big_global, big_total = _piece_list(seg_local, seg_global, n_big, 0, BIG_PIECE, MAX_BIG_PIECES)
    small_local, small_global, small_total = _piece_list(
        seg_local, seg_global, seg_rows % BIG_PIECE // SEG_ALIGN, n_big * BIG_PIECE, SEG_ALIGN, MAX_SMALL_PIECES)
    pad_pieces = (jnp.sum(seg_rows, axis=1) - ROUTE_TILE * TOP_K) // SEG_ALIGN
    tables = (flat(big_local), flat(big_global), flat(small_local), flat(small_global),
              flat(jnp.stack([big_total, small_total, pad_pieces], axis=1)))

    n_used = block_end[-1:].astype(jnp.int32)
    all_blocks = jnp.arange(n_blocks, dtype=jnp.int32)
    block_ids = jnp.minimum(all_blocks, n_used[0] - 1)
    block_expert = jnp.minimum(jnp.sum(block_end[None, :] <= block_ids[:, None], axis=1),
                               N_EXPERTS - 1).astype(jnp.int32)
    is_group_end = jnp.any((block_end[None, :] == all_blocks[:, None] + 1) & (blocks_per[None, :] > 0), axis=1)
    fill_flags = (is_group_end | (all_blocks >= n_used[0])).astype(jnp.int32)

    xs = _dispatch(tables, fill_flags, xn, pos, n_blocks)
    ys = _expert_ffn(block_expert, n_used, xs, w_up_b, b_up_s, w_down_b, b_down_s)
    out = _combine(tables, col, h, ys, final_norm_g[None, :])
    return out.reshape(b, seq, D_MODEL)
```

```python
import functools
import math

import jax
import jax.numpy as jnp
import numpy as np
from jax import lax
from jax.experimental import pallas as pl
from jax.experimental.pallas import tpu as pltpu

F32 = jnp.float32
BF16 = jnp.bfloat16

D_MODEL = 1024
N_META = 16
ROPE_THETA = 10000.0
NORM_EPS = 1e-5
DIFF_HEADS = 4
HEAD_DIM = 64
VALUE_DIM = 128
DIFF_WIDTH = DIFF_HEADS * VALUE_DIM
RET_HEADS = 4
RET_WIDTH = RET_HEADS * VALUE_DIM
QK_WIDTH = RET_HEADS * HEAD_DIM
IN_PROJ_WIDTH = 3 * DIFF_WIDTH + 2 * QK_WIDTH + 2 * RET_WIDTH
LAMBDA_INIT = 0.8 - 0.6 * math.exp(-0.3 * 0)
N_EXPERTS = 32
TOP_K = 4
D_FF = D_MODEL
SWIGLU_ALPHA = 1.702
SWIGLU_LIMIT = 7.0

LANES = 128
MXU_WIDTH = 256
META_ROWS = 128
VMEM_LIMIT = 56 * 1024 * 1024

ONES_ROWS = 16
LOG2_E = math.log2(math.e)

PROJ_TILE = 512
ATTN_TILE = 512
HEADS_PER_STEP = 4
RET_CHUNK = 256
RET_BATCH_ROWS = 8
ROUTE_TILE = 512
ROUTE_GROUP = 256
MOE_TILE = 512
SEG_ALIGN = 8
BIG_PIECE = 4 * SEG_ALIGN
COMBINE_ROWS = 128
LOC_ROWS = ROUTE_TILE * TOP_K + N_EXPERTS * SEG_ALIGN
MAX_BIG_PIECES = LOC_ROWS // BIG_PIECE
MAX_SMALL_PIECES = N_EXPERTS * (BIG_PIECE // SEG_ALIGN - 1)

_LOG_GAMMA = [float(v) for v in np.log1p(-np.exp2(-5.0 - np.arange(RET_HEADS, dtype=np.float32)))]


def _params(sem):
    return pltpu.CompilerParams(dimension_semantics=sem, vmem_limit_bytes=VMEM_LIMIT)


def _rope(t, cos, sin_signed):
    lane = lax.broadcasted_iota(jnp.int32, t.shape, 1)
    first_half = (lane % HEAD_DIM) < (HEAD_DIM // 2)
    partner = jnp.where(first_half, pltpu.roll(t, LANES - HEAD_DIM // 2, 1),
                        pltpu.roll(t, HEAD_DIM // 2, 1))
    return t * cos + partner * sin_signed


def _inproj_kernel(x_ref, g_ref, w_ref, cd_ref, sd_ref, cr_ref, sr_ref,
                   dq_ref, dk_ref, dv_ref, rq_ref, rk_ref, rv_ref, rg_ref):
    x = x_ref[...]
    ms = jnp.mean(x * x, axis=-1, keepdims=True)
    hn = (x * lax.rsqrt(ms + NORM_EPS) * g_ref[...]).astype(BF16)
    cd, sd, cr, sr = cd_ref[...], sd_ref[...], cr_ref[...], sr_ref[...]
    scale = HEAD_DIM ** -0.5

    def proj(col, width):
        return jnp.dot(hn, w_ref[:, col:col + width], preferred_element_type=F32)

    col = 0
    for out_ref, width, cos, sin, mul in (
            (dq_ref, DIFF_WIDTH, cd, sd, scale * LOG2_E), (dk_ref, DIFF_WIDTH, cd, sd, None),
            (dv_ref, DIFF_WIDTH, None, None, None),
            (rq_ref, QK_WIDTH, cr, sr, None), (rk_ref, QK_WIDTH, cr, sr, scale),
            (rv_ref, RET_WIDTH, None, None, None), (rg_ref, RET_WIDTH, None, None, None)):
        for c in range(0, width, MXU_WIDTH):
            wide = proj(col + c, MXU_WIDTH)
            for half in range(0, MXU_WIDTH, LANES):
                p = wide[:, half:half + LANES]
                if cos is not None:
                    p = _rope(p, cos, sin)
                if mul is not None:
                    p = p * mul
                out_ref[:, c + half:c + half + LANES] = p.astype(out_ref.dtype)
        col += width


def _in_projection(rows, g, w_bf16, tables, tile, seq_tiles):
    n = rows.shape[0]
    row_spec = lambda w: pl.BlockSpec((tile, w), lambda i: (i, 0))
    table_spec = pl.BlockSpec((tile, LANES), lambda i: (i % seq_tiles, 0))
    widths = (DIFF_WIDTH, DIFF_WIDTH, DIFF_WIDTH, QK_WIDTH, QK_WIDTH, RET_WIDTH, RET_WIDTH)
    return pl.pallas_call(
        _inproj_kernel,
        grid=(n // tile,),
        in_specs=[row_spec(D_MODEL),
                  pl.BlockSpec((1, D_MODEL), lambda i: (0, 0)),
                  pl.BlockSpec((D_MODEL, IN_PROJ_WIDTH), lambda i: (0, 0)),
                  table_spec, table_spec, table_spec, table_spec],
        out_specs=[row_spec(w) for w in widths],
        out_shape=[jax.ShapeDtypeStruct((n, w), BF16) for w in widths],
        compiler_params=_params(("parallel",)),
        name="in_projection",
    )(rows, g, w_bf16, *tables)


def _rope_tables(pos, inv_freq):
    ang = pos[:, None] * inv_freq[None, :]
    cos = jnp.tile(jnp.cos(ang), (1, LANES // (HEAD_DIM // 2)))
    sin = jnp.sin(ang)
    sin_signed = jnp.tile(jnp.concatenate([-sin, sin], axis=1), (1, LANES // HEAD_DIM))
    return cos, sin_signed


def _transpose_bf16(t):
    return t.astype(F32).T.astype(BF16)


def _with_ones_rows(vt):
    return jnp.concatenate([vt, jnp.ones((ONES_ROWS, vt.shape[1]), vt.dtype)], axis=0)


def _attn_kernel(q_ref, k_ref, v_ref, km_ref, vm_ref, lp_ref, sg_ref, o_ref,
                 vt_sc, m_sc, acc_sc, sa_sc, sb_sc, *, tile, n_chunks):
    i = pl.program_id(2)
    heads = range(HEADS_PER_STEP)
    lanes = lambda hh: slice(hh * LANES, (hh + 1) * LANES)

    @pl.when(i == 0)
    def _():
        def transpose_chunk(j, carry):
            start = pl.multiple_of(j * tile, tile)
            for hh in heads:
                vt_sc[hh, j] = _with_ones_rows(_transpose_bf16(v_ref[pl.ds(start, tile), lanes(hh)]))
            return carry
        lax.fori_loop(0, n_chunks, transpose_chunk, 0)

    qcat = []
    for hh in heads:
        qt = q_ref[:, lanes(hh)].astype(F32).T
        sub = lax.broadcasted_iota(jnp.int32, qt.shape, 0)
        qcat.append(jnp.concatenate([jnp.where(sub < HEAD_DIM, qt, 0.0).astype(BF16),
                                     jnp.where(sub >= HEAD_DIM, qt, 0.0).astype(BF16)], axis=1))

    def scores(hh, j):
        start = pl.multiple_of(j * tile, tile)
        return jnp.dot(k_ref[pl.ds(start, tile), lanes(hh)], qcat[hh], preferred_element_type=F32)

    for hh in heads:
        sa_sc[hh] = scores(hh, 0)
        m_sc[hh] = jnp.full(m_sc.shape[1:], -jnp.inf, F32)
        acc_sc[hh] = jnp.zeros(acc_sc.shape[1:], F32)

    meta_row = lax.broadcasted_iota(jnp.int32, (META_ROWS, 2 * tile), 0) >= META_ROWS - N_META
    s_meta = [jnp.where(meta_row, jnp.dot(km_ref[:, lanes(hh)], qcat[hh], preferred_element_type=F32), -jnp.inf)
              for hh in heads]
    vmt = [_with_ones_rows(_transpose_bf16(vm_ref[:, lanes(hh)])) for hh in heads]

    def absorb(hh, s_ref, j, last):
        s = s_ref[hh]
        m_old = m_sc[hh]
        if last:
            key = lax.broadcasted_iota(jnp.int32, s.shape, 0)
            lane = lax.broadcasted_iota(jnp.int32, s.shape, 1)
            s = jnp.where(key <= jnp.where(lane >= tile, lane - tile, lane), s, -jnp.inf)
            m_old = jnp.maximum(m_old, jnp.max(s_meta[hh], axis=0, keepdims=True))
        m_new = jnp.maximum(m_old, jnp.max(s, axis=0, keepdims=True))
        update = jnp.dot(vt_sc[hh, j], jnp.exp2(s - m_new).astype(BF16), preferred_element_type=F32)
        if last:
            update = update + jnp.dot(vmt[hh], jnp.exp2(s_meta[hh] - m_new).astype(BF16),
                                      preferred_element_type=F32)
        acc_sc[hh] = jnp.exp2(m_sc[hh] - m_new) * acc_sc[hh] + update
        m_sc[hh] = m_new

    def pair(t, carry):
        for hh in heads:
            sb_sc[hh] = scores(hh, 2 * t + 1)
            absorb(hh, sa_sc, 2 * t, False)
        for hh in heads:
            sa_sc[hh] = scores(hh, 2 * t + 2)
            absorb(hh, sb_sc, 2 * t + 1, False)
        return carry

    lax.fori_loop(0, i // 2, pair, 0)

    @pl.when(i % 2 == 0)
    def _():
        for hh in heads:
            absorb(hh, sa_sc, i, True)

    @pl.when(i % 2 == 1)
    def _():
        for hh in heads:
            sb_sc[hh] = scores(hh, i)
            absorb(hh, sa_sc, i - 1, False)
        for hh in heads:
            absorb(hh, sb_sc, i, True)

    lp = lp_ref[...]
    lam = (jnp.exp(jnp.sum(lp[0:1] * lp[1:2], axis=-1, keepdims=True))
           - jnp.exp(jnp.sum(lp[2:3] * lp[3:4], axis=-1, keepdims=True)) + LAMBDA_INIT)
    for hh in heads:
        num, den = acc_sc[hh, 0:VALUE_DIM, :], acc_sc[hh, VALUE_DIM:VALUE_DIM + 1, :]
        o = num[:, :tile] / den[:, :tile] - lam * (num[:, tile:] / den[:, tile:])
        o = o * lax.rsqrt(jnp.mean(o * o, axis=0, keepdims=True) + NORM_EPS)
        o_ref[:, lanes(hh)] = (o * sg_ref[...] * (1.0 - LAMBDA_INIT)).T.astype(o_ref.dtype)


def _diff_attention(dq, dk, dv, km, vm, lam_params, subln_g):
    b, seq, _ = dq.shape
    tile = ATTN_TILE
    n_chunks = seq // tile
    width = HEADS_PER_STEP * LANES
    head_rows = pl.BlockSpec((None, tile, width), lambda bi, h, i: (bi, i, h))
    head_seq = pl.BlockSpec((None, seq, width), lambda bi, h, i: (bi, 0, h))
    head_meta = pl.BlockSpec((META_ROWS, width), lambda bi, h, i: (0, h))
    per_head = lambda *shape: pltpu.VMEM((HEADS_PER_STEP,) + shape, F32)
    return pl.pallas_call(
        functools.partial(_attn_kernel, tile=tile, n_chunks=n_chunks),
        grid=(b, DIFF_HEADS // HEADS_PER_STEP, n_chunks),
        in_specs=[head_rows, head_seq, head_seq, head_meta, head_meta,
                  pl.BlockSpec((4, HEAD_DIM), lambda bi, h, i: (0, 0)),
                  pl.BlockSpec((VALUE_DIM, 1), lambda bi, h, i: (0, 0))],
        out_specs=head_rows,
        out_shape=jax.ShapeDtypeStruct((b, seq, DIFF_WIDTH), BF16),
        scratch_shapes=[pltpu.VMEM((HEADS_PER_STEP, n_chunks, VALUE_DIM + ONES_ROWS, tile), BF16),
                        per_head(1, 2 * tile), per_head(VALUE_DIM + ONES_ROWS, 2 * tile),
                        per_head(tile, 2 * tile), per_head(tile, 2 * tile)],
        compiler_params=_params(("parallel", "parallel", "arbitrary")),
        name="diff_attention",
    )(dq, dk, dv, km, vm, lam_params, subln_g)


def _decay_col(log_gamma, n, offset_fn):
    pos = lax.broadcasted_iota(jnp.int32, (n, 1), 0).astype(F32)
    return jnp.exp(log_gamma * offset_fn(pos))


def _ret_kernel(q_ref, k_ref, v_ref, g_ref, km_ref, vm_ref, o_ref, state_sc, decay_sc, *, chunk):
    c = pl.program_id(1)

    @pl.when(c == 0)
    def _():
        km = km_ref[...].astype(F32)
        for h in range(RET_HEADS):
            pair = h // 2
            kt = km[:, pair * LANES:(pair + 1) * LANES].T.astype(BF16)
            kd = _decay_col(_LOG_GAMMA[h], META_ROWS, lambda p: (META_ROWS - 1) - p)
            vd = (vm_ref[:, h * VALUE_DIM:(h + 1) * VALUE_DIM].astype(F32) * kd).astype(BF16)
            meta_state = jnp.dot(kt, vd, preferred_element_type=F32)
            for bb in range(RET_BATCH_ROWS):
                state_sc[bb, h] = meta_state
        row = lax.broadcasted_iota(jnp.int32, (chunk, chunk), 0)
        col = lax.broadcasted_iota(jnp.int32, (chunk, chunk), 1)
        rel = (row - col).astype(F32)
        for h in range(RET_HEADS):
            decay_sc[h] = jnp.where(rel >= 0, jnp.exp(_LOG_GAMMA[h] * jnp.maximum(rel, 0.0)), 0.0)

    lane = lax.broadcasted_iota(jnp.int32, (chunk, LANES), 1)
    for bb, pair in [(bb, pair) for pair in range(RET_HEADS // 2) for bb in range(RET_BATCH_ROWS)]:
        q_pair = q_ref[bb, :, pair * LANES:(pair + 1) * LANES]
        k_pair = k_ref[bb, :, pair * LANES:(pair + 1) * LANES]
        kt = k_pair.astype(F32).T.astype(BF16)
        for h in (2 * pair, 2 * pair + 1):
            lg = _LOG_GAMMA[h]
            own = (lane < HEAD_DIM) if h % 2 == 0 else (lane >= HEAD_DIM)
            qm = jnp.where(own, q_pair, jnp.zeros_like(q_pair))
            v = v_ref[bb, :, h * VALUE_DIM:(h + 1) * VALUE_DIM]
            s = lax.dot_general(qm, k_pair, (((1,), (1,)), ((), ())), preferred_element_type=F32)
            y = jnp.dot((s * decay_sc[h]).astype(BF16), v, preferred_element_type=F32)
            qd = _decay_col(lg, chunk, lambda p: p + 1.0)
            state = state_sc[bb, h]
            y = y + jnp.dot((qm.astype(F32) * qd).astype(BF16), state.astype(BF16),
                            preferred_element_type=F32)
            kd = _decay_col(lg, chunk, lambda p: (chunk - 1) - p)
            vd = (v.astype(F32) * kd).astype(BF16)
            state_sc[bb, h] = math.exp(lg * chunk) * state + jnp.dot(kt, vd, preferred_element_type=F32)
            y = y * lax.rsqrt(jnp.mean(y * y, axis=-1, keepdims=True) + NORM_EPS)
            gate = g_ref[bb, :, h * VALUE_DIM:(h + 1) * VALUE_DIM].astype(F32)
            o_ref[bb, :, h * VALUE_DIM:(h + 1) * VALUE_DIM] = (y * gate * jax.nn.sigmoid(gate)).astype(o_ref.dtype)


def _retention(rq, rk, rv, rg, km, vm):
    b, seq, _ = rq.shape
    chunk = RET_CHUNK
    assert b % RET_BATCH_ROWS == 0
    rows = lambda w: pl.BlockSpec((RET_BATCH_ROWS, chunk, w), lambda bi, c: (bi, c, 0))
    return pl.pallas_call(
        functools.partial(_ret_kernel, chunk=chunk),
        grid=(b // RET_BATCH_ROWS, seq // chunk),
        in_specs=[rows(QK_WIDTH), rows(QK_WIDTH), rows(RET_WIDTH), rows(RET_WIDTH),
                  pl.BlockSpec((META_ROWS, QK_WIDTH), lambda bi, c: (0, 0)),
                  pl.BlockSpec((META_ROWS, RET_WIDTH), lambda bi, c: (0, 0))],
        out_specs=rows(RET_WIDTH),
        out_shape=jax.ShapeDtypeStruct((b, seq, RET_WIDTH), BF16),
        scratch_shapes=[pltpu.VMEM((RET_BATCH_ROWS, RET_HEADS, LANES, VALUE_DIM), F32),
                        pltpu.VMEM((RET_HEADS, chunk, chunk), F32)],
        compiler_params=_params(("parallel", "arbitrary")),
        name="retention",
    )(rq, rk, rv, rg, km, vm)


def _route_kernel(x_ref, d_ref, r_ref, wo_ref, g_ref, wrh_ref, wrl_ref, brt_ref,
                  h_ref, xn_ref, pos_ref, col_ref, cnt_ref, *, tile):
    group_hots, group_gates = [], []
    for r0 in range(0, tile, ROUTE_GROUP):
        rows = slice(r0, r0 + ROUTE_GROUP)
        h = (x_ref[rows, :]
             + jnp.dot(d_ref[rows, :], wo_ref[0:DIFF_WIDTH, :], preferred_element_type=F32)
             + jnp.dot(r_ref[rows, :], wo_ref[DIFF_WIDTH:, :], preferred_element_type=F32))
        h_ref[rows, :] = h
        xn = h * lax.rsqrt(jnp.mean(h * h, axis=-1, keepdims=True) + NORM_EPS) * g_ref[...]
        xn_hi = xn.astype(BF16)
        xn_ref[rows, :] = xn_hi
        xn_lo = (xn - xn_hi.astype(F32)).astype(BF16)
        logits = (jnp.dot(xn_hi, wrh_ref[...], preferred_element_type=F32)
                  + jnp.dot(xn_hi, wrl_ref[...], preferred_element_type=F32)
                  + jnp.dot(xn_lo, wrh_ref[...], preferred_element_type=F32))
        logits = logits.T[0:N_EXPERTS, :] + brt_ref[...]
        expert = lax.broadcasted_iota(jnp.int32, logits.shape, 0)
        work = logits
        vals, hots = [], []
        for k in range(TOP_K):
            v = jnp.max(work, axis=0, keepdims=True)
            idx = jnp.min(jnp.where(work == v, expert, N_EXPERTS), axis=0, keepdims=True)
            hot = expert == idx
            work = jnp.where(hot, -jnp.inf, work)
            vals.append(v)
            hots.append(hot.astype(F32))
        exps = [jnp.exp(v - vals[0]) for v in vals]
        denom = exps[0] + exps[1] + exps[2] + exps[3]
        group_hots.append(hots)
        group_gates.append([e / denom for e in exps])
    hots = [jnp.concatenate([g[k] for g in group_hots], axis=1) for k in range(TOP_K)]
    gates = [jnp.concatenate([g[k] for g in group_gates], axis=1) for k in range(TOP_K)]

    chosen = hots[0] + hots[1] + hots[2] + hots[3]
    c = lax.broadcasted_iota(jnp.int32, (tile, tile), 0)
    r = lax.broadcasted_iota(jnp.int32, (tile, tile), 1)
    earlier = (c < r).astype(BF16)
    before = jnp.dot(chosen.astype(BF16), earlier, preferred_element_type=F32)
    cnt = jnp.sum(chosen, axis=1, keepdims=True)
    cnt_pad = jnp.ceil(cnt / SEG_ALIGN) * SEG_ALIGN
    er = lax.broadcasted_iota(jnp.int32, (N_EXPERTS, N_EXPERTS), 0)
    ec = lax.broadcasted_iota(jnp.int32, (N_EXPERTS, N_EXPERTS), 1)
    seg_start = jnp.dot((ec < er).astype(F32), jnp.broadcast_to(cnt_pad, (N_EXPERTS, LANES)),
                        preferred_element_type=F32, precision=lax.Precision.HIGHEST)[:, 0:1]
    base = seg_start + before
    pos = [jnp.sum(hot * base, axis=0, keepdims=True) for hot in hots]
    for k in range(TOP_K):
        pos_ref[k:k + 1, :] = pos[k].astype(jnp.int32)
    cnt_ref[...] = jnp.broadcast_to(cnt, (N_EXPERTS, LANES)).astype(jnp.int32)
    stacked = jnp.concatenate(pos + gates + [jnp.zeros((LANES - 2 * TOP_K, tile), F32)], axis=0)
    col_ref[...] = stacked.T


def _out_proj_and_route(x, diff, ret, wo, g, wr, brt):
    n = x.shape[0]
    tile = ROUTE_TILE
    rows = lambda w: pl.BlockSpec((tile, w), lambda i: (i, 0))
    full = lambda a, b: pl.BlockSpec((a, b), lambda i: (0, 0))
    wr = jnp.pad(wr, ((0, 0), (0, LANES - N_EXPERTS)))
    wr_hi = wr.astype(BF16)
    wr_lo = (wr - wr_hi.astype(F32)).astype(BF16)
    return pl.pallas_call(
        functools.partial(_route_kernel, tile=tile),
        grid=(n // tile,),
        in_specs=[rows(D_MODEL), rows(DIFF_WIDTH), rows(RET_WIDTH), full(D_MODEL, D_MODEL),
                  full(1, D_MODEL), full(D_MODEL, LANES), full(D_MODEL, LANES), full(N_EXPERTS, 1)],
        out_specs=[rows(D_MODEL), rows(D_MODEL),
                   pl.BlockSpec((TOP_K, tile), lambda i: (0, i)), rows(LANES),
                   pl.BlockSpec((None, N_EXPERTS, LANES), lambda i: (i, 0, 0))],
        out_shape=[jax.ShapeDtypeStruct((n, D_MODEL), F32), jax.ShapeDtypeStruct((n, D_MODEL), BF16),
                   jax.ShapeDtypeStruct((TOP_K, n), jnp.int32), jax.ShapeDtypeStruct((n, LANES), F32),
                   jax.ShapeDtypeStruct((n // tile, N_EXPERTS, LANES), jnp.int32)],
        compiler_params=_params(("parallel",)),
        name="out_proj_route",
    )(x, diff, ret, wo, g, wr_hi, wr_lo, brt)


def _split_gate_linear(w_ref, o_ref):
    group = 2 * LANES
    src = lax.broadcasted_iota(jnp.int32, (group, group), 0)
    dst = lax.broadcasted_iota(jnp.int32, (group, group), 1)
    select = (src == jnp.where(dst < LANES, 2 * dst, 2 * (dst - LANES) + 1)).astype(BF16)
    for c in range(2 * D_FF // group):
        part = jnp.dot(w_ref[:, c * group:(c + 1) * group].astype(BF16), select,
                       preferred_element_type=F32).astype(BF16)
        o_ref[:, c * LANES:(c + 1) * LANES] = part[:, :LANES]
        o_ref[:, D_FF + c * LANES:D_FF + (c + 1) * LANES] = part[:, LANES:]


def _segment_copy(local_buf, local_row, global_hbm, global_row, rows, sem, to_global):
    local_piece = local_buf.at[pl.ds(local_row, rows), :]
    global_piece = global_hbm.at[pl.ds(global_row, rows), :]
    if to_global:
        return pltpu.make_async_copy(local_piece, global_piece, sem)
    return pltpu.make_async_copy(global_piece, local_piece, sem)


def _piece_list(seg_local, seg_global, n_pieces, skip_rows, rows_per_piece, max_pieces):
    cum = jnp.cumsum(n_pieces, axis=1)
    j = jnp.arange(max_pieces, dtype=jnp.int32)
    owner = jnp.sum(cum[:, None, :] <= j[None, :, None], axis=2)
    onehot = owner[:, :, None] == jnp.arange(N_EXPERTS, dtype=jnp.int32)[None, None, :]
    pick = lambda t: jnp.sum(jnp.where(onehot, t[:, None, :], 0), axis=2)
    within = (j[None, :] - pick(cum - n_pieces)) * rows_per_piece
    return pick(seg_local + skip_rows) + within, pick(seg_global + skip_rows) + within, cum[:, -1]


def _start_segments(tables, tile_index, local_buf, global_hbm, sem, to_global):
    big_local, big_global, small_local, small_global, counts = tables

    def run(local_ref, global_ref, count, rows, stride):
        first = tile_index * stride

        def per_piece(p, carry):
            lo = pl.multiple_of(local_ref[first + p], SEG_ALIGN)
            gl = pl.multiple_of(global_ref[first + p], SEG_ALIGN)
            _segment_copy(local_buf, lo, global_hbm, gl, rows, sem, to_global).start()
            return carry

        lax.fori_loop(0, count, per_piece, 0)

    run(big_local, big_global, counts[3 * tile_index], BIG_PIECE, MAX_BIG_PIECES)
    run(small_local, small_global, counts[3 * tile_index + 1], SEG_ALIGN, MAX_SMALL_PIECES)


def _wait_segments(tables, tile_index, local_buf, global_hbm, sem, to_global):
    counts = tables[4]
    _segment_copy(local_buf, 0, global_hbm, 0, ROUTE_TILE * TOP_K, sem, to_global).wait()

    def per_pad(p, carry):
        _segment_copy(local_buf, 0, global_hbm, 0, SEG_ALIGN, sem, to_global).wait()
        return carry

    lax.fori_loop(0, counts[3 * tile_index + 2], per_pad, 0)


def _dispatch_kernel(big_local, big_global, small_local, small_global, counts, fill_ref, xn_ref, pos_ref,
                     xs_out, loc_buf, zero_buf, seg_sem, fill_sem, *, n_blocks, n_tiles):
    i = pl.program_id(0)
    slot = i % 2

    @pl.when(i == 0)
    def _():
        zero_buf[...] = jnp.zeros_like(zero_buf)

        def fill_copy(j):
            row = pl.multiple_of(j * MOE_TILE, MOE_TILE)
            return pltpu.make_async_copy(zero_buf, xs_out.at[pl.ds(row, MOE_TILE), :], fill_sem)

        def start(j, carry):
            @pl.when(fill_ref[j] > 0)
            def _():
                fill_copy(j).start()
            return carry

        def wait(j, carry):
            @pl.when(fill_ref[j] > 0)
            def _():
                fill_copy(j).wait()
            return carry

        lax.fori_loop(0, n_blocks, start, 0)
        lax.fori_loop(0, n_blocks, wait, 0)

    pos = pos_ref[...]
    row = lax.broadcasted_iota(jnp.int32, (LOC_ROWS, pos.shape[1]), 0)
    select = jnp.zeros(row.shape, F32)
    for k in range(TOP_K):
        select = jnp.where(row == pos[k:k + 1], 1.0, select)
    loc_buf[slot] = jnp.dot(select.astype(BF16), xn_ref[...], preferred_element_type=F32)

    tables = (big_local, big_global, small_local, small_global, counts)
    _start_segments(tables, i, loc_buf.at[slot], xs_out, seg_sem.at[slot], True)

    @pl.when(i > 0)
    def _():
        _wait_segments(tables, i - 1, loc_buf.at[1 - slot], xs_out, seg_sem.at[1 - slot], True)

    @pl.when(i == n_tiles - 1)
    def _():
        _wait_segments(tables, i, loc_buf.at[slot], xs_out, seg_sem.at[slot], True)


def _dispatch(tables, fill_flags, xn, pos, n_blocks):
    n = xn.shape[0]
    tile = ROUTE_TILE
    return pl.pallas_call(
        functools.partial(_dispatch_kernel, n_blocks=n_blocks, n_tiles=n // tile),
        grid_spec=pltpu.PrefetchScalarGridSpec(
            num_scalar_prefetch=6,
            grid=(n // tile,),
            in_specs=[pl.BlockSpec((tile, D_MODEL), lambda i, *_: (i, 0)),
                      pl.BlockSpec((TOP_K, tile), lambda i, *_: (0, i))],
            out_specs=pl.BlockSpec(memory_space=pl.ANY),
            scratch_shapes=[pltpu.VMEM((2, LOC_ROWS, D_MODEL), F32),
                            pltpu.VMEM((MOE_TILE, D_MODEL), F32),
                            pltpu.SemaphoreType.DMA((2,)), pltpu.SemaphoreType.DMA]),
        out_shape=jax.ShapeDtypeStruct((n_blocks * MOE_TILE, D_MODEL), F32),
        compiler_params=_params(("arbitrary",)),
        name="moe_dispatch",
    )(*tables, fill_flags, xn, pos)


def _expert_kernel(be_ref, nu_ref, xs_ref, wu_ref, bu_ref, wd_ref, bd_ref, ys_ref, wu_sc, wd_sc):
    j = pl.program_id(0)

    @pl.when((j == 0) | (be_ref[j] != be_ref[jnp.maximum(j - 1, 0)]))
    def _():
        _split_gate_linear(wu_ref, wu_sc)
        wd_sc[...] = wd_ref[...].astype(BF16)

    @pl.when(j < nu_ref[0])
    def _():
        x = xs_ref[...].astype(BF16)
        glu = jnp.dot(x, wu_sc[:, 0:D_FF], preferred_element_type=F32) + bu_ref[:, 0:D_FF]
        lin = jnp.dot(x, wu_sc[:, D_FF:], preferred_element_type=F32) + bu_ref[:, D_FF:]
        glu = jnp.minimum(glu, SWIGLU_LIMIT)
        lin = jnp.clip(lin, -SWIGLU_LIMIT, SWIGLU_LIMIT)
        act = glu * jax.nn.sigmoid(SWIGLU_ALPHA * glu) * (lin + 1.0)
        ys_ref[...] = jnp.dot(act.astype(BF16), wd_sc[...], preferred_element_type=F32) + bd_ref[...]

    @pl.when(j >= nu_ref[0])
    def _():
        ys_ref[...] = jnp.zeros_like(ys_ref)


def _expert_ffn(block_expert, n_used, xs, wu, bu, wd, bd):
    n_rows = xs.shape[0]
    tile = MOE_TILE
    rows = pl.BlockSpec((tile, D_MODEL), lambda j, be, nu: (jnp.minimum(j, nu[0] - 1), 0))
    per_expert = lambda a, b: pl.BlockSpec((None, a, b), lambda j, be, nu: (be[j], 0, 0))
    return pl.pallas_call(
        _expert_kernel,
        grid_spec=pltpu.PrefetchScalarGridSpec(
            num_scalar_prefetch=2,
            grid=(n_rows // tile,),
            in_specs=[rows, per_expert(D_MODEL, 2 * D_FF), per_expert(1, 2 * D_FF),
                      per_expert(D_FF, D_MODEL), per_expert(1, D_MODEL)],
            out_specs=pl.BlockSpec((tile, D_MODEL), lambda j, be, nu: (j, 0)),
            scratch_shapes=[pltpu.VMEM((D_MODEL, 2 * D_FF), BF16), pltpu.VMEM((D_FF, D_MODEL), BF16)]),
        out_shape=jax.ShapeDtypeStruct((n_rows, D_MODEL), F32),
        compiler_params=_params(("arbitrary",)),
        name="expert_ffn",
    )(block_expert, n_used, xs, wu, bu, wd, bd)


def _combine_kernel(big_local, big_global, small_local, small_global, counts, col_ref, h_ref, ys_hbm, g_ref,
                    o_ref, loc_buf, seg_sem, *, n_tiles):
    i = pl.program_id(0)
    slot = i % 2
    tables = (big_local, big_global, small_local, small_global, counts)

    def fetch(tile_index, buf_slot):
        _start_segments(tables, tile_index, loc_buf.at[buf_slot], ys_hbm, seg_sem.at[buf_slot], False)

    @pl.when(i == 0)
    def _():
        loc_buf[...] = jnp.zeros_like(loc_buf)
        fetch(0, 0)

    @pl.when(i + 1 < n_tiles)
    def _():
        fetch(i + 1, 1 - slot)

    _wait_segments(tables, i, loc_buf.at[slot], ys_hbm, seg_sem.at[slot], False)

    rows = loc_buf[slot].astype(BF16)
    lane = lax.broadcasted_iota(jnp.int32, (COMBINE_ROWS, LOC_ROWS), 1)
    for r in range(0, col_ref.shape[0], COMBINE_ROWS):
        col = col_ref[r:r + COMBINE_ROWS, :]
        weights = jnp.zeros((COMBINE_ROWS, LOC_ROWS), F32)
        for k in range(TOP_K):
            weights = jnp.where(lane == col[:, k:k + 1].astype(jnp.int32),
                                col[:, TOP_K + k:TOP_K + k + 1], weights)
        h = h_ref[r:r + COMBINE_ROWS, :] + jnp.dot(weights.astype(BF16), rows, preferred_element_type=F32)
        o_ref[r:r + COMBINE_ROWS, :] = (h * lax.rsqrt(jnp.mean(h * h, axis=-1, keepdims=True) + NORM_EPS)
                                        * g_ref[...])


def _combine(tables, col, h, ys, g):
    n = h.shape[0]
    tile = ROUTE_TILE
    return pl.pallas_call(
        functools.partial(_combine_kernel, n_tiles=n // tile),
        grid_spec=pltpu.PrefetchScalarGridSpec(
            num_scalar_prefetch=5,
            grid=(n // tile,),
            in_specs=[pl.BlockSpec((tile, LANES), lambda i, *_: (i, 0)),
                      pl.BlockSpec((tile, D_MODEL), lambda i, *_: (i, 0)),
                      pl.BlockSpec(memory_space=pl.ANY),
                      pl.BlockSpec((1, D_MODEL), lambda i, *_: (0, 0))],
            out_specs=pl.BlockSpec((tile, D_MODEL), lambda i, *_: (i, 0)),
            scratch_shapes=[pltpu.VMEM((2, LOC_ROWS, D_MODEL), F32), pltpu.SemaphoreType.DMA((2,))]),
        out_shape=jax.ShapeDtypeStruct((n, D_MODEL), F32),
        compiler_params=_params(("arbitrary",)),
        name="moe_combine",
    )(*tables, col, h, ys, g)


def kernel(x, meta_tokens, attn_norm_g, w_in, diff_lambda, diff_subln_g, w_out, ffn_norm_g,
           w_router, b_router, w_up, b_up, w_down, b_down, final_norm_g):
    b, seq, d = x.shape
    assert d == D_MODEL and seq % PROJ_TILE == 0 and seq % RET_CHUNK == 0 and seq % ATTN_TILE == 0
    assert w_in.shape[0] == 1, "one layer"
    n_tok = b * seq
    assert n_tok % ROUTE_TILE == 0

    w_in_b = w_in[0].astype(BF16)
    w_out_b = w_out[0].astype(BF16)
    w_up_b = w_up[0]
    b_up_s = jnp.concatenate([b_up[0][..., 0::2], b_up[0][..., 1::2]], axis=-1)[:, None, :]
    w_down_b = w_down[0]
    b_down_s = b_down[0][:, None, :]

    diff_inv_freq = ROPE_THETA ** (-jnp.arange(0, HEAD_DIM, 2, dtype=F32) / HEAD_DIM)
    ret_inv_freq = ROPE_THETA ** (-jnp.linspace(0.0, 1.0, HEAD_DIM // 2, dtype=F32))
    pos_x = jnp.arange(seq, dtype=F32) + N_META
    pos_m = jnp.arange(META_ROWS, dtype=F32) - (META_ROWS - N_META)
    tables_x = _rope_tables(pos_x, diff_inv_freq) + _rope_tables(pos_x, ret_inv_freq)
    tables_m = _rope_tables(pos_m, diff_inv_freq) + _rope_tables(pos_m, ret_inv_freq)

    g_attn = attn_norm_g[0][None, :]
    x2 = x.reshape(n_tok, D_MODEL)
    meta_rows = jnp.concatenate(
        [jnp.zeros((META_ROWS - N_META, D_MODEL), x.dtype), meta_tokens.astype(x.dtype)], axis=0)
    dq, dk, dv, rq, rk, rv, rg = _in_projection(x2, g_attn, w_in_b, tables_x, PROJ_TILE, seq // PROJ_TILE)
    _, dk_m, dv_m, _, rk_m, rv_m, _ = _in_projection(meta_rows, g_attn, w_in_b, tables_m, META_ROWS, 1)

    per_batch = lambda t: t.reshape(b, seq, t.shape[-1])
    diff_out = _diff_attention(per_batch(dq), per_batch(dk), per_batch(dv), dk_m, dv_m,
                               diff_lambda[0], diff_subln_g[0][:, None])
    ret_out = _retention(per_batch(rq), per_batch(rk), per_batch(rv), per_batch(rg), rk_m, rv_m)

    h, xn, pos, col, counts = _out_proj_and_route(
        x2, diff_out.reshape(n_tok, DIFF_WIDTH), ret_out.reshape(n_tok, RET_WIDTH),
        w_out_b, ffn_norm_g[0][None, :], w_router[0], b_router[0][:, None])

    n_tiles = n_tok // ROUTE_TILE
    n_blocks = -(-(n_tok * TOP_K + n_tiles * N_EXPERTS * (SEG_ALIGN - 1) + N_EXPERTS * (MOE_TILE - 1))
                 // MOE_TILE)
    seg_rows = (counts[:, :, 0] + SEG_ALIGN - 1) // SEG_ALIGN * SEG_ALIGN
    seg_local = jnp.cumsum(seg_rows, axis=1) - seg_rows
    blocks_per = (jnp.sum(seg_rows, axis=0) + MOE_TILE - 1) // MOE_TILE
    block_end = jnp.cumsum(blocks_per)
    group_start = (block_end - blocks_per) * MOE_TILE
    seg_global = group_start[None, :] + jnp.cumsum(seg_rows, axis=0) - seg_rows
    flat = lambda t: t.reshape(-1).astype(jnp.int32)
    n_big = seg_rows // BIG_PIECE
    big_local, big_global, big_total = _piece_list(seg_local, seg_global, n_big, 0, BIG_PIECE, MAX_BIG_PIECES)
    small_local, small_global, small_total = _piece_list(
        seg_local, seg_global, seg_rows % BIG_PIECE // SEG_ALIGN, n_big * BIG_PIECE, SEG_ALIGN, MAX_SMALL_PIECES)
    pad_pieces = (jnp.sum(seg_rows, axis=1) - ROUTE_TILE * TOP_K) // SEG_ALIGN
    tables = (flat(big_local), flat(big_global), flat(small_local), flat(small_global),
              flat(jnp.stack([big_total, small_total, pad_pieces], axis=1)))

    n_used = block_end[-1:].astype(jnp.int32)
    all_blocks = jnp.arange(n_blocks, dtype=jnp.int32)
    block_ids = jnp.minimum(all_blocks, n_used[0] - 1)
    block_expert = jnp.minimum(jnp.sum(block_end[None, :] <= block_ids[:, None], axis=1),
                               N_EXPERTS - 1).astype(jnp.int32)
    is_group_end = jnp.any((block_end[None, :] == all_blocks[:, None] + 1) & (blocks_per[None, :] > 0), axis=1)
    fill_flags = (is_group_end | (all_blocks >= n_used[0])).astype(jnp.int32)

    xs = _dispatch(tables, fill_flags, xn, pos, n_blocks)
    ys = _expert_ffn(block_expert, n_used, xs, w_up_b, b_up_s, w_down_b, b_down_s)
    out = _combine(tables, col, h, ys, final_norm_g[None, :])
    return out.reshape(b, seq, D_MODEL)
```

```python
import functools
import math

import jax
import jax.numpy as jnp
import numpy as np
from jax import lax
from jax.experimental import pallas as pl
from jax.experimental.pallas import tpu as pltpu

F32 = jnp.float32
BF16 = jnp.bfloat16

D_MODEL = 1024
N_META = 16
ROPE_THETA = 10000.0
NORM_EPS = 1e-5
DIFF_HEADS = 4
HEAD_DIM = 64
VALUE_DIM = 128
DIFF_WIDTH = DIFF_HEADS * VALUE_DIM
RET_HEADS = 4
RET_WIDTH = RET_HEADS * VALUE_DIM
QK_WIDTH = RET_HEADS * HEAD_DIM
IN_PROJ_WIDTH = 3 * DIFF_WIDTH + 2 * QK_WIDTH + 2 * RET_WIDTH
LAMBDA_INIT = 0.8 - 0.6 * math.exp(-0.3 * 0)
N_EXPERTS = 32
TOP_K = 4
D_FF = D_MODEL
SWIGLU_ALPHA = 1.702
SWIGLU_LIMIT = 7.0

LANES = 128
MXU_WIDTH = 256
META_ROWS = 128
VMEM_LIMIT = 56 * 1024 * 1024

ONES_ROWS = 16
LOG2_E = math.log2(math.e)

PROJ_TILE = 1024
ATTN_TILE = 512
HEADS_PER_STEP = 4
RET_CHUNK = 256
RET_BATCH_ROWS = 8
ROUTE_TILE = 512
ROUTE_GROUP = 256
MOE_TILE = 512
SEG_ALIGN = 8
BIG_PIECE = 4 * SEG_ALIGN
COMBINE_ROWS = 128
LOC_ROWS = ROUTE_TILE * TOP_K + N_EXPERTS * SEG_ALIGN
MAX_BIG_PIECES = LOC_ROWS // BIG_PIECE
MAX_SMALL_PIECES = N_EXPERTS * (BIG_PIECE // SEG_ALIGN - 1)

_LOG_GAMMA = [float(v) for v in np.log1p(-np.exp2(-5.0 - np.arange(RET_HEADS, dtype=np.float32)))]


def _params(sem):
    return pltpu.CompilerParams(dimension_semantics=sem, vmem_limit_bytes=VMEM_LIMIT)


def _rope(t, cos, sin_signed):
    lane = lax.broadcasted_iota(jnp.int32, t.shape, 1)
    first_half = (lane % HEAD_DIM) < (HEAD_DIM // 2)
    partner = jnp.where(first_half, pltpu.roll(t, LANES - HEAD_DIM // 2, 1),
                        pltpu.roll(t, HEAD_DIM // 2, 1))
    return t * cos + partner * sin_signed


def _inproj_kernel(x_ref, g_ref, w_ref, cd_ref, sd_ref, cr_ref, sr_ref,
                   dq_ref, dk_ref, dv_ref, rq_ref, rk_ref, rv_ref, rg_ref):
    x = x_ref[...]
    ms = jnp.mean(x * x, axis=-1, keepdims=True)
    hn = (x * lax.rsqrt(ms + NORM_EPS) * g_ref[...]).astype(BF16)
    cd, sd, cr, sr = cd_ref[...], sd_ref[...], cr_ref[...], sr_ref[...]
    scale = HEAD_DIM ** -0.5

    def proj(col, width):
        return jnp.dot(hn, w_ref[:, col:col + width], preferred_element_type=F32)

    col = 0
    for out_ref, width, cos, sin, mul in (
            (dq_ref, DIFF_WIDTH, cd, sd, scale * LOG2_E), (dk_ref, DIFF_WIDTH, cd, sd, None),
            (dv_ref, DIFF_WIDTH, None, None, None),
            (rq_ref, QK_WIDTH, cr, sr, None), (rk_ref, QK_WIDTH, cr, sr, scale),
            (rv_ref, RET_WIDTH, None, None, None), (rg_ref, RET_WIDTH, None, None, None)):
        for c in range(0, width, MXU_WIDTH):
            wide = proj(col + c, MXU_WIDTH)
            for half in range(0, MXU_WIDTH, LANES):
                p = wide[:, half:half + LANES]
                if cos is not None:
                    p = _rope(p, cos, sin)
                if mul is not None:
                    p = p * mul
                out_ref[:, c + half:c + half + LANES] = p.astype(out_ref.dtype)
        col += width


def _in_projection(rows, g, w_bf16, tables, tile, seq_tiles):
    n = rows.shape[0]
    row_spec = lambda w: pl.BlockSpec((tile, w), lambda i: (i, 0))
    table_spec = pl.BlockSpec((tile, LANES), lambda i: (i % seq_tiles, 0))
    widths = (DIFF_WIDTH, DIFF_WIDTH, DIFF_WIDTH, QK_WIDTH, QK_WIDTH, RET_WIDTH, RET_WIDTH)
    return pl.pallas_call(
        _inproj_kernel,
        grid=(n // tile,),
        in_specs=[row_spec(D_MODEL),
                  pl.BlockSpec((1, D_MODEL), lambda i: (0, 0)),
                  pl.BlockSpec((D_MODEL, IN_PROJ_WIDTH), lambda i: (0, 0)),
                  table_spec, table_spec, table_spec, table_spec],
        out_specs=[row_spec(w) for w in widths],
        out_shape=[jax.ShapeDtypeStruct((n, w), BF16) for w in widths],
        compiler_params=_params(("parallel",)),
        name="in_projection",
    )(rows, g, w_bf16, *tables)


def _rope_tables(pos, inv_freq):
    ang = pos[:, None] * inv_freq[None, :]
    cos = jnp.tile(jnp.cos(ang), (1, LANES // (HEAD_DIM // 2)))
    sin = jnp.sin(ang)
    sin_signed = jnp.tile(jnp.concatenate([-sin, sin], axis=1), (1, LANES // HEAD_DIM))
    return cos, sin_signed


def _transpose_bf16(t):
    return t.astype(F32).T.astype(BF16)


def _with_ones_rows(vt):
    return jnp.concatenate([vt, jnp.ones((ONES_ROWS, vt.shape[1]), vt.dtype)], axis=0)


def _attn_kernel(q_ref, k_ref, v_ref, km_ref, vm_ref, lp_ref, sg_ref, o_ref,
                 vt_sc, m_sc, acc_sc, sa_sc, sb_sc, *, tile, n_chunks):
    i = pl.program_id(2)
    heads = range(HEADS_PER_STEP)
    lanes = lambda hh: slice(hh * LANES, (hh + 1) * LANES)

    @pl.when(i == 0)
    def _():
        def transpose_chunk(j, carry):
            start = pl.multiple_of(j * tile, tile)
            for hh in heads:
                vt_sc[hh, j] = _with_ones_rows(_transpose_bf16(v_ref[pl.ds(start, tile), lanes(hh)]))
            return carry
        lax.fori_loop(0, n_chunks, transpose_chunk, 0)

    qcat = []
    for hh in heads:
        qt = q_ref[:, lanes(hh)].astype(F32).T
        sub = lax.broadcasted_iota(jnp.int32, qt.shape, 0)
        qcat.append(jnp.concatenate([jnp.where(sub < HEAD_DIM, qt, 0.0).astype(BF16),
                                     jnp.where(sub >= HEAD_DIM, qt, 0.0).astype(BF16)], axis=1))

    def scores(hh, j):
        start = pl.multiple_of(j * tile, tile)
        return jnp.dot(k_ref[pl.ds(start, tile), lanes(hh)], qcat[hh], preferred_element_type=F32)

    for hh in heads:
        sa_sc[hh] = scores(hh, 0)
        m_sc[hh] = jnp.full(m_sc.shape[1:], -jnp.inf, F32)
        acc_sc[hh] = jnp.zeros(acc_sc.shape[1:], F32)

    meta_row = lax.broadcasted_iota(jnp.int32, (META_ROWS, 2 * tile), 0) >= META_ROWS - N_META
    s_meta = [jnp.where(meta_row, jnp.dot(km_ref[:, lanes(hh)], qcat[hh], preferred_element_type=F32), -jnp.inf)
              for hh in heads]
    vmt = [_with_ones_rows(_transpose_bf16(vm_ref[:, lanes(hh)])) for hh in heads]

    def absorb(hh, s_ref, j, last):
        s = s_ref[hh]
        m_old = m_sc[hh]
        if last:
            key = lax.broadcasted_iota(jnp.int32, s.shape, 0)
            lane = lax.broadcasted_iota(jnp.int32, s.shape, 1)
            s = jnp.where(key <= jnp.where(lane >= tile, lane - tile, lane), s, -jnp.inf)
            m_old = jnp.maximum(m_old, jnp.max(s_meta[hh], axis=0, keepdims=True))
        m_new = jnp.maximum(m_old, jnp.max(s, axis=0, keepdims=True))
        update = jnp.dot(vt_sc[hh, j], jnp.exp2(s - m_new).astype(BF16), preferred_element_type=F32)
        if last:
            update = update + jnp.dot(vmt[hh], jnp.exp2(s_meta[hh] - m_new).astype(BF16),
                                      preferred_element_type=F32)
        acc_sc[hh] = jnp.exp2(m_sc[hh] - m_new) * acc_sc[hh] + update
        m_sc[hh] = m_new

    def pair(t, carry):
        for hh in heads:
            sb_sc[hh] = scores(hh, 2 * t + 1)
            absorb(hh, sa_sc, 2 * t, False)
        for hh in heads:
            sa_sc[hh] = scores(hh, 2 * t + 2)
            absorb(hh, sb_sc, 2 * t + 1, False)
        return carry

    lax.fori_loop(0, i // 2, pair, 0)

    @pl.when(i % 2 == 0)
    def _():
        for hh in heads:
            absorb(hh, sa_sc, i, True)

    @pl.when(i % 2 == 1)
    def _():
        for hh in heads:
            sb_sc[hh] = scores(hh, i)
            absorb(hh, sa_sc, i - 1, False)
        for hh in heads:
            absorb(hh, sb_sc, i, True)

    lp = lp_ref[...]
    lam = (jnp.exp(jnp.sum(lp[0:1] * lp[1:2], axis=-1, keepdims=True))
           - jnp.exp(jnp.sum(lp[2:3] * lp[3:4], axis=-1, keepdims=True)) + LAMBDA_INIT)
    for hh in heads:
        num, den = acc_sc[hh, 0:VALUE_DIM, :], acc_sc[hh, VALUE_DIM:VALUE_DIM + 1, :]
        o = num[:, :tile] / den[:, :tile] - lam * (num[:, tile:] / den[:, tile:])
        o = o * lax.rsqrt(jnp.mean(o * o, axis=0, keepdims=True) + NORM_EPS)
        o_ref[:, lanes(hh)] = (o * sg_ref[...] * (1.0 - LAMBDA_INIT)).T.astype(o_ref.dtype)


def _diff_attention(dq, dk, dv, km, vm, lam_params, subln_g):
    b, seq, _ = dq.shape
    tile = ATTN_TILE
    n_chunks = seq // tile
    width = HEADS_PER_STEP * LANES
    head_rows = pl.BlockSpec((None, tile, width), lambda bi, h, i: (bi, i, h))
    head_seq = pl.BlockSpec((None, seq, width), lambda bi, h, i: (bi, 0, h))
    head_meta = pl.BlockSpec((META_ROWS, width), lambda bi, h, i: (0, h))
    per_head = lambda *shape: pltpu.VMEM((HEADS_PER_STEP,) + shape, F32)
    return pl.pallas_call(
        functools.partial(_attn_kernel, tile=tile, n_chunks=n_chunks),
        grid=(b, DIFF_HEADS // HEADS_PER_STEP, n_chunks),
        in_specs=[head_rows, head_seq, head_seq, head_meta, head_meta,
                  pl.BlockSpec((4, HEAD_DIM), lambda bi, h, i: (0, 0)),
                  pl.BlockSpec((VALUE_DIM, 1), lambda bi, h, i: (0, 0))],
        out_specs=head_rows,
        out_shape=jax.ShapeDtypeStruct((b, seq, DIFF_WIDTH), BF16),
        scratch_shapes=[pltpu.VMEM((HEADS_PER_STEP, n_chunks, VALUE_DIM + ONES_ROWS, tile), BF16),
                        per_head(1, 2 * tile), per_head(VALUE_DIM + ONES_ROWS, 2 * tile),
                        per_head(tile, 2 * tile), per_head(tile, 2 * tile)],
        compiler_params=_params(("parallel", "parallel", "arbitrary")),
        name="diff_attention",
    )(dq, dk, dv, km, vm, lam_params, subln_g)


def _decay_col(log_gamma, n, offset_fn):
    pos = lax.broadcasted_iota(jnp.int32, (n, 1), 0).astype(F32)
    return jnp.exp(log_gamma * offset_fn(pos))


def _ret_kernel(q_ref, k_ref, v_ref, g_ref, km_ref, vm_ref, o_ref, state_sc, decay_sc, *, chunk):
    c = pl.program_id(1)

    @pl.when(c == 0)
    def _():
        km = km_ref[...].astype(F32)
        for h in range(RET_HEADS):
            pair = h // 2
            kt = km[:, pair * LANES:(pair + 1) * LANES].T.astype(BF16)
            kd = _decay_col(_LOG_GAMMA[h], META_ROWS, lambda p: (META_ROWS - 1) - p)
            vd = (vm_ref[:, h * VALUE_DIM:(h + 1) * VALUE_DIM].astype(F32) * kd).astype(BF16)
            meta_state = jnp.dot(kt, vd, preferred_element_type=F32)
            for bb in range(RET_BATCH_ROWS):
                state_sc[bb, h] = meta_state
        row = lax.broadcasted_iota(jnp.int32, (chunk, chunk), 0)
        col = lax.broadcasted_iota(jnp.int32, (chunk, chunk), 1)
        rel = (row - col).astype(F32)
        for h in range(RET_HEADS):
            decay_sc[h] = jnp.where(rel >= 0, jnp.exp(_LOG_GAMMA[h] * jnp.maximum(rel, 0.0)), 0.0)

    lane = lax.broadcasted_iota(jnp.int32, (chunk, LANES), 1)
    for bb, pair in [(bb, pair) for pair in range(RET_HEADS // 2) for bb in range(RET_BATCH_ROWS)]:
        q_pair = q_ref[bb, :, pair * LANES:(pair + 1) * LANES]
        k_pair = k_ref[bb, :, pair * LANES:(pair + 1) * LANES]
        kt = k_pair.astype(F32).T.astype(BF16)
        for h in (2 * pair, 2 * pair + 1):
            lg = _LOG_GAMMA[h]
            own = (lane < HEAD_DIM) if h % 2 == 0 else (lane >= HEAD_DIM)
            qm = jnp.where(own, q_pair, jnp.zeros_like(q_pair))
            v = v_ref[bb, :, h * VALUE_DIM:(h + 1) * VALUE_DIM]
            s = lax.dot_general(qm, k_pair, (((1,), (1,)), ((), ())), preferred_element_type=F32)
            y = jnp.dot((s * decay_sc[h]).astype(BF16), v, preferred_element_type=F32)
            qd = _decay_col(lg, chunk, lambda p: p + 1.0)
            state = state_sc[bb, h]
            y = y + jnp.dot((qm.astype(F32) * qd).astype(BF16), state.astype(BF16),
                            preferred_element_type=F32)
            kd = _decay_col(lg, chunk, lambda p: (chunk - 1) - p)
            vd = (v.astype(F32) * kd).astype(BF16)
            state_sc[bb, h] = math.exp(lg * chunk) * state + jnp.dot(kt, vd, preferred_element_type=F32)
            y = y * lax.rsqrt(jnp.mean(y * y, axis=-1, keepdims=True) + NORM_EPS)
            gate = g_ref[bb, :, h * VALUE_DIM:(h + 1) * VALUE_DIM].astype(F32)
            o_ref[bb, :, h * VALUE_DIM:(h + 1) * VALUE_DIM] = (y * gate * jax.nn.sigmoid(gate)).astype(o_ref.dtype)


def _retention(rq, rk, rv, rg, km, vm):
    b, seq, _ = rq.shape
    chunk = RET_CHUNK
    assert b % RET_BATCH_ROWS == 0
    rows = lambda w: pl.BlockSpec((RET_BATCH_ROWS, chunk, w), lambda bi, c: (bi, c, 0))
    return pl.pallas_call(
        functools.partial(_ret_kernel, chunk=chunk),
        grid=(b // RET_BATCH_ROWS, seq // chunk),
        in_specs=[rows(QK_WIDTH), rows(QK_WIDTH), rows(RET_WIDTH), rows(RET_WIDTH),
                  pl.BlockSpec((META_ROWS, QK_WIDTH), lambda bi, c: (0, 0)),
                  pl.BlockSpec((META_ROWS, RET_WIDTH), lambda bi, c: (0, 0))],
        out_specs=rows(RET_WIDTH),
        out_shape=jax.ShapeDtypeStruct((b, seq, RET_WIDTH), BF16),
        scratch_shapes=[pltpu.VMEM((RET_BATCH_ROWS, RET_HEADS, LANES, VALUE_DIM), F32),
                        pltpu.VMEM((RET_HEADS, chunk, chunk), F32)],
        compiler_params=_params(("parallel", "arbitrary")),
        name="retention",
    )(rq, rk, rv, rg, km, vm)


def _route_kernel(x_ref, d_ref, r_ref, wo_ref, g_ref, wrh_ref, wrl_ref, brt_ref,
                  h_ref, xn_ref, pos_ref, col_ref, cnt_ref, *, tile):
    group_hots, group_gates = [], []
    for r0 in range(0, tile, ROUTE_GROUP):
        rows = slice(r0, r0 + ROUTE_GROUP)
        h = (x_ref[rows, :]
             + jnp.dot(d_ref[rows, :], wo_ref[0:DIFF_WIDTH, :], preferred_element_type=F32)
             + jnp.dot(r_ref[rows, :], wo_ref[DIFF_WIDTH:, :], preferred_element_type=F32))
        h_ref[rows, :] = h
        xn = h * lax.rsqrt(jnp.mean(h * h, axis=-1, keepdims=True) + NORM_EPS) * g_ref[...]
        xn_hi = xn.astype(BF16)
        xn_ref[rows, :] = xn_hi
        xn_lo = (xn - xn_hi.astype(F32)).astype(BF16)
        logits = (jnp.dot(xn_hi, wrh_ref[...], preferred_element_type=F32)
                  + jnp.dot(xn_hi, wrl_ref[...], preferred_element_type=F32)
                  + jnp.dot(xn_lo, wrh_ref[...], preferred_element_type=F32))
        logits = logits.T[0:N_EXPERTS, :] + brt_ref[...]
        expert = lax.broadcasted_iota(jnp.int32, logits.shape, 0)
        work = logits
        vals, hots = [], []
        for k in range(TOP_K):
            v = jnp.max(work, axis=0, keepdims=True)
            idx = jnp.min(jnp.where(work == v, expert, N_EXPERTS), axis=0, keepdims=True)
            hot = expert == idx
            work = jnp.where(hot, -jnp.inf, work)
            vals.append(v)
            hots.append(hot.astype(F32))
        exps = [jnp.exp(v - vals[0]) for v in vals]
        denom = exps[0] + exps[1] + exps[2] + exps[3]
        group_hots.append(hots)
        group_gates.append([e / denom for e in exps])
    hots = [jnp.concatenate([g[k] for g in group_hots], axis=1) for k in range(TOP_K)]
    gates = [jnp.concatenate([g[k] for g in group_gates], axis=1) for k in range(TOP_K)]

    chosen = hots[0] + hots[1] + hots[2] + hots[3]
    c = lax.broadcasted_iota(jnp.int32, (tile, tile), 0)
    r = lax.broadcasted_iota(jnp.int32, (tile, tile), 1)
    earlier = (c < r).astype(BF16)
    before = jnp.dot(chosen.astype(BF16), earlier, preferred_element_type=F32)
    cnt = jnp.sum(chosen, axis=1, keepdims=True)
    cnt_pad = jnp.ceil(cnt / SEG_ALIGN) * SEG_ALIGN
    er = lax.broadcasted_iota(jnp.int32, (N_EXPERTS, N_EXPERTS), 0)
    ec = lax.broadcasted_iota(jnp.int32, (N_EXPERTS, N_EXPERTS), 1)
    seg_start = jnp.dot((ec < er).astype(F32), jnp.broadcast_to(cnt_pad, (N_EXPERTS, LANES)),
                        preferred_element_type=F32, precision=lax.Precision.HIGHEST)[:, 0:1]
    base = seg_start + before
    pos = [jnp.sum(hot * base, axis=0, keepdims=True) for hot in hots]
    for k in range(TOP_K):
        pos_ref[k:k + 1, :] = pos[k].astype(jnp.int32)
    cnt_ref[...] = jnp.broadcast_to(cnt, (N_EXPERTS, LANES)).astype(jnp.int32)
    stacked = jnp.concatenate(pos + gates + [jnp.zeros((LANES - 2 * TOP_K, tile), F32)], axis=0)
    col_ref[...] = stacked.T


def _out_proj_and_route(x, diff, ret, wo, g, wr, brt):
    n = x.shape[0]
    tile = ROUTE_TILE
    rows = lambda w: pl.BlockSpec((tile, w), lambda i: (i, 0))
    full = lambda a, b: pl.BlockSpec((a, b), lambda i: (0, 0))
    wr = jnp.pad(wr, ((0, 0), (0, LANES - N_EXPERTS)))
    wr_hi = wr.astype(BF16)
    wr_lo = (wr - wr_hi.astype(F32)).astype(BF16)
    return pl.pallas_call(
        functools.partial(_route_kernel, tile=tile),
        grid=(n // tile,),
        in_specs=[rows(D_MODEL), rows(DIFF_WIDTH), rows(RET_WIDTH), full(D_MODEL, D_MODEL),
                  full(1, D_MODEL), full(D_MODEL, LANES), full(D_MODEL, LANES), full(N_EXPERTS, 1)],
        out_specs=[rows(D_MODEL), rows(D_MODEL),
                   pl.BlockSpec((TOP_K, tile), lambda i: (0, i)), rows(LANES),
                   pl.BlockSpec((None, N_EXPERTS, LANES), lambda i: (i, 0, 0))],
        out_shape=[jax.ShapeDtypeStruct((n, D_MODEL), F32), jax.ShapeDtypeStruct((n, D_MODEL), BF16),
                   jax.ShapeDtypeStruct((TOP_K, n), jnp.int32), jax.ShapeDtypeStruct((n, LANES), F32),
                   jax.ShapeDtypeStruct((n // tile, N_EXPERTS, LANES), jnp.int32)],
        compiler_params=_params(("parallel",)),
        name="out_proj_route",
    )(x, diff, ret, wo, g, wr_hi, wr_lo, brt)


def _split_gate_linear(w_ref, o_ref):
    group = 2 * LANES
    src = lax.broadcasted_iota(jnp.int32, (group, group), 0)
    dst = lax.broadcasted_iota(jnp.int32, (group, group), 1)
    select = (src == jnp.where(dst < LANES, 2 * dst, 2 * (dst - LANES) + 1)).astype(BF16)
    for c in range(2 * D_FF // group):
        part = jnp.dot(w_ref[:, c * group:(c + 1) * group].astype(BF16), select,
                       preferred_element_type=F32).astype(BF16)
        o_ref[:, c * LANES:(c + 1) * LANES] = part[:, :LANES]
        o_ref[:, D_FF + c * LANES:D_FF + (c + 1) * LANES] = part[:, LANES:]


def _segment_copy(local_buf, local_row, global_hbm, global_row, rows, sem, to_global):
    local_piece = local_buf.at[pl.ds(local_row, rows), :]
    global_piece = global_hbm.at[pl.ds(global_row, rows), :]
    if to_global:
        return pltpu.make_async_copy(local_piece, global_piece, sem)
    return pltpu.make_async_copy(global_piece, local_piece, sem)


def _piece_list(seg_local, seg_global, n_pieces, skip_rows, rows_per_piece, max_pieces):
    cum = jnp.cumsum(n_pieces, axis=1)
    j = jnp.arange(max_pieces, dtype=jnp.int32)
    owner = jnp.sum(cum[:, None, :] <= j[None, :, None], axis=2)
    onehot = owner[:, :, None] == jnp.arange(N_EXPERTS, dtype=jnp.int32)[None, None, :]
    pick = lambda t: jnp.sum(jnp.where(onehot, t[:, None, :], 0), axis=2)
    within = (j[None, :] - pick(cum - n_pieces)) * rows_per_piece
    return pick(seg_local + skip_rows) + within, pick(seg_global + skip_rows) + within, cum[:, -1]


def _start_segments(tables, tile_index, local_buf, global_hbm, sem, to_global):
    big_local, big_global, small_local, small_global, counts = tables

    def run(local_ref, global_ref, count, rows, stride):
        first = tile_index * stride

        def per_piece(p, carry):
            lo = pl.multiple_of(local_ref[first + p], SEG_ALIGN)
            gl = pl.multiple_of(global_ref[first + p], SEG_ALIGN)
            _segment_copy(local_buf, lo, global_hbm, gl, rows, sem, to_global).start()
            return carry

        lax.fori_loop(0, count, per_piece, 0)

    run(big_local, big_global, counts[3 * tile_index], BIG_PIECE, MAX_BIG_PIECES)
    run(small_local, small_global, counts[3 * tile_index + 1], SEG_ALIGN, MAX_SMALL_PIECES)


def _wait_segments(tables, tile_index, local_buf, global_hbm, sem, to_global):
    counts = tables[4]
    _segment_copy(local_buf, 0, global_hbm, 0, ROUTE_TILE * TOP_K, sem, to_global).wait()

    def per_pad(p, carry):
        _segment_copy(local_buf, 0, global_hbm, 0, SEG_ALIGN, sem, to_global).wait()
        return carry

    lax.fori_loop(0, counts[3 * tile_index + 2], per_pad, 0)


def _dispatch_kernel(big_local, big_global, small_local, small_global, counts, fill_ref, xn_ref, pos_ref,
                     xs_out, loc_buf, zero_buf, seg_sem, fill_sem, *, n_blocks, n_tiles):
    i = pl.program_id(0)
    slot = i % 2

    @pl.when(i == 0)
    def _():
        zero_buf[...] = jnp.zeros_like(zero_buf)

        def fill_copy(j):
            row = pl.multiple_of(j * MOE_TILE, MOE_TILE)
            return pltpu.make_async_copy(zero_buf, xs_out.at[pl.ds(row, MOE_TILE), :], fill_sem)

        def start(j, carry):
            @pl.when(fill_ref[j] > 0)
            def _():
                fill_copy(j).start()
            return carry

        def wait(j, carry):
            @pl.when(fill_ref[j] > 0)
            def _():
                fill_copy(j).wait()
            return carry

        lax.fori_loop(0, n_blocks, start, 0)
        lax.fori_loop(0, n_blocks, wait, 0)

    pos = pos_ref[...]
    row = lax.broadcasted_iota(jnp.int32, (LOC_ROWS, pos.shape[1]), 0)
    select = jnp.zeros(row.shape, F32)
    for k in range(TOP_K):
        select = jnp.where(row == pos[k:k + 1], 1.0, select)
    loc_buf[slot] = jnp.dot(select.astype(BF16), xn_ref[...], preferred_element_type=F32)

    tables = (big_local, big_global, small_local, small_global, counts)
    _start_segments(tables, i, loc_buf.at[slot], xs_out, seg_sem.at[slot], True)

    @pl.when(i > 0)
    def _():
        _wait_segments(tables, i - 1, loc_buf.at[1 - slot], xs_out, seg_sem.at[1 - slot], True)

    @pl.when(i == n_tiles - 1)
    def _():
        _wait_segments(tables, i, loc_buf.at[slot], xs_out, seg_sem.at[slot], True)


def _dispatch(tables, fill_flags, xn, pos, n_blocks):
    n = xn.shape[0]
    tile = ROUTE_TILE
    return pl.pallas_call(
        functools.partial(_dispatch_kernel, n_blocks=n_blocks, n_tiles=n // tile),
        grid_spec=pltpu.PrefetchScalarGridSpec(
            num_scalar_prefetch=6,
            grid=(n // tile,),
            in_specs=[pl.BlockSpec((tile, D_MODEL), lambda i, *_: (i, 0)),
                      pl.BlockSpec((TOP_K, tile), lambda i, *_: (0, i))],
            out_specs=pl.BlockSpec(memory_space=pl.ANY),
            scratch_shapes=[pltpu.VMEM((2, LOC_ROWS, D_MODEL), F32),
                            pltpu.VMEM((MOE_TILE, D_MODEL), F32),
                            pltpu.SemaphoreType.DMA((2,)), pltpu.SemaphoreType.DMA]),
        out_shape=jax.ShapeDtypeStruct((n_blocks * MOE_TILE, D_MODEL), F32),
        compiler_params=_params(("arbitrary",)),
        name="moe_dispatch",
    )(*tables, fill_flags, xn, pos)


def _expert_kernel(be_ref, nu_ref, xs_ref, wu_ref, bu_ref, wd_ref, bd_ref, ys_ref, wu_sc, wd_sc):
    j = pl.program_id(0)

    @pl.when((j == 0) | (be_ref[j] != be_ref[jnp.maximum(j - 1, 0)]))
    def _():
        _split_gate_linear(wu_ref, wu_sc)
        wd_sc[...] = wd_ref[...].astype(BF16)

    @pl.when(j < nu_ref[0])
    def _():
        x = xs_ref[...].astype(BF16)
        glu = jnp.dot(x, wu_sc[:, 0:D_FF], preferred_element_type=F32) + bu_ref[:, 0:D_FF]
        lin = jnp.dot(x, wu_sc[:, D_FF:], preferred_element_type=F32) + bu_ref[:, D_FF:]
        glu = jnp.minimum(glu, SWIGLU_LIMIT)
        lin = jnp.clip(lin, -SWIGLU_LIMIT, SWIGLU_LIMIT)
        act = glu * jax.nn.sigmoid(SWIGLU_ALPHA * glu) * (lin + 1.0)
        ys_ref[...] = jnp.dot(act.astype(BF16), wd_sc[...], preferred_element_type=F32) + bd_ref[...]

    @pl.when(j >= nu_ref[0])
    def _():
        ys_ref[...] = jnp.zeros_like(ys_ref)


def _expert_ffn(block_expert, n_used, xs, wu, bu, wd, bd):
    n_rows = xs.shape[0]
    tile = MOE_TILE
    rows = pl.BlockSpec((tile, D_MODEL), lambda j, be, nu: (jnp.minimum(j, nu[0] - 1), 0))
    per_expert = lambda a, b: pl.BlockSpec((None, a, b), lambda j, be, nu: (be[j], 0, 0))
    return pl.pallas_call(
        _expert_kernel,
        grid_spec=pltpu.PrefetchScalarGridSpec(
            num_scalar_prefetch=2,
            grid=(n_rows // tile,),
            in_specs=[rows, per_expert(D_MODEL, 2 * D_FF), per_expert(1, 2 * D_FF),
                      per_expert(D_FF, D_MODEL), per_expert(1, D_MODEL)],
            out_specs=pl.BlockSpec((tile, D_MODEL), lambda j, be, nu: (j, 0)),
            scratch_shapes=[pltpu.VMEM((D_MODEL, 2 * D_FF), BF16), pltpu.VMEM((D_FF, D_MODEL), BF16)]),
        out_shape=jax.ShapeDtypeStruct((n_rows, D_MODEL), F32),
        compiler_params=_params(("arbitrary",)),
        name="expert_ffn",
    )(block_expert, n_used, xs, wu, bu, wd, bd)


def _combine_kernel(big_local, big_global, small_local, small_global, counts, col_ref, h_ref, ys_hbm, g_ref,
                    o_ref, loc_buf, seg_sem, *, n_tiles):
    i = pl.program_id(0)
    slot = i % 2
    tables = (big_local, big_global, small_local, small_global, counts)

    def fetch(tile_index, buf_slot):
        _start_segments(tables, tile_index, loc_buf.at[buf_slot], ys_hbm, seg_sem.at[buf_slot], False)

    @pl.when(i == 0)
    def _():
        loc_buf[...] = jnp.zeros_like(loc_buf)
        fetch(0, 0)

    @pl.when(i + 1 < n_tiles)
    def _():
        fetch(i + 1, 1 - slot)

    _wait_segments(tables, i, loc_buf.at[slot], ys_hbm, seg_sem.at[slot], False)

    rows = loc_buf[slot].astype(BF16)
    lane = lax.broadcasted_iota(jnp.int32, (COMBINE_ROWS, LOC_ROWS), 1)
    for r in range(0, col_ref.shape[0], COMBINE_ROWS):
        col = col_ref[r:r + COMBINE_ROWS, :]
        weights = jnp.zeros((COMBINE_ROWS, LOC_ROWS), F32)
        for k in range(TOP_K):
            weights = jnp.where(lane == col[:, k:k + 1].astype(jnp.int32),
                                col[:, TOP_K + k:TOP_K + k + 1], weights)
        h = h_ref[r:r + COMBINE_ROWS, :] + jnp.dot(weights.astype(BF16), rows, preferred_element_type=F32)
        o_ref[r:r + COMBINE_ROWS, :] = (h * lax.rsqrt(jnp.mean(h * h, axis=-1, keepdims=True) + NORM_EPS)
                                        * g_ref[...])


def _combine(tables, col, h, ys, g):
    n = h.shape[0]
    tile = ROUTE_TILE
    return pl.pallas_call(
        functools.partial(_combine_kernel, n_tiles=n // tile),
        grid_spec=pltpu.PrefetchScalarGridSpec(
            num_scalar_prefetch=5,
            grid=(n // tile,),
            in_specs=[pl.BlockSpec((tile, LANES), lambda i, *_: (i, 0)),
                      pl.BlockSpec((tile, D_MODEL), lambda i, *_: (i, 0)),
                      pl.BlockSpec(memory_space=pl.ANY),
                      pl.BlockSpec((1, D_MODEL), lambda i, *_: (0, 0))],
            out_specs=pl.BlockSpec((tile, D_MODEL), lambda i, *_: (i, 0)),
            scratch_shapes=[pltpu.VMEM((2, LOC_ROWS, D_MODEL), F32), pltpu.SemaphoreType.DMA((2,))]),
        out_shape=jax.ShapeDtypeStruct((n, D_MODEL), F32),
        compiler_params=_params(("arbitrary",)),
        name="moe_combine",
    )(*tables, col, h, ys, g)


def kernel(x, meta_tokens, attn_norm_g, w_in, diff_lambda, diff_subln_g, w_out, ffn_norm_g,
           w_router, b_router, w_up, b_up, w_down, b_down, final_norm_g):
    b, seq, d = x.shape
    assert d == D_MODEL and seq % PROJ_TILE == 0 and seq % RET_CHUNK == 0 and seq % ATTN_TILE == 0
    assert w_in.shape[0] == 1, "one layer"
    n_tok = b * seq
    assert n_tok % ROUTE_TILE == 0

    w_in_b = w_in[0].astype(BF16)
    w_out_b = w_out[0].astype(BF16)
    w_up_b = w_up[0]
    b_up_s = jnp.concatenate([b_up[0][..., 0::2], b_up[0][..., 1::2]], axis=-1)[:, None, :]
    w_down_b = w_down[0]
    b_down_s = b_down[0][:, None, :]

    diff_inv_freq = ROPE_THETA ** (-jnp.arange(0, HEAD_DIM, 2, dtype=F32) / HEAD_DIM)
    ret_inv_freq = ROPE_THETA ** (-jnp.linspace(0.0, 1.0, HEAD_DIM // 2, dtype=F32))
    pos_x = jnp.arange(seq, dtype=F32) + N_META
    pos_m = jnp.arange(META_ROWS, dtype=F32) - (META_ROWS - N_META)
    tables_x = _rope_tables(pos_x, diff_inv_freq) + _rope_tables(pos_x, ret_inv_freq)
    tables_m = _rope_tables(pos_m, diff_inv_freq) + _rope_tables(pos_m, ret_inv_freq)

    g_attn = attn_norm_g[0][None, :]
    x2 = x.reshape(n_tok, D_MODEL)
    meta_rows = jnp.concatenate(
        [jnp.zeros((META_ROWS - N_META, D_MODEL), x.dtype), meta_tokens.astype(x.dtype)], axis=0)
    dq, dk, dv, rq, rk, rv, rg = _in_projection(x2, g_attn, w_in_b, tables_x, PROJ_TILE, seq // PROJ_TILE)
    _, dk_m, dv_m, _, rk_m, rv_m, _ = _in_projection(meta_rows, g_attn, w_in_b, tables_m, META_ROWS, 1)

    per_batch = lambda t: t.reshape(b, seq, t.shape[-1])
    diff_out = _diff_attention(per_batch(dq), per_batch(dk), per_batch(dv), dk_m, dv_m,
                               diff_lambda[0], diff_subln_g[0][:, None])
    ret_out = _retention(per_batch(rq), per_batch(rk), per_batch(rv), per_batch(rg), rk_m, rv_m)

    h, xn, pos, col, counts = _out_proj_and_route(
        x2, diff_out.reshape(n_tok, DIFF_WIDTH), ret_out.reshape(n_tok, RET_WIDTH),
        w_out_b, ffn_norm_g[0][None, :], w_router[0], b_router[0][:, None])

    n_tiles = n_tok // ROUTE_TILE
    n_blocks = -(-(n_tok * TOP_K + n_tiles * N_EXPERTS * (SEG_ALIGN - 1) + N_EXPERTS * (MOE_TILE - 1))
                 // MOE_TILE)
    seg_rows = (counts[:, :, 0] + SEG_ALIGN - 1) // SEG_ALIGN * SEG_ALIGN
    seg_local = jnp.cumsum(seg_rows, axis=1) - seg_rows
    blocks_per = (jnp.sum(seg_rows, axis=0) + MOE_TILE - 1) // MOE_TILE
    block_end = jnp.cumsum(blocks_per)
    group_start = (block_end - blocks_per) * MOE_TILE
    seg_global = group_start[None, :] + jnp.cumsum(seg_rows, axis=0) - seg_rows
    flat = lambda t: t.reshape(-1).astype(jnp.int32)
    n_big = seg_rows // BIG_PIECE
    big_local, big_global, big_total = _piece_list(seg_local, seg_global, n_big, 0, BIG_PIECE, MAX_BIG_PIECES)
    small_local, small_global, small_total = _piece_list(
        seg_local, seg_global, seg_rows % BIG_PIECE // SEG_ALIGN, n_big * BIG_PIECE, SEG_ALIGN, MAX_SMALL_PIECES)
    pad_pieces = (jnp.sum(seg_rows, axis=1) - ROUTE_TILE * TOP_K) // SEG_ALIGN
    tables = (flat(big_local), flat(big_global), flat(small_local), flat(small_global),
              flat(jnp.stack([big_total, small_total, pad_pieces], axis=1)))

    n_used = block_end[-1:].astype(jnp.int32)
    all_blocks = jnp.arange(n_blocks, dtype=jnp.int32)
    block_ids = jnp.minimum(all_blocks, n_used[0] - 1)
    block_expert = jnp.minimum(jnp.sum(block_end[None, :] <= block_ids[:, None], axis=1),
                               N_EXPERTS - 1).astype(jnp.int32)
    is_group_end = jnp.any((block_end[None, :] == all_blocks[:, None] + 1) & (blocks_per[None, :] > 0), axis=1)
    fill_flags = (is_group_end | (all_blocks >= n_used[0])).astype(jnp.int32)

    xs = _dispatch(tables, fill_flags, xn, pos, n_blocks)
    ys = _expert_ffn(block_expert, n_used, xs, w_up_b, b_up_s, w_down_b, b_down_s)
    out = _combine(tables, col, h, ys, final_norm_g[None, :])
    return out.reshape(b, seq, D_MODEL)
```

```python
import functools
import math

import jax
import jax.numpy as jnp
import numpy as np
from jax import lax
from jax.experimental import pallas as pl
from jax.experimental.pallas import tpu as pltpu

F32 = jnp.float32
BF16 = jnp.bfloat16

D_MODEL = 1024
N_META = 16
ROPE_THETA = 10000.0
NORM_EPS = 1e-5
DIFF_HEADS = 4
HEAD_DIM = 64
VALUE_DIM = 128
DIFF_WIDTH = DIFF_HEADS * VALUE_DIM
RET_HEADS = 4
RET_WIDTH = RET_HEADS * VALUE_DIM
QK_WIDTH = RET_HEADS * HEAD_DIM
IN_PROJ_WIDTH = 3 * DIFF_WIDTH + 2 * QK_WIDTH + 2 * RET_WIDTH
LAMBDA_INIT = 0.8 - 0.6 * math.exp(-0.3 * 0)
N_EXPERTS = 32
TOP_K = 4
D_FF = D_MODEL
SWIGLU_ALPHA = 1.702
SWIGLU_LIMIT = 7.0

LANES = 128
MXU_WIDTH = 256
META_ROWS = 128
VMEM_LIMIT = 56 * 1024 * 1024

ONES_ROWS = 16
LOG2_E = math.log2(math.e)

PROJ_TILE = 1024
ATTN_TILE = 512
HEADS_PER_STEP = 4
RET_CHUNK = 256
RET_BATCH_ROWS = 8
ROUTE_TILE = 512
ROUTE_GROUP = 256
MOE_TILE = 1024
SEG_ALIGN = 8
BIG_PIECE = 4 * SEG_ALIGN
COMBINE_ROWS = 128
LOC_ROWS = ROUTE_TILE * TOP_K + N_EXPERTS * SEG_ALIGN
MAX_BIG_PIECES = LOC_ROWS // BIG_PIECE
MAX_SMALL_PIECES = N_EXPERTS * (BIG_PIECE // SEG_ALIGN - 1)

_LOG_GAMMA = [float(v) for v in np.log1p(-np.exp2(-5.0 - np.arange(RET_HEADS, dtype=np.float32)))]


def _params(sem):
    return pltpu.CompilerParams(dimension_semantics=sem, vmem_limit_bytes=VMEM_LIMIT)


def _rope(t, cos, sin_signed):
    lane = lax.broadcasted_iota(jnp.int32, t.shape, 1)
    first_half = (lane % HEAD_DIM) < (HEAD_DIM // 2)
    partner = jnp.where(first_half, pltpu.roll(t, LANES - HEAD_DIM // 2, 1),
                        pltpu.roll(t, HEAD_DIM // 2, 1))
    return t * cos + partner * sin_signed


def _inproj_kernel(x_ref, g_ref, w_ref, cd_ref, sd_ref, cr_ref, sr_ref,
                   dq_ref, dk_ref, dv_ref, rq_ref, rk_ref, rv_ref, rg_ref):
    x = x_ref[...]
    ms = jnp.mean(x * x, axis=-1, keepdims=True)
    hn = (x * lax.rsqrt(ms + NORM_EPS) * g_ref[...]).astype(BF16)
    cd, sd, cr, sr = cd_ref[...], sd_ref[...], cr_ref[...], sr_ref[...]
    scale = HEAD_DIM ** -0.5

    def proj(col, width):
        return jnp.dot(hn, w_ref[:, col:col + width], preferred_element_type=F32)

    col = 0
    for out_ref, width, cos, sin, mul in (
            (dq_ref, DIFF_WIDTH, cd, sd, scale * LOG2_E), (dk_ref, DIFF_WIDTH, cd, sd, None),
            (dv_ref, DIFF_WIDTH, None, None, None),
            (rq_ref, QK_WIDTH, cr, sr, None), (rk_ref, QK_WIDTH, cr, sr, scale),
            (rv_ref, RET_WIDTH, None, None, None), (rg_ref, RET_WIDTH, None, None, None)):
        for c in range(0, width, MXU_WIDTH):
            wide = proj(col + c, MXU_WIDTH)
            for half in range(0, MXU_WIDTH, LANES):
                p = wide[:, half:half + LANES]
                if cos is not None:
                    p = _rope(p, cos, sin)
                if mul is not None:
                    p = p * mul
                out_ref[:, c + half:c + half + LANES] = p.astype(out_ref.dtype)
        col += width


def _in_projection(rows, g, w_bf16, tables, tile, seq_tiles):
    n = rows.shape[0]
    row_spec = lambda w: pl.BlockSpec((tile, w), lambda i: (i, 0))
    table_spec = pl.BlockSpec((tile, LANES), lambda i: (i % seq_tiles, 0))
    widths = (DIFF_WIDTH, DIFF_WIDTH, DIFF_WIDTH, QK_WIDTH, QK_WIDTH, RET_WIDTH, RET_WIDTH)
    return pl.pallas_call(
        _inproj_kernel,
        grid=(n // tile,),
        in_specs=[row_spec(D_MODEL),
                  pl.BlockSpec((1, D_MODEL), lambda i: (0, 0)),
                  pl.BlockSpec((D_MODEL, IN_PROJ_WIDTH), lambda i: (0, 0)),
                  table_spec, table_spec, table_spec, table_spec],
        out_specs=[row_spec(w) for w in widths],
        out_shape=[jax.ShapeDtypeStruct((n, w), BF16) for w in widths],
        compiler_params=_params(("parallel",)),
        name="in_projection",
    )(rows, g, w_bf16, *tables)


def _rope_tables(pos, inv_freq):
    ang = pos[:, None] * inv_freq[None, :]
    cos = jnp.tile(jnp.cos(ang), (1, LANES // (HEAD_DIM // 2)))
    sin = jnp.sin(ang)
    sin_signed = jnp.tile(jnp.concatenate([-sin, sin], axis=1), (1, LANES // HEAD_DIM))
    return cos, sin_signed


def _transpose_bf16(t):
    return t.astype(F32).T.astype(BF16)


def _with_ones_rows(vt):
    return jnp.concatenate([vt, jnp.ones((ONES_ROWS, vt.shape[1]), vt.dtype)], axis=0)


def _attn_kernel(q_ref, k_ref, v_ref, km_ref, vm_ref, lp_ref, sg_ref, o_ref,
                 vt_sc, m_sc, acc_sc, sa_sc, sb_sc, *, tile, n_chunks):
    i = pl.program_id(2)
    heads = range(HEADS_PER_STEP)
    lanes = lambda hh: slice(hh * LANES, (hh + 1) * LANES)

    @pl.when(i == 0)
    def _():
        def transpose_chunk(j, carry):
            start = pl.multiple_of(j * tile, tile)
            for hh in heads:
                vt_sc[hh, j] = _with_ones_rows(_transpose_bf16(v_ref[pl.ds(start, tile), lanes(hh)]))
            return carry
        lax.fori_loop(0, n_chunks, transpose_chunk, 0)

    qcat = []
    for hh in heads:
        qt = q_ref[:, lanes(hh)].astype(F32).T
        sub = lax.broadcasted_iota(jnp.int32, qt.shape, 0)
        qcat.append(jnp.concatenate([jnp.where(sub < HEAD_DIM, qt, 0.0).astype(BF16),
                                     jnp.where(sub >= HEAD_DIM, qt, 0.0).astype(BF16)], axis=1))

    def scores(hh, j):
        start = pl.multiple_of(j * tile, tile)
        return jnp.dot(k_ref[pl.ds(start, tile), lanes(hh)], qcat[hh], preferred_element_type=F32)

    for hh in heads:
        sa_sc[hh] = scores(hh, 0)
        m_sc[hh] = jnp.full(m_sc.shape[1:], -jnp.inf, F32)
        acc_sc[hh] = jnp.zeros(acc_sc.shape[1:], F32)

    meta_row = lax.broadcasted_iota(jnp.int32, (META_ROWS, 2 * tile), 0) >= META_ROWS - N_META
    s_meta = [jnp.where(meta_row, jnp.dot(km_ref[:, lanes(hh)], qcat[hh], preferred_element_type=F32), -jnp.inf)
              for hh in heads]
    vmt = [_with_ones_rows(_transpose_bf16(vm_ref[:, lanes(hh)])) for hh in heads]

    def absorb(hh, s_ref, j, last):
        s = s_ref[hh]
        m_old = m_sc[hh]
        if last:
            key = lax.broadcasted_iota(jnp.int32, s.shape, 0)
            lane = lax.broadcasted_iota(jnp.int32, s.shape, 1)
            s = jnp.where(key <= jnp.where(lane >= tile, lane - tile, lane), s, -jnp.inf)
            m_old = jnp.maximum(m_old, jnp.max(s_meta[hh], axis=0, keepdims=True))
        m_new = jnp.maximum(m_old, jnp.max(s, axis=0, keepdims=True))
        update = jnp.dot(vt_sc[hh, j], jnp.exp2(s - m_new).astype(BF16), preferred_element_type=F32)
        if last:
            update = update + jnp.dot(vmt[hh], jnp.exp2(s_meta[hh] - m_new).astype(BF16),
                                      preferred_element_type=F32)
        acc_sc[hh] = jnp.exp2(m_sc[hh] - m_new) * acc_sc[hh] + update
        m_sc[hh] = m_new

    def pair(t, carry):
        for hh in heads:
            sb_sc[hh] = scores(hh, 2 * t + 1)
            absorb(hh, sa_sc, 2 * t, False)
        for hh in heads:
            sa_sc[hh] = scores(hh, 2 * t + 2)
            absorb(hh, sb_sc, 2 * t + 1, False)
        return carry

    lax.fori_loop(0, i // 2, pair, 0)

    @pl.when(i % 2 == 0)
    def _():
        for hh in heads:
            absorb(hh, sa_sc, i, True)

    @pl.when(i % 2 == 1)
    def _():
        for hh in heads:
            sb_sc[hh] = scores(hh, i)
            absorb(hh, sa_sc, i - 1, False)
        for hh in heads:
            absorb(hh, sb_sc, i, True)

    lp = lp_ref[...]
    lam = (jnp.exp(jnp.sum(lp[0:1] * lp[1:2], axis=-1, keepdims=True))
           - jnp.exp(jnp.sum(lp[2:3] * lp[3:4], axis=-1, keepdims=True)) + LAMBDA_INIT)
    for hh in heads:
        num, den = acc_sc[hh, 0:VALUE_DIM, :], acc_sc[hh, VALUE_DIM:VALUE_DIM + 1, :]
        o = num[:, :tile] / den[:, :tile] - lam * (num[:, tile:] / den[:, tile:])
        o = o * lax.rsqrt(jnp.mean(o * o, axis=0, keepdims=True) + NORM_EPS)
        o_ref[:, lanes(hh)] = (o * sg_ref[...] * (1.0 - LAMBDA_INIT)).T.astype(o_ref.dtype)


def _diff_attention(dq, dk, dv, km, vm, lam_params, subln_g):
    b, seq, _ = dq.shape
    tile = ATTN_TILE
    n_chunks = seq // tile
    width = HEADS_PER_STEP * LANES
    head_rows = pl.BlockSpec((None, tile, width), lambda bi, h, i: (bi, i, h))
    head_seq = pl.BlockSpec((None, seq, width), lambda bi, h, i: (bi, 0, h))
    head_meta = pl.BlockSpec((META_ROWS, width), lambda bi, h, i: (0, h))
    per_head = lambda *shape: pltpu.VMEM((HEADS_PER_STEP,) + shape, F32)
    return pl.pallas_call(
        functools.partial(_attn_kernel, tile=tile, n_chunks=n_chunks),
        grid=(b, DIFF_HEADS // HEADS_PER_STEP, n_chunks),
        in_specs=[head_rows, head_seq, head_seq, head_meta, head_meta,
                  pl.BlockSpec((4, HEAD_DIM), lambda bi, h, i: (0, 0)),
                  pl.BlockSpec((VALUE_DIM, 1), lambda bi, h, i: (0, 0))],
        out_specs=head_rows,
        out_shape=jax.ShapeDtypeStruct((b, seq, DIFF_WIDTH), BF16),
        scratch_shapes=[pltpu.VMEM((HEADS_PER_STEP, n_chunks, VALUE_DIM + ONES_ROWS, tile), BF16),
                        per_head(1, 2 * tile), per_head(VALUE_DIM + ONES_ROWS, 2 * tile),
                        per_head(tile, 2 * tile), per_head(tile, 2 * tile)],
        compiler_params=_params(("parallel", "parallel", "arbitrary")),
        name="diff_attention",
    )(dq, dk, dv, km, vm, lam_params, subln_g)


def _decay_col(log_gamma, n, offset_fn):
    pos = lax.broadcasted_iota(jnp.int32, (n, 1), 0).astype(F32)
    return jnp.exp(log_gamma * offset_fn(pos))


def _ret_kernel(q_ref, k_ref, v_ref, g_ref, km_ref, vm_ref, o_ref, state_sc, decay_sc, *, chunk):
    c = pl.program_id(1)

    @pl.when(c == 0)
    def _():
        km = km_ref[...].astype(F32)
        for h in range(RET_HEADS):
            pair = h // 2
            kt = km[:, pair * LANES:(pair + 1) * LANES].T.astype(BF16)
            kd = _decay_col(_LOG_GAMMA[h], META_ROWS, lambda p: (META_ROWS - 1) - p)
            vd = (vm_ref[:, h * VALUE_DIM:(h + 1) * VALUE_DIM].astype(F32) * kd).astype(BF16)
            meta_state = jnp.dot(kt, vd, preferred_element_type=F32)
            for bb in range(RET_BATCH_ROWS):
                state_sc[bb, h] = meta_state
        row = lax.broadcasted_iota(jnp.int32, (chunk, chunk), 0)
        col = lax.broadcasted_iota(jnp.int32, (chunk, chunk), 1)
        rel = (row - col).astype(F32)
        for h in range(RET_HEADS):
            decay_sc[h] = jnp.where(rel >= 0, jnp.exp(_LOG_GAMMA[h] * jnp.maximum(rel, 0.0)), 0.0)

    lane = lax.broadcasted_iota(jnp.int32, (chunk, LANES), 1)
    for bb, pair in [(bb, pair) for pair in range(RET_HEADS // 2) for bb in range(RET_BATCH_ROWS)]:
        q_pair = q_ref[bb, :, pair * LANES:(pair + 1) * LANES]
        k_pair = k_ref[bb, :, pair * LANES:(pair + 1) * LANES]
        kt = k_pair.astype(F32).T.astype(BF16)
        for h in (2 * pair, 2 * pair + 1):
            lg = _LOG_GAMMA[h]
            own = (lane < HEAD_DIM) if h % 2 == 0 else (lane >= HEAD_DIM)
            qm = jnp.where(own, q_pair, jnp.zeros_like(q_pair))
            v = v_ref[bb, :, h * VALUE_DIM:(h + 1) * VALUE_DIM]
            s = lax.dot_general(qm, k_pair, (((1,), (1,)), ((), ())), preferred_element_type=F32)
            y = jnp.dot((s * decay_sc[h]).astype(BF16), v, preferred_element_type=F32)
            qd = _decay_col(lg, chunk, lambda p: p + 1.0)
            state = state_sc[bb, h]
            y = y + jnp.dot((qm.astype(F32) * qd).astype(BF16), state.astype(BF16),
                            preferred_element_type=F32)
            kd = _decay_col(lg, chunk, lambda p: (chunk - 1) - p)
            vd = (v.astype(F32) * kd).astype(BF16)
            state_sc[bb, h] = math.exp(lg * chunk) * state + jnp.dot(kt, vd, preferred_element_type=F32)
            y = y * lax.rsqrt(jnp.mean(y * y, axis=-1, keepdims=True) + NORM_EPS)
            gate = g_ref[bb, :, h * VALUE_DIM:(h + 1) * VALUE_DIM].astype(F32)
            o_ref[bb, :, h * VALUE_DIM:(h + 1) * VALUE_DIM] = (y * gate * jax.nn.sigmoid(gate)).astype(o_ref.dtype)


def _retention(rq, rk, rv, rg, km, vm):
    b, seq, _ = rq.shape
    chunk = RET_CHUNK
    assert b % RET_BATCH_ROWS == 0
    rows = lambda w: pl.BlockSpec((RET_BATCH_ROWS, chunk, w), lambda bi, c: (bi, c, 0))
    return pl.pallas_call(
        functools.partial(_ret_kernel, chunk=chunk),
        grid=(b // RET_BATCH_ROWS, seq // chunk),
        in_specs=[rows(QK_WIDTH), rows(QK_WIDTH), rows(RET_WIDTH), rows(RET_WIDTH),
                  pl.BlockSpec((META_ROWS, QK_WIDTH), lambda bi, c: (0, 0)),
                  pl.BlockSpec((META_ROWS, RET_WIDTH), lambda bi, c: (0, 0))],
        out_specs=rows(RET_WIDTH),
        out_shape=jax.ShapeDtypeStruct((b, seq, RET_WIDTH), BF16),
        scratch_shapes=[pltpu.VMEM((RET_BATCH_ROWS, RET_HEADS, LANES, VALUE_DIM), F32),
                        pltpu.VMEM((RET_HEADS, chunk, chunk), F32)],
        compiler_params=_params(("parallel", "arbitrary")),
        name="retention",
    )(rq, rk, rv, rg, km, vm)


def _route_kernel(x_ref, d_ref, r_ref, wo_ref, g_ref, wrh_ref, wrl_ref, brt_ref,
                  h_ref, xn_ref, pos_ref, col_ref, cnt_ref, *, tile):
    group_hots, group_gates = [], []
    for r0 in range(0, tile, ROUTE_GROUP):
        rows = slice(r0, r0 + ROUTE_GROUP)
        h = (x_ref[rows, :]
             + jnp.dot(d_ref[rows, :], wo_ref[0:DIFF_WIDTH, :], preferred_element_type=F32)
             + jnp.dot(r_ref[rows, :], wo_ref[DIFF_WIDTH:, :], preferred_element_type=F32))
        h_ref[rows, :] = h
        xn = h * lax.rsqrt(jnp.mean(h * h, axis=-1, keepdims=True) + NORM_EPS) * g_ref[...]
        xn_hi = xn.astype(BF16)
        xn_ref[rows, :] = xn_hi
        xn_lo = (xn - xn_hi.astype(F32)).astype(BF16)
        logits = (jnp.dot(xn_hi, wrh_ref[...], preferred_element_type=F32)
                  + jnp.dot(xn_hi, wrl_ref[...], preferred_element_type=F32)
                  + jnp.dot(xn_lo, wrh_ref[...], preferred_element_type=F32))
        logits = logits.T[0:N_EXPERTS, :] + brt_ref[...]
        expert = lax.broadcasted_iota(jnp.int32, logits.shape, 0)
        work = logits
        vals, hots = [], []
        for k in range(TOP_K):
            v = jnp.max(work, axis=0, keepdims=True)
            idx = jnp.min(jnp.where(work == v, expert, N_EXPERTS), axis=0, keepdims=True)
            hot = expert == idx
            work = jnp.where(hot, -jnp.inf, work)
            vals.append(v)
            hots.append(hot.astype(F32))
        exps = [jnp.exp(v - vals[0]) for v in vals]
        denom = exps[0] + exps[1] + exps[2] + exps[3]
        group_hots.append(hots)
        group_gates.append([e / denom for e in exps])
    hots = [jnp.concatenate([g[k] for g in group_hots], axis=1) for k in range(TOP_K)]
    gates = [jnp.concatenate([g[k] for g in group_gates], axis=1) for k in range(TOP_K)]

    chosen = hots[0] + hots[1] + hots[2] + hots[3]
    c = lax.broadcasted_iota(jnp.int32, (tile, tile), 0)
    r = lax.broadcasted_iota(jnp.int32, (tile, tile), 1)
    earlier = (c < r).astype(BF16)
    before = jnp.dot(chosen.astype(BF16), earlier, preferred_element_type=F32)
    cnt = jnp.sum(chosen, axis=1, keepdims=True)
    cnt_pad = jnp.ceil(cnt / SEG_ALIGN) * SEG_ALIGN
    er = lax.broadcasted_iota(jnp.int32, (N_EXPERTS, N_EXPERTS), 0)
    ec = lax.broadcasted_iota(jnp.int32, (N_EXPERTS, N_EXPERTS), 1)
    seg_start = jnp.dot((ec < er).astype(F32), jnp.broadcast_to(cnt_pad, (N_EXPERTS, LANES)),
                        preferred_element_type=F32, precision=lax.Precision.HIGHEST)[:, 0:1]
    base = seg_start + before
    pos = [jnp.sum(hot * base, axis=0, keepdims=True) for hot in hots]
    for k in range(TOP_K):
        pos_ref[k:k + 1, :] = pos[k].astype(jnp.int32)
    cnt_ref[...] = jnp.broadcast_to(cnt, (N_EXPERTS, LANES)).astype(jnp.int32)
    stacked = jnp.concatenate(pos + gates + [jnp.zeros((LANES - 2 * TOP_K, tile), F32)], axis=0)
    col_ref[...] = stacked.T


def _out_proj_and_route(x, diff, ret, wo, g, wr, brt):
    n = x.shape[0]
    tile = ROUTE_TILE
    rows = lambda w: pl.BlockSpec((tile, w), lambda i: (i, 0))
    full = lambda a, b: pl.BlockSpec((a, b), lambda i: (0, 0))
    wr = jnp.pad(wr, ((0, 0), (0, LANES - N_EXPERTS)))
    wr_hi = wr.astype(BF16)
    wr_lo = (wr - wr_hi.astype(F32)).astype(BF16)
    return pl.pallas_call(
        functools.partial(_route_kernel, tile=tile),
        grid=(n // tile,),
        in_specs=[rows(D_MODEL), rows(DIFF_WIDTH), rows(RET_WIDTH), full(D_MODEL, D_MODEL),
                  full(1, D_MODEL), full(D_MODEL, LANES), full(D_MODEL, LANES), full(N_EXPERTS, 1)],
        out_specs=[rows(D_MODEL), rows(D_MODEL),
                   pl.BlockSpec((TOP_K, tile), lambda i: (0, i)), rows(LANES),
                   pl.BlockSpec((None, N_EXPERTS, LANES), lambda i: (i, 0, 0))],
        out_shape=[jax.ShapeDtypeStruct((n, D_MODEL), F32), jax.ShapeDtypeStruct((n, D_MODEL), BF16),
                   jax.ShapeDtypeStruct((TOP_K, n), jnp.int32), jax.ShapeDtypeStruct((n, LANES), F32),
                   jax.ShapeDtypeStruct((n // tile, N_EXPERTS, LANES), jnp.int32)],
        compiler_params=_params(("parallel",)),
        name="out_proj_route",
    )(x, diff, ret, wo, g, wr_hi, wr_lo, brt)


def _split_gate_linear(w_ref, o_ref):
    group = 2 * LANES
    src = lax.broadcasted_iota(jnp.int32, (group, group), 0)
    dst = lax.broadcasted_iota(jnp.int32, (group, group), 1)
    select = (src == jnp.where(dst < LANES, 2 * dst, 2 * (dst - LANES) + 1)).astype(BF16)
    for c in range(2 * D_FF // group):
        part = jnp.dot(w_ref[:, c * group:(c + 1) * group].astype(BF16), select,
                       preferred_element_type=F32).astype(BF16)
        o_ref[:, c * LANES:(c + 1) * LANES] = part[:, :LANES]
        o_ref[:, D_FF + c * LANES:D_FF + (c + 1) * LANES] = part[:, LANES:]


def _segment_copy(local_buf, local_row, global_hbm, global_row, rows, sem, to_global):
    local_piece = local_buf.at[pl.ds(local_row, rows), :]
    global_piece = global_hbm.at[pl.ds(global_row, rows), :]
    if to_global:
        return pltpu.make_async_copy(local_piece, global_piece, sem)
    return pltpu.make_async_copy(global_piece, local_piece, sem)


def _piece_list(seg_local, seg_global, n_pieces, skip_rows, rows_per_piece, max_pieces):
    cum = jnp.cumsum(n_pieces, axis=1)
    j = jnp.arange(max_pieces, dtype=jnp.int32)
    owner = jnp.sum(cum[:, None, :] <= j[None, :, None], axis=2)
    onehot = owner[:, :, None] == jnp.arange(N_EXPERTS, dtype=jnp.int32)[None, None, :]
    pick = lambda t: jnp.sum(jnp.where(onehot, t[:, None, :], 0), axis=2)
    within = (j[None, :] - pick(cum - n_pieces)) * rows_per_piece
    return pick(seg_local + skip_rows) + within, pick(seg_global + skip_rows) + within, cum[:, -1]


def _start_segments(tables, tile_index, local_buf, global_hbm, sem, to_global):
    big_local, big_global, small_local, small_global, counts = tables

    def run(local_ref, global_ref, count, rows, stride):
        first = tile_index * stride

        def per_piece(p, carry):
            lo = pl.multiple_of(local_ref[first + p], SEG_ALIGN)
            gl = pl.multiple_of(global_ref[first + p], SEG_ALIGN)
            _segment_copy(local_buf, lo, global_hbm, gl, rows, sem, to_global).start()
            return carry

        lax.fori_loop(0, count, per_piece, 0)

    run(big_local, big_global, counts[3 * tile_index], BIG_PIECE, MAX_BIG_PIECES)
    run(small_local, small_global, counts[3 * tile_index + 1], SEG_ALIGN, MAX_SMALL_PIECES)


def _wait_segments(tables, tile_index, local_buf, global_hbm, sem, to_global):
    counts = tables[4]
    _segment_copy(local_buf, 0, global_hbm, 0, ROUTE_TILE * TOP_K, sem, to_global).wait()

    def per_pad(p, carry):
        _segment_copy(local_buf, 0, global_hbm, 0, SEG_ALIGN, sem, to_global).wait()
        return carry

    lax.fori_loop(0, counts[3 * tile_index + 2], per_pad, 0)


def _dispatch_kernel(big_local, big_global, small_local, small_global, counts, fill_ref, xn_ref, pos_ref,
                     xs_out, loc_buf, zero_buf, seg_sem, fill_sem, *, n_blocks, n_tiles):
    i = pl.program_id(0)
    slot = i % 2

    @pl.when(i == 0)
    def _():
        zero_buf[...] = jnp.zeros_like(zero_buf)

        def fill_copy(j):
            row = pl.multiple_of(j * MOE_TILE, MOE_TILE)
            return pltpu.make_async_copy(zero_buf, xs_out.at[pl.ds(row, MOE_TILE), :], fill_sem)

        def start(j, carry):
            @pl.when(fill_ref[j] > 0)
            def _():
                fill_copy(j).start()
            return carry

        def wait(j, carry):
            @pl.when(fill_ref[j] > 0)
            def _():
                fill_copy(j).wait()
            return carry

        lax.fori_loop(0, n_blocks, start, 0)
        lax.fori_loop(0, n_blocks, wait, 0)

    pos = pos_ref[...]
    row = lax.broadcasted_iota(jnp.int32, (LOC_ROWS, pos.shape[1]), 0)
    select = jnp.zeros(row.shape, F32)
    for k in range(TOP_K):
        select = jnp.where(row == pos[k:k + 1], 1.0, select)
    loc_buf[slot] = jnp.dot(select.astype(BF16), xn_ref[...], preferred_element_type=F32)

    tables = (big_local, big_global, small_local, small_global, counts)
    _start_segments(tables, i, loc_buf.at[slot], xs_out, seg_sem.at[slot], True)

    @pl.when(i > 0)
    def _():
        _wait_segments(tables, i - 1, loc_buf.at[1 - slot], xs_out, seg_sem.at[1 - slot], True)

    @pl.when(i == n_tiles - 1)
    def _():
        _wait_segments(tables, i, loc_buf.at[slot], xs_out, seg_sem.at[slot], True)


def _dispatch(tables, fill_flags, xn, pos, n_blocks):
    n = xn.shape[0]
    tile = ROUTE_TILE
    return pl.pallas_call(
        functools.partial(_dispatch_kernel, n_blocks=n_blocks, n_tiles=n // tile),
        grid_spec=pltpu.PrefetchScalarGridSpec(
            num_scalar_prefetch=6,
            grid=(n // tile,),
            in_specs=[pl.BlockSpec((tile, D_MODEL), lambda i, *_: (i, 0)),
                      pl.BlockSpec((TOP_K, tile), lambda i, *_: (0, i))],
            out_specs=pl.BlockSpec(memory_space=pl.ANY),
            scratch_shapes=[pltpu.VMEM((2, LOC_ROWS, D_MODEL), F32),
                            pltpu.VMEM((MOE_TILE, D_MODEL), F32),
                            pltpu.SemaphoreType.DMA((2,)), pltpu.SemaphoreType.DMA]),
        out_shape=jax.ShapeDtypeStruct((n_blocks * MOE_TILE, D_MODEL), F32),
        compiler_params=_params(("arbitrary",)),
        name="moe_dispatch",
    )(*tables, fill_flags, xn, pos)


def _expert_kernel(be_ref, nu_ref, xs_ref, wu_ref, bu_ref, wd_ref, bd_ref, ys_ref, wu_sc, wd_sc):
    j = pl.program_id(0)

    @pl.when((j == 0) | (be_ref[j] != be_ref[jnp.maximum(j - 1, 0)]))
    def _():
        _split_gate_linear(wu_ref, wu_sc)
        wd_sc[...] = wd_ref[...].astype(BF16)

    @pl.when(j < nu_ref[0])
    def _():
        x = xs_ref[...].astype(BF16)
        glu = jnp.dot(x, wu_sc[:, 0:D_FF], preferred_element_type=F32) + bu_ref[:, 0:D_FF]
        lin = jnp.dot(x, wu_sc[:, D_FF:], preferred_element_type=F32) + bu_ref[:, D_FF:]
        glu = jnp.minimum(glu, SWIGLU_LIMIT)
        lin = jnp.clip(lin, -SWIGLU_LIMIT, SWIGLU_LIMIT)
        act = glu * jax.nn.sigmoid(SWIGLU_ALPHA * glu) * (lin + 1.0)
        ys_ref[...] = jnp.dot(act.astype(BF16), wd_sc[...], preferred_element_type=F32) + bd_ref[...]

    @pl.when(j >= nu_ref[0])
    def _():
        ys_ref[...] = jnp.zeros_like(ys_ref)


def _expert_ffn(block_expert, n_used, xs, wu, bu, wd, bd):
    n_rows = xs.shape[0]
    tile = MOE_TILE
    rows = pl.BlockSpec((tile, D_MODEL), lambda j, be, nu: (jnp.minimum(j, nu[0] - 1), 0))
    per_expert = lambda a, b: pl.BlockSpec((None, a, b), lambda j, be, nu: (be[j], 0, 0))
    return pl.pallas_call(
        _expert_kernel,
        grid_spec=pltpu.PrefetchScalarGridSpec(
            num_scalar_prefetch=2,
            grid=(n_rows // tile,),
            in_specs=[rows, per_expert(D_MODEL, 2 * D_FF), per_expert(1, 2 * D_FF),
                      per_expert(D_FF, D_MODEL), per_expert(1, D_MODEL)],
            out_specs=pl.BlockSpec((tile, D_MODEL), lambda j, be, nu: (j, 0)),
            scratch_shapes=[pltpu.VMEM((D_MODEL, 2 * D_FF), BF16), pltpu.VMEM((D_FF, D_MODEL), BF16)]),
        out_shape=jax.ShapeDtypeStruct((n_rows, D_MODEL), F32),
        compiler_params=_params(("arbitrary",)),
        name="expert_ffn",
    )(block_expert, n_used, xs, wu, bu, wd, bd)


def _combine_kernel(big_local, big_global, small_local, small_global, counts, col_ref, h_ref, ys_hbm, g_ref,
                    o_ref, loc_buf, seg_sem, *, n_tiles):
    i = pl.program_id(0)
    slot = i % 2
    tables = (big_local, big_global, small_local, small_global, counts)

    def fetch(tile_index, buf_slot):
        _start_segments(tables, tile_index, loc_buf.at[buf_slot], ys_hbm, seg_sem.at[buf_slot], False)

    @pl.when(i == 0)
    def _():
        loc_buf[...] = jnp.zeros_like(loc_buf)
        fetch(0, 0)

    @pl.when(i + 1 < n_tiles)
    def _():
        fetch(i + 1, 1 - slot)

    _wait_segments(tables, i, loc_buf.at[slot], ys_hbm, seg_sem.at[slot], False)

    rows = loc_buf[slot].astype(BF16)
    lane = lax.broadcasted_iota(jnp.int32, (COMBINE_ROWS, LOC_ROWS), 1)
    for r in range(0, col_ref.shape[0], COMBINE_ROWS):
        col = col_ref[r:r + COMBINE_ROWS, :]
        weights = jnp.zeros((COMBINE_ROWS, LOC_ROWS), F32)
        for k in range(TOP_K):
            weights = jnp.where(lane == col[:, k:k + 1].astype(jnp.int32),
                                col[:, TOP_K + k:TOP_K + k + 1], weights)
        h = h_ref[r:r + COMBINE_ROWS, :] + jnp.dot(weights.astype(BF16), rows, preferred_element_type=F32)
        o_ref[r:r + COMBINE_ROWS, :] = (h * lax.rsqrt(jnp.mean(h * h, axis=-1, keepdims=True) + NORM_EPS)
                                        * g_ref[...])


def _combine(tables, col, h, ys, g):
    n = h.shape[0]
    tile = ROUTE_TILE
    return pl.pallas_call(
        functools.partial(_combine_kernel, n_tiles=n // tile),
        grid_spec=pltpu.PrefetchScalarGridSpec(
            num_scalar_prefetch=5,
            grid=(n // tile,),
            in_specs=[pl.BlockSpec((tile, LANES), lambda i, *_: (i, 0)),
                      pl.BlockSpec((tile, D_MODEL), lambda i, *_: (i, 0)),
                      pl.BlockSpec(memory_space=pl.ANY),
                      pl.BlockSpec((1, D_MODEL), lambda i, *_: (0, 0))],
            out_specs=pl.BlockSpec((tile, D_MODEL), lambda i, *_: (i, 0)),
            scratch_shapes=[pltpu.VMEM((2, LOC_ROWS, D_MODEL), F32), pltpu.SemaphoreType.DMA((2,))]),
        out_shape=jax.ShapeDtypeStruct((n, D_MODEL), F32),
        compiler_params=_params(("arbitrary",)),
        name="moe_combine",
    )(*tables, col, h, ys, g)


def kernel(x, meta_tokens, attn_norm_g, w_in, diff_lambda, diff_subln_g, w_out, ffn_norm_g,
           w_router, b_router, w_up, b_up, w_down, b_down, final_norm_g):
    b, seq, d = x.shape
    assert d == D_MODEL and seq % PROJ_TILE == 0 and seq % RET_CHUNK == 0 and seq % ATTN_TILE == 0
    assert w_in.shape[0] == 1, "one layer"
    n_tok = b * seq
    assert n_tok % ROUTE_TILE == 0

    w_in_b = w_in[0].astype(BF16)
    w_out_b = w_out[0].astype(BF16)
    w_up_b = w_up[0]
    b_up_s = jnp.concatenate([b_up[0][..., 0::2], b_up[0][..., 1::2]], axis=-1)[:, None, :]
    w_down_b = w_down[0]
    b_down_s = b_down[0][:, None, :]

    diff_inv_freq = ROPE_THETA ** (-jnp.arange(0, HEAD_DIM, 2, dtype=F32) / HEAD_DIM)
    ret_inv_freq = ROPE_THETA ** (-jnp.linspace(0.0, 1.0, HEAD_DIM // 2, dtype=F32))
    pos_x = jnp.arange(seq, dtype=F32) + N_META
    pos_m = jnp.arange(META_ROWS, dtype=F32) - (META_ROWS - N_META)
    tables_x = _rope_tables(pos_x, diff_inv_freq) + _rope_tables(pos_x, ret_inv_freq)
    tables_m = _rope_tables(pos_m, diff_inv_freq) + _rope_tables(pos_m, ret_inv_freq)

    g_attn = attn_norm_g[0][None, :]
    x2 = x.reshape(n_tok, D_MODEL)
    meta_rows = jnp.concatenate(
        [jnp.zeros((META_ROWS - N_META, D_MODEL), x.dtype), meta_tokens.astype(x.dtype)], axis=0)
    dq, dk, dv, rq, rk, rv, rg = _in_projection(x2, g_attn, w_in_b, tables_x, PROJ_TILE, seq // PROJ_TILE)
    _, dk_m, dv_m, _, rk_m, rv_m, _ = _in_projection(meta_rows, g_attn, w_in_b, tables_m, META_ROWS, 1)

    per_batch = lambda t: t.reshape(b, seq, t.shape[-1])
    diff_out = _diff_attention(per_batch(dq), per_batch(dk), per_batch(dv), dk_m, dv_m,
                               diff_lambda[0], diff_subln_g[0][:, None])
    ret_out = _retention(per_batch(rq), per_batch(rk), per_batch(rv), per_batch(rg), rk_m, rv_m)

    h, xn, pos, col, counts = _out_proj_and_route(
        x2, diff_out.reshape(n_tok, DIFF_WIDTH), ret_out.reshape(n_tok, RET_WIDTH),
        w_out_b, ffn_norm_g[0][None, :], w_router[0], b_router[0][:, None])

    n_tiles = n_tok // ROUTE_TILE
    n_blocks = -(-(n_tok * TOP_K + n_tiles * N_EXPERTS * (SEG_ALIGN - 1) + N_EXPERTS * (MOE_TILE - 1))
                 // MOE_TILE)
    seg_rows = (counts[:, :, 0] + SEG_ALIGN - 1) // SEG_ALIGN * SEG_ALIGN
    seg_local = jnp.cumsum(seg_rows, axis=1) - seg_rows
    blocks_per = (jnp.sum(seg_rows, axis=0) + MOE_TILE - 1) // MOE_TILE
    block_end = jnp.cumsum(blocks_per)
    group_start = (block_end - blocks_per) * MOE_TILE
    seg_global = group_start[None, :] + jnp.cumsum(seg_rows, axis=0) - seg_rows
    flat = lambda t: t.reshape(-1).astype(jnp.int32)
    n_big = seg_rows // BIG_PIECE
    big_local, big_global, big_total = _piece_list(seg_local, seg_global, n_big, 0, BIG_PIECE, MAX_BIG_PIECES)
    small_local, small_global, small_total = _piece_list(
        seg_local, seg_global, seg_rows % BIG_PIECE // SEG_ALIGN, n_big * BIG_PIECE, SEG_ALIGN, MAX_SMALL_PIECES)
    pad_pieces = (jnp.sum(seg_rows, axis=1) - ROUTE_TILE * TOP_K) // SEG_ALIGN
    tables = (flat(big_local), flat(big_global), flat(small_local), flat(small_global),
              flat(jnp.stack([big_total, small_total, pad_pieces], axis=1)))

    n_used = block_end[-1:].astype(jnp.int32)
    all_blocks = jnp.arange(n_blocks, dtype=jnp.int32)
    block_ids = jnp.minimum(all_blocks, n_used[0] - 1)
    block_expert = jnp.minimum(jnp.sum(block_end[None, :] <= block_ids[:, None], axis=1),
                               N_EXPERTS - 1).astype(jnp.int32)
    is_group_end = jnp.any((block_end[None, :] == all_blocks[:, None] + 1) & (blocks_per[None, :] > 0), axis=1)
    fill_flags = (is_group_end | (all_blocks >= n_used[0])).astype(jnp.int32)

    xs = _dispatch(tables, fill_flags, xn, pos, n_blocks)
    ys = _expert_ffn(block_expert, n_used, xs, w_up_b, b_up_s, w_down_b, b_down_s)
    out = _combine(tables, col, h, ys, final_norm_g[None, :])
    return out.reshape(b, seq, D_MODEL)
```
